```python
import math
import jax
import jax.numpy as jnp
from jax import lax
import numpy as np

D_MODEL = 2048
BATCH = 4
SEQ = 2048
DEPTH = 2
DEC_BATCH = 32
DEC_SEQ = 1
PAST_LEN = 8192
PAGE_SIZE = 128

MIX_WIDTH = D_MODEL
GROUP_WIDTH = MIX_WIDTH // 4
HA = 4
DKA = GROUP_WIDTH // HA
DVA = GROUP_WIDTH // HA
HB = 4
DB = GROUP_WIDTH // HB
SB_BIAS_INIT = -8.0
HC = 4
DVC = GROUP_WIDTH // HC
DKC = DVC // 2
GLA_RANK = 16
GLA_TAU = 16.0
HD = 8
PD = GROUP_WIDTH // HD
NG = 2
NSTATE = 128
CONV_W = 4
CHUNK = 64
SB_BLOCK = 128
PEER_HEADS = 8
N_KEYS = 128
N_EXPERTS = N_KEYS * N_KEYS
PK_DIM = 128
PEER_TOPK = 16
PEER_BLOCK = 128
EPS = 1e-6

GDN_CONV_DIM = 2 * HA * DKA + HA * DVA
SSD_CONV_DIM = HD * PD + 2 * NG * NSTATE
IN_SPLITS = (GDN_CONV_DIM, HA * DVA, HA, HA,
             HB * DB, HB * DB, HB * DB,
             HC * DKC, HC * DKC, HC * DVC, HC * DVC, GLA_RANK,
             SSD_CONV_DIM, HD * PD, HD)
IN_DIM = sum(IN_SPLITS)
SPLIT_POINTS = tuple(sum(IN_SPLITS[:i + 1]) for i in range(len(IN_SPLITS) - 1))

kernel_name = 'hymba_style_gdn_sb_gla_ssd_peer_step'


def rmsnorm(x, gain):
    xf = x.astype(jnp.float32)
    y = xf * lax.rsqrt(jnp.mean(xf * xf, axis=-1, keepdims=True) + EPS)
    return (y * gain.astype(jnp.float32)).astype(x.dtype)


def l2norm(x):
    xf = x.astype(jnp.float32)
    return xf * lax.rsqrt(jnp.sum(xf * xf, axis=-1, keepdims=True) + EPS)


def causal_conv(u, buf, w):
    L = u.shape[1]
    ext = jnp.concatenate([buf.astype(u.dtype), u], axis=1)
    out = ext[:, 0:L] * w[0]
    for i in range(1, CONV_W):
        out = out + ext[:, i:i + L] * w[i]
    return out, ext[:, L:]


def to_chunks(t, c):
    b, l, h = t.shape[:3]
    t = t.reshape((b, l // c, c, h) + t.shape[3:])
    return jnp.moveaxis(t, (1, 3), (0, 2))


def from_chunks(t):
    n, b, h, c = t.shape[:4]
    t = jnp.moveaxis(t, (0, 2), (1, 3))
    return t.reshape((b, n * c, h) + t.shape[4:])


def gdn_chunked(q, k, v, g, beta, S0):
    L = q.shape[1]
    C = math.gcd(L, CHUNK)
    incl = jnp.tril(jnp.ones((C, C), dtype=bool))
    strict = jnp.tril(jnp.ones((C, C), dtype=bool), -1)
    eye = jnp.eye(C, dtype=jnp.float32)

    def step(S, inp):
        qc, kc, vc, gc, bc = inp
        G = jnp.cumsum(gc, axis=-1)
        decay = jnp.exp(jnp.where(incl, G[..., :, None] - G[..., None, :], -jnp.inf))
        M = jnp.where(strict, decay, 0.0) * jnp.einsum('bhid,bhjd->bhij', kc, kc) * bc[..., :, None]
        lhs = eye + M
        W = lax.linalg.triangular_solve(lhs, (bc * jnp.exp(G))[..., None] * kc,
                                        left_side=True, lower=True, unit_diagonal=True)
        U = lax.linalg.triangular_solve(lhs, bc[..., None] * vc,
                                        left_side=True, lower=True, unit_diagonal=True)
        U = U - jnp.einsum('bhik,bhkv->bhiv', W, S)
        attn = jnp.einsum('bhid,bhjd->bhij', qc, kc) * decay
        o = (jnp.einsum('bhid,bhdv->bhiv', qc * jnp.exp(G)[..., None], S)
             + jnp.einsum('bhij,bhjv->bhiv', attn, U))
        g_last = G[..., -1:]
        S_new = (jnp.exp(g_last)[..., None] * S
                 + jnp.einsum('bhjd,bhjv->bhdv', kc * jnp.exp(g_last - G)[..., None], U))
        return S_new, o

    inputs = tuple(to_chunks(t, C) for t in (q, k, v, g, beta))
    S_final, o = lax.scan(step, S0, inputs)
    return from_chunks(o), S_final


def gla_chunked(q, k, v, loga, S0):
    L = q.shape[1]
    C = math.gcd(L, CHUNK)
    incl = jnp.tril(jnp.ones((C, C), dtype=bool))[:, :, None]

    def step(S, inp):
        qc, kc, vc, ac = inp
        Bc = jnp.cumsum(ac, axis=2)
        decay = jnp.exp(jnp.where(incl, Bc[:, :, :, None, :] - Bc[:, :, None, :, :], -jnp.inf))
        attn = jnp.einsum('bhid,bhjd,bhijd->bhij', qc, kc, decay)
        o = (jnp.einsum('bhid,bhdv->bhiv', qc * jnp.exp(Bc), S)
             + jnp.einsum('bhij,bhjv->bhiv', attn, vc))
        b_last = Bc[:, :, -1:]
        S_new = (jnp.exp(b_last[:, :, 0])[..., None] * S
                 + jnp.einsum('bhjd,bhjv->bhdv', kc * jnp.exp(b_last - Bc), vc))
        return S_new, o

    inputs = tuple(to_chunks(t, C) for t in (q, k, v, loga))
    S_final, o = lax.scan(step, S0, inputs)
    return from_chunks(o), S_final


def ssd_chunked(x, Bm, Cm, dt, A, S0):
    L = x.shape[1]
    C = math.gcd(L, CHUNK)
    incl = jnp.tril(jnp.ones((C, C), dtype=bool))

    def step(S, inp):
        xc, bc, cc, dtc = inp
        Lc = jnp.cumsum(dtc * A[None, :, None], axis=-1)
        decay = jnp.exp(jnp.where(incl, Lc[..., :, None] - Lc[..., None, :], -jnp.inf))
        scores = jnp.einsum('bhin,bhjn->bhij', cc, bc) * decay * dtc[..., None, :]
        y = (jnp.einsum('bhij,bhjp->bhip', scores, xc)
             + jnp.einsum('bhin,bhpn->bhip', cc * jnp.exp(Lc)[..., None], S))
        l_last = Lc[..., -1:]
        S_new = (jnp.exp(l_last)[..., None] * S
                 + jnp.einsum('bhjp,bhjn->bhpn', xc * (dtc * jnp.exp(l_last - Lc))[..., None], bc))
        return S_new, y

    inputs = tuple(to_chunks(t, C) for t in (x, Bm, Cm, dt))
    S_final, y = lax.scan(step, S0, inputs)
    return from_chunks(y), S_final


def stick_breaking(q, k, v, q_pos, bias):
    d = q.shape[-1]
    z = (jnp.einsum('bqhd,bkhd->bhqk', q.astype(jnp.float32), k.astype(jnp.float32)) * (d ** -0.5)
         + bias.astype(jnp.float32)[None, :, None, None])
    key_pos = jnp.arange(k.shape[1], dtype=jnp.int32)
    mask = key_pos[None, :] < q_pos[:, None]
    log_fail = jnp.where(mask, jax.nn.log_sigmoid(-z), 0.0)
    after = lax.cumsum(log_fail, axis=3, reverse=True) - log_fail
    A = jnp.where(mask, jnp.exp(jax.nn.log_sigmoid(z) + after), 0.0)
    return jnp.einsum('bhqk,bkhd->bqhd', A, v.astype(jnp.float32))


def sb_prompt(q, k, v, bias):
    b, L, h, d = q.shape
    nblk = L // SB_BLOCK
    qb = jnp.moveaxis(q.reshape(b, nblk, SB_BLOCK, h, d), 1, 0)
    pos = jnp.arange(L, dtype=jnp.int32).reshape(nblk, SB_BLOCK)
    ob = lax.map(lambda a: stick_breaking(a[0], k, v, a[1], bias), (qb, pos))
    return jnp.moveaxis(ob, 0, 1).reshape(b, L, h, d)


def peer(h, w_query, sub_keys, expert_u, expert_v):
    b, L, D = h.shape
    T = b * L
    xt = h.reshape(T, D)
    q = (xt @ w_query).astype(jnp.float32).reshape(T, PEER_HEADS, 2, PK_DIM)
    s = jnp.einsum('thpd,pkd->thpk', q, sub_keys.astype(jnp.float32))
    sv, si = lax.top_k(s, PEER_TOPK)
    cand = sv[:, :, 0, :, None] + sv[:, :, 1, None, :]
    cv, ci = lax.top_k(cand.reshape(T, PEER_HEADS, PEER_TOPK * PEER_TOPK), PEER_TOPK)
    e1 = jnp.take_along_axis(si[:, :, 0], ci // PEER_TOPK, axis=-1)
    e2 = jnp.take_along_axis(si[:, :, 1], ci % PEER_TOPK, axis=-1)
    idx = (e1 * N_KEYS + e2).reshape(T, PEER_HEADS * PEER_TOPK)
    gates = jax.nn.softmax(cv, axis=-1).reshape(T, PEER_HEADS * PEER_TOPK)
    n_blk = -(-T // PEER_BLOCK)
    pad = n_blk * PEER_BLOCK - T
    xb = jnp.pad(xt, ((0, pad), (0, 0))).reshape(n_blk, PEER_BLOCK, D)
    ib = jnp.pad(idx, ((0, pad), (0, 0))).reshape(n_blk, PEER_BLOCK, -1)
    gb = jnp.pad(gates, ((0, pad), (0, 0))).reshape(n_blk, PEER_BLOCK, -1)

    def block(args):
        xx, ii, gg = args
        act = jax.nn.gelu(jnp.einsum('td,ted->te', xx, expert_u[ii]).astype(jnp.float32), approximate=False)
        return jnp.einsum('te,ted->td', gg * act, expert_v[ii].astype(jnp.float32))

    out = lax.map(block, (xb, ib, gb)).reshape(n_blk * PEER_BLOCK, D)[:T]
    return out.reshape(b, L, D).astype(h.dtype)


def token_mixers(h, p, gdn_S, gdn_buf, gla_S, ssd_S, ssd_buf, past_k, past_v):
    f32 = jnp.float32
    b, L, _ = h.shape
    proj = h @ p['w_in']
    (qkvA, zA, aA, bA, qB, kB, vB, qC, kC, vC, rC, gC, xbcD, zD, dtD) = jnp.split(proj, SPLIT_POINTS, axis=-1)

    qkvA, new_gdn_buf = causal_conv(qkvA, gdn_buf, p['gdn_conv_w'])
    qkvA = jax.nn.silu(qkvA)
    qA, kA, vA = jnp.split(qkvA, (HA * DKA, 2 * HA * DKA), axis=-1)
    qA = l2norm(qA.reshape(b, L, HA, DKA)) * (DKA ** -0.5)
    kA = l2norm(kA.reshape(b, L, HA, DKA))
    vA = vA.reshape(b, L, HA, DVA).astype(f32)
    gA = -jnp.exp(p['gdn_A_log'].astype(f32)) * jax.nn.softplus(aA.astype(f32) + p['gdn_dt_bias'])
    betaA = jax.nn.sigmoid(bA.astype(f32))
    oA, new_gdn_S = gdn_chunked(qA, kA, vA, gA, betaA, gdn_S.astype(f32))
    oA = rmsnorm(oA, p['gdn_norm_g']) * jax.nn.silu(zA.reshape(b, L, HA, DVA).astype(f32))

    qB = qB.reshape(b, L, HB, DB)
    kB = kB.reshape(b, L, HB, DB)
    vB = vB.reshape(b, L, HB, DB)
    if past_k is None:
        oB = sb_prompt(qB, kB, vB, p['sb_bias'])
    else:
        keys = jnp.concatenate([past_k.astype(kB.dtype), kB], axis=1)
        vals = jnp.concatenate([past_v.astype(vB.dtype), vB], axis=1)
        q_pos = past_k.shape[1] + jnp.arange(L, dtype=jnp.int32)
        oB = stick_breaking(qB, keys, vals, q_pos, p['sb_bias'])

    qC = qC.reshape(b, L, HC, DKC).astype(f32) * (DKC ** -0.5)
    kC = kC.reshape(b, L, HC, DKC).astype(f32)
    vC = vC.reshape(b, L, HC, DVC).astype(f32)
    logaC = jax.nn.log_sigmoid(gC.astype(f32) @ p['gla_w2'].astype(f32) + p['gla_b2']) / GLA_TAU
    oC, new_gla_S = gla_chunked(qC, kC, vC, logaC.reshape(b, L, HC, DKC), gla_S.astype(f32))
    oC = rmsnorm(oC, p['gla_norm_g']) * jax.nn.silu(rC.reshape(b, L, HC, DVC).astype(f32))

    xbcD, new_ssd_buf = causal_conv(xbcD, ssd_buf, p['ssd_conv_w'])
    xbcD = jax.nn.silu(xbcD + p['ssd_conv_b']).astype(f32)
    xD, BD, CD = jnp.split(xbcD, (HD * PD, HD * PD + NG * NSTATE), axis=-1)
    xD = xD.reshape(b, L, HD, PD)
    BD = jnp.repeat(BD.reshape(b, L, NG, NSTATE), HD // NG, axis=2)
    CD = jnp.repeat(CD.reshape(b, L, NG, NSTATE), HD // NG, axis=2)
    dt = jax.nn.softplus(dtD.astype(f32) + p['ssd_dt_bias'])
    A = -jnp.exp(p['ssd_A_log'].astype(f32))
    yD, new_ssd_S = ssd_chunked(xD, BD, CD, dt, A, ssd_S.astype(f32))
    yD = (yD + p['ssd_D'][:, None] * xD) * jax.nn.silu(zD.reshape(b, L, HD, PD).astype(f32))
    yD = rmsnorm(yD.reshape(b, L, NG, HD * PD // NG), p['ssd_norm_g'].reshape(NG, HD * PD // NG))

    mixed = jnp.concatenate([oA.reshape(b, L, -1), oB.reshape(b, L, -1), oC.reshape(b, L, -1),
                             yD.reshape(b, L, -1)], axis=-1).astype(h.dtype)
    out = mixed @ p['w_out']
    return out, (kB, vB, new_gdn_S, new_gdn_buf, new_gla_S, new_ssd_S, new_ssd_buf)


def layer(x, c, p, gdn_S, gdn_buf, gla_S, ssd_S, ssd_buf, past_k, past_v):
    mod = jax.nn.silu(c) @ p['ada_w'] + p['ada_b']
    sh1, sc1, g1, sh2, sc2, g2 = jnp.split(mod[:, None, :], 6, axis=-1)
    h = rmsnorm(x, p['norm1_g']) * (1 + sc1) + sh1
    mix, new_states = token_mixers(h, p, gdn_S, gdn_buf, gla_S, ssd_S, ssd_buf, past_k, past_v)
    x = x + g1 * mix
    h = rmsnorm(x, p['norm2_g']) * (1 + sc2) + sh2
    x = x + g2 * peer(h, p['peer_w_query'], p['peer_sub_keys'], p['peer_u'], p['peer_v'])
    return x, new_states


def setup_inputs(seed: int = 0) -> dict:
    key = jax.random.key(seed)
    ks = list(jax.random.split(key, 48))
    f32 = jnp.float32

    def nrm(shape, scale):
        return scale * jax.random.normal(ks.pop(), shape, f32)

    def gain(shape):
        return 1.0 + nrm(shape, 0.1)

    def a_log(shape):
        return jnp.log(jax.random.uniform(ks.pop(), shape, f32, 1.0, 16.0))

    def dt_bias(shape):
        dt = jnp.exp(jax.random.uniform(ks.pop(), shape, f32, math.log(1e-3), math.log(1e-1)))
        return dt + jnp.log(-jnp.expm1(-dt))

    n_pages = PAST_LEN // PAGE_SIZE
    n_used = DEC_BATCH * n_pages
    n_pool = n_used + max(1, n_used // 4)
    page_table = jax.random.permutation(ks.pop(), n_pool)[:n_used].reshape(DEC_BATCH, n_pages).astype(jnp.int32)
    return {
        'x_prompt': nrm((BATCH, SEQ, D_MODEL), 1.0),
        'x_sample': nrm((DEC_BATCH, DEC_SEQ, D_MODEL), 1.0),
        'cache_k': nrm((DEPTH, n_pool, PAGE_SIZE, HB, DB), 1.0),
        'cache_v': nrm((DEPTH, n_pool, PAGE_SIZE, HB, DB), 1.0),
        'state_gdn': nrm((DEPTH, DEC_BATCH, HA, DKA, DVA), 0.1),
        'state_gdn_conv': nrm((DEPTH, DEC_BATCH, CONV_W - 1, GDN_CONV_DIM), 1.0),
        'state_gla': nrm((DEPTH, DEC_BATCH, HC, DKC, DVC), 0.3),
        'state_ssd': nrm((DEPTH, DEC_BATCH, HD, PD, NSTATE), 0.1),
        'state_ssd_conv': nrm((DEPTH, DEC_BATCH, CONV_W - 1, SSD_CONV_DIM), 1.0),
        'page_table': page_table,
        'c_prompt': nrm((BATCH, D_MODEL), 1.0),
        'c_sample': nrm((DEC_BATCH, D_MODEL), 1.0),
        'ada_w': nrm((DEPTH, D_MODEL, 6 * D_MODEL), 0.5 * D_MODEL ** -0.5),
        'ada_b': nrm((DEPTH, 6 * D_MODEL), 0.01),
        'norm1_g': gain((DEPTH, D_MODEL)),
        'norm2_g': gain((DEPTH, D_MODEL)),
        'w_in': nrm((DEPTH, D_MODEL, IN_DIM), D_MODEL ** -0.5),
        'w_out': nrm((DEPTH, MIX_WIDTH, D_MODEL), MIX_WIDTH ** -0.5),
        'gdn_conv_w': nrm((DEPTH, CONV_W, GDN_CONV_DIM), 0.5),
        'gdn_A_log': a_log((DEPTH, HA)),
        'gdn_dt_bias': dt_bias((DEPTH, HA)),
        'gdn_norm_g': gain((DEPTH, DVA)),
        'sb_bias': SB_BIAS_INIT + nrm((DEPTH, HB), 0.1),
        'gla_w2': nrm((DEPTH, GLA_RANK, HC * DKC), GLA_RANK ** -0.5),
        'gla_b2': nrm((DEPTH, HC * DKC), 0.1),
        'gla_norm_g': gain((DEPTH, DVC)),
        'ssd_conv_w': nrm((DEPTH, CONV_W, SSD_CONV_DIM), 0.5),
        'ssd_conv_b': nrm((DEPTH, SSD_CONV_DIM), 0.02),
        'ssd_A_log': a_log((DEPTH, HD)),
        'ssd_dt_bias': dt_bias((DEPTH, HD)),
        'ssd_D': gain((DEPTH, HD)),
        'ssd_norm_g': gain((DEPTH, HD * PD)),
        'peer_w_query': nrm((DEPTH, D_MODEL, PEER_HEADS * 2 * PK_DIM), D_MODEL ** -0.5),
        'peer_sub_keys': nrm((DEPTH, 2, N_KEYS, PK_DIM), PK_DIM ** -0.5),
        'peer_u': nrm((DEPTH, N_EXPERTS, D_MODEL), D_MODEL ** -0.5),
        'peer_v': nrm((DEPTH, N_EXPERTS, D_MODEL), 0.2),
        'final_norm_g': gain((D_MODEL,)),
    }


def reference(x_prompt, x_sample, cache_k, cache_v, state_gdn, state_gdn_conv, state_gla, state_ssd,
              state_ssd_conv, page_table, c_prompt, c_sample, ada_w, ada_b, norm1_g, norm2_g, w_in, w_out,
              gdn_conv_w, gdn_A_log, gdn_dt_bias, gdn_norm_g, sb_bias, gla_w2, gla_b2, gla_norm_g, ssd_conv_w,
              ssd_conv_b, ssd_A_log, ssd_dt_bias, ssd_D, ssd_norm_g, peer_w_query, peer_sub_keys, peer_u,
              peer_v, final_norm_g):
    f32 = jnp.float32
    n_pages = PAST_LEN // PAGE_SIZE
    bsz = x_prompt.shape[0]
    dec_b = x_sample.shape[0]
    xp, xs = x_prompt, x_sample
    sp, ss = [], []
    for l in range(DEPTH):
        p = {'ada_w': ada_w[l], 'ada_b': ada_b[l], 'norm1_g': norm1_g[l], 'norm2_g': norm2_g[l],
             'w_in': w_in[l], 'w_out': w_out[l], 'gdn_conv_w': gdn_conv_w[l], 'gdn_A_log': gdn_A_log[l],
             'gdn_dt_bias': gdn_dt_bias[l], 'gdn_norm_g': gdn_norm_g[l], 'sb_bias': sb_bias[l],
             'gla_w2': gla_w2[l], 'gla_b2': gla_b2[l], 'gla_norm_g': gla_norm_g[l], 'ssd_conv_w': ssd_conv_w[l],
             'ssd_conv_b': ssd_conv_b[l], 'ssd_A_log': ssd_A_log[l], 'ssd_dt_bias': ssd_dt_bias[l],
             'ssd_D': ssd_D[l], 'ssd_norm_g': ssd_norm_g[l], 'peer_w_query': peer_w_query[l],
             'peer_sub_keys': peer_sub_keys[l], 'peer_u': peer_u[l], 'peer_v': peer_v[l]}
        xp, st_p = layer(xp, c_prompt, p,
                         jnp.zeros((bsz, HA, DKA, DVA), f32),
                         jnp.zeros((bsz, CONV_W - 1, GDN_CONV_DIM), xp.dtype),
                         jnp.zeros((bsz, HC, DKC, DVC), f32),
                         jnp.zeros((bsz, HD, PD, NSTATE), f32),
                         jnp.zeros((bsz, CONV_W - 1, SSD_CONV_DIM), xp.dtype),
                         None, None)
        past_k = cache_k[l][page_table].reshape(dec_b, n_pages * PAGE_SIZE, HB, DB)
        past_v = cache_v[l][page_table].reshape(dec_b, n_pages * PAGE_SIZE, HB, DB)
        xs, st_s = layer(xs, c_sample, p, state_gdn[l], state_gdn_conv[l], state_gla[l], state_ssd[l],
                         state_ssd_conv[l], past_k, past_v)
        sp.append(st_p)
        ss.append(st_s)
    y_prompt = rmsnorm(xp, final_norm_g)
    y_sample = rmsnorm(xs, final_norm_g)
    k_prompt = jnp.stack([s[0] for s in sp], axis=0)
    v_prompt = jnp.stack([s[1] for s in sp], axis=0)
    gdn_prompt = jnp.stack([s[2] for s in sp], axis=0)
    gdn_conv_prompt = jnp.stack([s[3] for s in sp], axis=0)
    gla_prompt = jnp.stack([s[4] for s in sp], axis=0)
    ssd_prompt = jnp.stack([s[5] for s in sp], axis=0)
    ssd_conv_prompt = jnp.stack([s[6] for s in sp], axis=0)
    k_sample = jnp.stack([s[0] for s in ss], axis=0)
    v_sample = jnp.stack([s[1] for s in ss], axis=0)
    gdn_sample = jnp.stack([s[2] for s in ss], axis=0)
    gdn_conv_sample = jnp.stack([s[3] for s in ss], axis=0)
    gla_sample = jnp.stack([s[4] for s in ss], axis=0)
    ssd_sample = jnp.stack([s[5] for s in ss], axis=0)
    ssd_conv_sample = jnp.stack([s[6] for s in ss], axis=0)
    return (y_prompt, y_sample, k_prompt, v_prompt, k_sample, v_sample, gdn_prompt, gdn_sample,
            gdn_conv_prompt, gdn_conv_sample, gla_prompt, gla_sample, ssd_prompt, ssd_sample,
            ssd_conv_prompt, ssd_conv_sample)
```

```python
import functools
import math

import jax
import jax.numpy as jnp
import numpy as np
from jax import lax
from jax.experimental import pallas as pl
from jax.experimental.pallas import tpu as pltpu

F32 = jnp.float32
BF16 = jnp.bfloat16
HIGHEST = lax.Precision.HIGHEST

D_MODEL = 2048
DEPTH = 2
PAGE_SIZE = 128
GROUP_WIDTH = D_MODEL // 4
HA, DKA, DVA = 4, 128, 128
HB, DB = 4, 128
HC, DKC, DVC = 4, 64, 128
GLA_RANK = 16
GLA_TAU = 16.0
HD, PD, NG, NSTATE = 8, 64, 2, 128
CONV_W = 4
CHUNK = 64
SB_BLOCK = 128
PEER_HEADS = 8
N_KEYS = 128
N_EXPERTS = N_KEYS * N_KEYS
PK_DIM = 128
PEER_TOPK = 16
EPS = 1e-6
GDN_CONV_DIM = 2 * HA * DKA + HA * DVA
SSD_CONV_DIM = HD * PD + 2 * NG * NSTATE

OFF_QKVA = 0
OFF_ZA = 1536
OFF_QB = 2048
OFF_KB = 2560
OFF_VB = 3072
OFF_QC = 3584
OFF_KC = 3840
OFF_VC = 4096
OFF_RC = 4608
OFF_XBC = 5120
OFF_ZD = 6144
OFF_SMALL = 6656
SM_A, SM_B, SM_G, SM_DT = 0, 4, 8, 24
N_PROJ = 7168

VMEM_LIMIT = 56 * 1024 * 1024


def _cparams(sem, vmem=None):
    return pltpu.CompilerParams(dimension_semantics=sem, vmem_limit_bytes=vmem)


def _softplus(x):
    return jnp.maximum(x, 0.0) + jnp.log1p(jnp.exp(-jnp.abs(x)))


def _silu(x):
    return x * jax.nn.sigmoid(x)


def _dot(a, b, precision=None):
    return jnp.dot(a, b, precision=precision, preferred_element_type=F32)


def _dot_nt(a, b, precision=None):
    return lax.dot_general(a, b, (((1,), (1,)), ((), ())), precision=precision,
                           preferred_element_type=F32)


def _split3(x):
    hi = x.astype(BF16)
    r1 = x - hi.astype(F32)
    mid = r1.astype(BF16)
    lo = (r1 - mid.astype(F32)).astype(BF16)
    return hi, mid, lo


def _tri_dot_left(tri_bf16, x):
    hi, mid, lo = _split3(x)
    return (_dot(tri_bf16, hi) + _dot(tri_bf16, mid)) + _dot(tri_bf16, lo)


def _tri_dot_right(x, tri_bf16):
    hi, mid, lo = _split3(x)
    return (_dot(hi, tri_bf16) + _dot(mid, tri_bf16)) + _dot(lo, tri_bf16)


def _iota2(shape, dim):
    return lax.broadcasted_iota(jnp.int32, shape, dim)


def _col_from_row(r):
    n = r.shape[1]
    eye = _iota2((n, n), 0) == _iota2((n, n), 1)
    return jnp.sum(jnp.where(eye, jnp.broadcast_to(r, (n, n)), 0.0), axis=1, keepdims=True)


def _ada_kernel(c_ref, w_ref, b_ref, o_ref):
    c = c_ref[...]
    o_ref[0] = _dot(_silu(c), w_ref[0]) + b_ref[0]


def ada_mod(c_all, ada_w, ada_b):
    r = c_all.shape[0]
    n = ada_w.shape[2]
    tn = 1024
    return pl.pallas_call(
        _ada_kernel,
        grid=(DEPTH, n // tn),
        in_specs=[pl.BlockSpec((r, D_MODEL), lambda l, j: (0, 0)),
                  pl.BlockSpec((1, D_MODEL, tn), lambda l, j: (l, 0, j)),
                  pl.BlockSpec((1, 1, tn), lambda l, j: (l, 0, j))],
        out_specs=pl.BlockSpec((1, r, tn), lambda l, j: (l, 0, j)),
        out_shape=jax.ShapeDtypeStruct((DEPTH, r, n), F32),
        compiler_params=_cparams(("parallel", "parallel"), VMEM_LIMIT),
        name="ada_mod",
    )(c_all, ada_w, ada_b.reshape(DEPTH, 1, n))


def _in_proj_kernel(x_ref, g_ref, sc_ref, sh_ref, w_ref, o_ref, h_scr):
    @pl.when(pl.program_id(1) == 0)
    def _():
        x = x_ref[...]
        y = x * lax.rsqrt(jnp.mean(x * x, axis=-1, keepdims=True) + EPS) * g_ref[...]
        h_scr[...] = (y * (1.0 + sc_ref[0]) + sh_ref[0]).astype(BF16)

    o_ref[...] = _dot(h_scr[...], w_ref[...])


def _row_tile(t, cap, mod, rows_per_batch):
    return min(t, cap, rows_per_batch) if mod.shape[1] == 1 else min(t, cap)


def _mod_spec(mod, tm, rows_per_batch):
    nb, r, d = mod.shape
    if r == 1:
        return pl.BlockSpec((1, 1, d), lambda i, *_: ((i * tm) // rows_per_batch, 0, 0))
    return pl.BlockSpec((1, r, d), lambda i, *_: (0, 0, 0))


def in_proj(x, g, sc, sh, w, rows_per_batch):
    t, d = x.shape
    n = w.shape[1]
    tm = _row_tile(t, 1024, sc, rows_per_batch)
    tn = 512
    return pl.pallas_call(
        _in_proj_kernel,
        grid=(t // tm, n // tn),
        in_specs=[pl.BlockSpec((tm, d), lambda i, j: (i, 0)),
                  pl.BlockSpec((1, d), lambda i, j: (0, 0)),
                  _mod_spec(sc, tm, rows_per_batch),
                  _mod_spec(sh, tm, rows_per_batch),
                  pl.BlockSpec((d, tn), lambda i, j: (0, j))],
        out_specs=pl.BlockSpec((tm, tn), lambda i, j: (i, j)),
        out_shape=jax.ShapeDtypeStruct((t, n), F32),
        scratch_shapes=[pltpu.VMEM((tm, d), BF16)],
        compiler_params=_cparams(("parallel", "arbitrary"), VMEM_LIMIT),
        name="in_proj",
    )(x, g.reshape(1, d), sc, sh, w)


def _causal_conv_chunk(u, prev_ref, w_ref):
    c = u.shape[0]
    rows = _iota2(u.shape, 0)
    prev = prev_ref[...]
    out = u * w_ref[CONV_W - 1:CONV_W, :]
    for k in range(1, CONV_W):
        shifted = jnp.where(rows >= k, pltpu.roll(u, k, 0), pltpu.roll(prev, k, 0))
        out = out + shifted * w_ref[CONV_W - 1 - k:CONV_W - k, :]
    prev_ref[...] = u
    del c
    return out


def _inv_unit_lower(m):
    c = m.shape[0]
    eye = (_iota2((c, c), 0) == _iota2((c, c), 1)).astype(F32)
    x = -m
    p = eye + x
    steps = int(math.ceil(math.log2(c))) - 1
    for _ in range(steps):
        x = _dot(x, x, HIGHEST)
        p = p + _dot(p, x, HIGHEST)
    return p


def _gdn_prompt_kernel(qkv_ref, z_ref, sm_ref, cw_ref, gcoef_ref, dtb_ref, ng_ref,
                       o_ref, s_out_ref, s_scr, prev_scr):
    ci = pl.program_id(1)
    nc = pl.num_programs(1)

    @pl.when(ci == 0)
    def _():
        s_scr[...] = jnp.zeros_like(s_scr)
        prev_scr[...] = jnp.zeros_like(prev_scr)

    c = CHUNK
    x = _silu(_causal_conv_chunk(qkv_ref[...], prev_scr, cw_ref))
    sm = sm_ref[:, 0:128]
    g = gcoef_ref[...] * _softplus(sm + dtb_ref[...])
    beta = jax.nn.sigmoid(sm)
    ri = _iota2((c, c), 0)
    cj = _iota2((c, c), 1)
    incl = cj <= ri
    strict = cj < ri
    tril = incl.astype(BF16)
    gcum = _tri_dot_left(tril, g)
    gcum_t = gcum.T
    z = z_ref[...]
    for h in range(HA):
        q = x[:, h * DKA:(h + 1) * DKA]
        k = x[:, HA * DKA + h * DKA:HA * DKA + (h + 1) * DKA]
        v = x[:, 2 * HA * DKA + h * DVA:2 * HA * DKA + (h + 1) * DVA]
        q = q * lax.rsqrt(jnp.sum(q * q, axis=-1, keepdims=True) + EPS) * (DKA ** -0.5)
        k = k * lax.rsqrt(jnp.sum(k * k, axis=-1, keepdims=True) + EPS)
        gc = gcum[:, SM_A + h:SM_A + h + 1]
        gr = gcum_t[SM_A + h:SM_A + h + 1, :]
        bc = beta[:, SM_B + h:SM_B + h + 1]
        decay = jnp.exp(jnp.where(incl, gc - gr, -jnp.inf))
        eg = jnp.exp(gc)
        kk = _dot_nt(k, k, HIGHEST)
        m = jnp.where(strict, decay, 0.0) * kk * bc
        tinv = _inv_unit_lower(m)
        w = _dot(tinv, (bc * eg) * k, HIGHEST)
        u = _dot(tinv, bc * v, HIGHEST)
        s = s_scr[h]
        u = u - _dot(w, s, HIGHEST)
        attn = _dot_nt(q, k, HIGHEST) * decay
        o = _dot(q * eg, s, HIGHEST) + _dot(attn, u, HIGHEST)
        g_last = gcum[c - 1:c, SM_A + h:SM_A + h + 1]
        kd = k * jnp.exp(g_last - gc)
        s_scr[h] = jnp.exp(g_last) * s + _dot(kd.T, u, HIGHEST)
        on = o * lax.rsqrt(jnp.mean(o * o, axis=-1, keepdims=True) + EPS) * ng_ref[...]
        o_ref[:, h * DVA:(h + 1) * DVA] = (on * _silu(z[:, h * DVA:(h + 1) * DVA])).astype(BF16)

    @pl.when(ci == nc - 1)
    def _():
        s_out_ref[0] = s_scr[...]


def gdn_prompt(proj, bsz, seq, conv_w, gcoef, dtb, norm_g):
    nc = seq // CHUNK
    c = CHUNK
    row = lambda b, i: b * nc + i
    return pl.pallas_call(
        _gdn_prompt_kernel,
        grid=(bsz, nc),
        in_specs=[pl.BlockSpec((c, GDN_CONV_DIM), lambda b, i: (row(b, i), OFF_QKVA // GDN_CONV_DIM)),
                  pl.BlockSpec((c, 512), lambda b, i: (row(b, i), OFF_ZA // 512)),
                  pl.BlockSpec((c, 512), lambda b, i: (row(b, i), OFF_SMALL // 512)),
                  pl.BlockSpec((CONV_W, GDN_CONV_DIM), lambda b, i: (0, 0)),
                  pl.BlockSpec((1, 128), lambda b, i: (0, 0)),
                  pl.BlockSpec((1, 128), lambda b, i: (0, 0)),
                  pl.BlockSpec((1, DVA), lambda b, i: (0, 0))],
        out_specs=[pl.BlockSpec((c, HA * DVA), lambda b, i: (row(b, i), 0)),
                   pl.BlockSpec((1, HA, DKA, DVA), lambda b, i: (b, 0, 0, 0))],
        out_shape=[jax.ShapeDtypeStruct((bsz * seq, HA * DVA), BF16),
                   jax.ShapeDtypeStruct((bsz, HA, DKA, DVA), F32)],
        scratch_shapes=[pltpu.VMEM((HA, DKA, DVA), F32), pltpu.VMEM((c, GDN_CONV_DIM), F32)],
        compiler_params=_cparams(("parallel", "arbitrary")),
        name="gdn_prompt",
    )(proj, proj, proj, conv_w, gcoef, dtb, norm_g.reshape(1, DVA))


def _sb_prompt_kernel(bias_ref, q_ref, k_ref, v_ref, o_ref):
    h = pl.program_id(1)
    qi = pl.program_id(2)
    blk = SB_BLOCK
    bias = bias_ref[h]
    q = q_ref[...].astype(BF16)
    ri = _iota2((blk, blk), 0)
    cj = _iota2((blk, blk), 1)
    tri = (ri > cj).astype(BF16)

    def body(jj, carry):
        acc, run = carry
        j = qi - jj
        off = pl.multiple_of(j * blk, blk)
        kb = k_ref[pl.ds(off, blk), :].astype(BF16)
        vb = v_ref[pl.ds(off, blk), :].astype(BF16)
        z = _dot_nt(q, kb) * (DB ** -0.5) + bias
        mask = (j * blk + cj) < (qi * blk + ri)
        sp = _softplus(z)
        lf = jnp.where(mask, -sp, 0.0)
        hi = lf.astype(BF16)
        lo = (lf - hi.astype(F32)).astype(BF16)
        after = (_dot(hi, tri) + _dot(lo, tri)) + run
        a = jnp.where(mask, jnp.exp((z - sp) + after), 0.0)
        acc = acc + _dot(a.astype(BF16), vb)
        run = run + jnp.sum(lf, axis=-1, keepdims=True)
        return acc, run

    acc, _ = lax.fori_loop(0, qi + 1, body,
                           (jnp.zeros((blk, DB), F32), jnp.zeros((blk, 1), F32)))
    o_ref[...] = acc.astype(BF16)


def sb_prompt(proj, bsz, seq, bias):
    nq = seq // SB_BLOCK
    blk = SB_BLOCK
    return pl.pallas_call(
        _sb_prompt_kernel,
        grid=(bsz, HB, nq),
        in_specs=[pl.BlockSpec(memory_space=pltpu.SMEM),
                  pl.BlockSpec((blk, DB), lambda b, h, i: (b * nq + i, OFF_QB // DB + h)),
                  pl.BlockSpec((seq, DB), lambda b, h, i: (b, OFF_KB // DB + h)),
                  pl.BlockSpec((seq, DB), lambda b, h, i: (b, OFF_VB // DB + h))],
        out_specs=pl.BlockSpec((blk, DB), lambda b, h, i: (b * nq + i, h)),
        out_shape=jax.ShapeDtypeStruct((bsz * seq, HB * DB), BF16),
        compiler_params=_cparams(("parallel", "parallel", "arbitrary")),
        name="sb_prompt",
    )(bias, proj, proj, proj)


GLA_SUB = 16


def _gla_prompt_kernel(q_ref, k_ref, v_ref, r_ref, sm_ref, w2_ref, b2_ref, ng_ref,
                       o_ref, s_out_ref, s_scr):
    ci = pl.program_id(1)
    nc = pl.num_programs(1)

    @pl.when(ci == 0)
    def _():
        s_scr[...] = jnp.zeros_like(s_scr)

    c = CHUNK
    sm = sm_ref[:, 0:128]
    pre = _dot(sm, w2_ref[...], HIGHEST) + b2_ref[...]
    loga = -_softplus(-pre) * (1.0 / GLA_TAU)
    ri = _iota2((c, c), 0)
    cj = _iota2((c, c), 1)
    incl = cj <= ri
    tril = incl.astype(BF16)
    bcum = _tri_dot_left(tril, loga)
    bcum_t = bcum.T
    qa = q_ref[...]
    ka = k_ref[...]
    va = v_ref[...]
    ra = r_ref[...]
    jrow = _iota2((c, DKC), 0)
    for h in range(HC):
        q = qa[:, h * DKC:(h + 1) * DKC] * (DKC ** -0.5)
        k = ka[:, h * DKC:(h + 1) * DKC]
        v = va[:, h * DVC:(h + 1) * DVC]
        bc = bcum[:, h * DKC:(h + 1) * DKC]
        rows = []
        for sb in range(c // GLA_SUB):
            i0 = sb * GLA_SUB
            ref = bc[i0:i0 + 1, :]
            qe = q[i0:i0 + GLA_SUB] * jnp.exp(bc[i0:i0 + GLA_SUB] - ref)
            ke = k * jnp.exp(jnp.where(jrow < i0 + GLA_SUB, ref - bc, 0.0))
            rows.append(_dot_nt(qe, ke, HIGHEST))
        attn = jnp.where(incl, jnp.concatenate(rows, axis=0), 0.0)
        s = s_scr[h]
        o = _dot(q * jnp.exp(bc), s, HIGHEST) + _dot(attn, v, HIGHEST)
        b_last = bc[c - 1:c, :]
        kd = k * jnp.exp(b_last - bc)
        b_last_col = bcum_t[h * DKC:(h + 1) * DKC, c - 1:c]
        s_scr[h] = jnp.exp(b_last_col) * s + _dot(kd.T, v, HIGHEST)
        on = o * lax.rsqrt(jnp.mean(o * o, axis=-1, keepdims=True) + EPS) * ng_ref[...]
        o_ref[:, h * DVC:(h + 1) * DVC] = (on * _silu(ra[:, h * DVC:(h + 1) * DVC])).astype(BF16)

    @pl.when(ci == nc - 1)
    def _():
        s_out_ref[0] = s_scr[...]


def gla_prompt(proj, bsz, seq, w2pad, b2, norm_g):
    nc = seq // CHUNK
    c = CHUNK
    row = lambda b, i: b * nc + i
    return pl.pallas_call(
        _gla_prompt_kernel,
        grid=(bsz, nc),
        in_specs=[pl.BlockSpec((c, HC * DKC), lambda b, i: (row(b, i), OFF_QC // (HC * DKC))),
                  pl.BlockSpec((c, HC * DKC), lambda b, i: (row(b, i), OFF_KC // (HC * DKC))),
                  pl.BlockSpec((c, HC * DVC), lambda b, i: (row(b, i), OFF_VC // (HC * DVC))),
                  pl.BlockSpec((c, HC * DVC), lambda b, i: (row(b, i), OFF_RC // (HC * DVC))),
                  pl.BlockSpec((c, 512), lambda b, i: (row(b, i), OFF_SMALL // 512)),
                  pl.BlockSpec((128, HC * DKC), lambda b, i: (0, 0)),
                  pl.BlockSpec((1, HC * DKC), lambda b, i: (0, 0)),
                  pl.BlockSpec((1, DVC), lambda b, i: (0, 0))],
        out_specs=[pl.BlockSpec((c, HC * DVC), lambda b, i: (row(b, i), 0)),
                   pl.BlockSpec((1, HC, DKC, DVC), lambda b, i: (b, 0, 0, 0))],
        out_shape=[jax.ShapeDtypeStruct((bsz * seq, HC * DVC), BF16),
                   jax.ShapeDtypeStruct((bsz, HC, DKC, DVC), F32)],
        scratch_shapes=[pltpu.VMEM((HC, DKC, DVC), F32)],
        compiler_params=_cparams(("parallel", "arbitrary")),
        name="gla_prompt",
    )(proj, proj, proj, proj, proj, w2pad, b2.reshape(1, HC * DKC), norm_g.reshape(1, DVC))


def _ssd_prompt_kernel(dvec_ref, xbc_ref, z_ref, sm_ref, cw_ref, cb_ref, arow_ref, dtb_ref, ng_ref,
                       o_ref, s_out_ref, s_scr, prev_scr):
    ci = pl.program_id(1)
    nc = pl.num_programs(1)

    @pl.when(ci == 0)
    def _():
        s_scr[...] = jnp.zeros_like(s_scr)
        prev_scr[...] = jnp.zeros_like(prev_scr)

    c = CHUNK
    xbc = _silu(_causal_conv_chunk(xbc_ref[...], prev_scr, cw_ref) + cb_ref[...])
    sm = sm_ref[:, 0:128]
    dt = _softplus(sm + dtb_ref[...])
    ri = _iota2((c, c), 0)
    cj = _iota2((c, c), 1)
    incl = cj <= ri
    tril = incl.astype(BF16)
    lcum = _tri_dot_left(tril, dt * arow_ref[...])
    lcum_t = lcum.T
    dt_t = dt.T
    z = z_ref[...]
    cb = []
    for g in range(NG):
        bg = xbc[:, HD * PD + g * NSTATE:HD * PD + (g + 1) * NSTATE]
        cg = xbc[:, HD * PD + NG * NSTATE + g * NSTATE:HD * PD + NG * NSTATE + (g + 1) * NSTATE]
        cb.append((bg, cg, _dot_nt(cg, bg, HIGHEST)))
    ys = []
    for h in range(HD):
        bg, cg, cbg = cb[h // (HD // NG)]
        xh = xbc[:, h * PD:(h + 1) * PD]
        lane = SM_DT + h
        li = lcum[:, lane:lane + 1]
        lj = lcum_t[lane:lane + 1, :]
        dti = dt[:, lane:lane + 1]
        dtj = dt_t[lane:lane + 1, :]
        decay = jnp.exp(jnp.where(incl, li - lj, -jnp.inf))
        scores = cbg * decay * dtj
        s = s_scr[h]
        y = _dot(scores, xh, HIGHEST) + _dot_nt(cg * jnp.exp(li), s, HIGHEST)
        l_last = lcum[c - 1:c, lane:lane + 1]
        xs = xh * (dti * jnp.exp(l_last - li))
        s_scr[h] = jnp.exp(l_last) * s + _dot(xs.T, bg, HIGHEST)
        ys.append((y + dvec_ref[h] * xh) * _silu(z[:, h * PD:(h + 1) * PD]))
    gw = HD * PD // NG
    for g in range(NG):
        yg = jnp.concatenate(ys[g * (HD // NG):(g + 1) * (HD // NG)], axis=-1)
        yn = yg * lax.rsqrt(jnp.mean(yg * yg, axis=-1, keepdims=True) + EPS)
        o_ref[:, g * gw:(g + 1) * gw] = (yn * ng_ref[:, g * gw:(g + 1) * gw]).astype(BF16)

    @pl.when(ci == nc - 1)
    def _():
        s_out_ref[0] = s_scr[...]


def ssd_prompt(proj, bsz, seq, conv_w, conv_b, arow, dtb, dvec, norm_g):
    nc = seq // CHUNK
    c = CHUNK
    row = lambda b, i: b * nc + i
    return pl.pallas_call(
        _ssd_prompt_kernel,
        grid=(bsz, nc),
        in_specs=[pl.BlockSpec(memory_space=pltpu.SMEM),
                  pl.BlockSpec((c, SSD_CONV_DIM), lambda b, i: (row(b, i), OFF_XBC // SSD_CONV_DIM)),
                  pl.BlockSpec((c, 512), lambda b, i: (row(b, i), OFF_ZD // 512)),
                  pl.BlockSpec((c, 512), lambda b, i: (row(b, i), OFF_SMALL // 512)),
                  pl.BlockSpec((CONV_W, SSD_CONV_DIM), lambda b, i: (0, 0)),
                  pl.BlockSpec((1, SSD_CONV_DIM), lambda b, i: (0, 0)),
                  pl.BlockSpec((1, 128), lambda b, i: (0, 0)),
                  pl.BlockSpec((1, 128), lambda b, i: (0, 0)),
                  pl.BlockSpec((1, HD * PD), lambda b, i: (0, 0))],
        out_specs=[pl.BlockSpec((c, HD * PD), lambda b, i: (row(b, i), 0)),
                   pl.BlockSpec((1, HD, PD, NSTATE), lambda b, i: (b, 0, 0, 0))],
        out_shape=[jax.ShapeDtypeStruct((bsz * seq, HD * PD), BF16),
                   jax.ShapeDtypeStruct((bsz, HD, PD, NSTATE), F32)],
        scratch_shapes=[pltpu.VMEM((HD, PD, NSTATE), F32), pltpu.VMEM((c, SSD_CONV_DIM), F32)],
        compiler_params=_cparams(("parallel", "arbitrary")),
        name="ssd_prompt",
    )(dvec, proj, proj, proj, conv_w, conv_b.reshape(1, SSD_CONV_DIM), arow, dtb,
      norm_g.reshape(1, HD * PD))


def _row8(r):
    return jnp.concatenate([r, jnp.zeros((7, r.shape[1]), F32)], axis=0)


def _sample_mixers_kernel(gneg_ref, gdtb_ref, aneg_ref, sdtb_ref, dvec_ref,
                          p_ref, gs_ref, gbuf_ref, ls_ref, ss_ref, sbuf_ref,
                          gcw_ref, gng_ref, w2_ref, b2_ref, lng_ref, scw_ref, scb_ref, sng_ref,
                          oa_ref, oc_ref, od_ref, gs_out, gbuf_out, ls_out, ss_out, sbuf_out):
    l = 0
    sm = p_ref[0, :, OFF_SMALL:OFF_SMALL + 128]

    u = p_ref[0, :, OFF_QKVA:OFF_QKVA + GDN_CONV_DIM]
    buf = gbuf_ref[0, 0]
    conv = (buf[0:1] * gcw_ref[0:1, :] + buf[1:2] * gcw_ref[1:2, :]
            + buf[2:3] * gcw_ref[2:3, :] + u * gcw_ref[3:4, :])
    gbuf_out[0, 0:2, :] = buf[1:3]
    gbuf_out[0, 2:3, :] = u
    x = _silu(conv)
    za = p_ref[0, :, OFF_ZA:OFF_ZA + HA * DVA]
    for h in range(HA):
        q = x[:, h * DKA:(h + 1) * DKA]
        k = x[:, HA * DKA + h * DKA:HA * DKA + (h + 1) * DKA]
        v = x[:, 2 * HA * DKA + h * DVA:2 * HA * DKA + (h + 1) * DVA]
        q = q * lax.rsqrt(jnp.sum(q * q, axis=-1, keepdims=True) + EPS) * (DKA ** -0.5)
        k = k * lax.rsqrt(jnp.sum(k * k, axis=-1, keepdims=True) + EPS)
        g = gneg_ref[h] * _softplus(sm[:, SM_A + h:SM_A + h + 1] + gdtb_ref[h])
        b = jax.nn.sigmoid(sm[:, SM_B + h:SM_B + h + 1])
        eg = jnp.exp(g)
        s = gs_ref[0, 0, h]
        lhs = jnp.concatenate([k, q * eg, jnp.zeros((6, DKA), F32)], axis=0)
        kq = _dot(lhs, s, HIGHEST)
        ks, qs = kq[0:1], kq[1:2]
        uu = b * v - (b * eg) * ks
        o = qs + jnp.sum(q * k, axis=-1, keepdims=True) * uu
        gs_out[0, h] = eg * s + _col_from_row(k) * uu
        on = o * lax.rsqrt(jnp.mean(o * o, axis=-1, keepdims=True) + EPS) * gng_ref[...]
        oa_ref[0, :, h * DVA:(h + 1) * DVA] = (on * _silu(za[:, h * DVA:(h + 1) * DVA])).astype(BF16)

    pre = _dot(jnp.broadcast_to(sm, (8, 128)), w2_ref[...], HIGHEST)[0:1] + b2_ref[...]
    loga = -_softplus(-pre) * (1.0 / GLA_TAU)
    qc = p_ref[0, :, OFF_QC:OFF_QC + HC * DKC]
    kc = p_ref[0, :, OFF_KC:OFF_KC + HC * DKC]
    vc = p_ref[0, :, OFF_VC:OFF_VC + HC * DVC]
    rc = p_ref[0, :, OFF_RC:OFF_RC + HC * DVC]
    for h in range(HC):
        q = qc[:, h * DKC:(h + 1) * DKC] * (DKC ** -0.5)
        k = kc[:, h * DKC:(h + 1) * DKC]
        v = vc[:, h * DVC:(h + 1) * DVC]
        ea = jnp.exp(loga[:, h * DKC:(h + 1) * DKC])
        s = ls_ref[0, 0, h]
        o = _dot(_row8(q * ea), s, HIGHEST)[0:1] + jnp.sum(q * k, axis=-1, keepdims=True) * v
        ls_out[0, h] = _col_from_row(ea) * s + _col_from_row(k) * v
        on = o * lax.rsqrt(jnp.mean(o * o, axis=-1, keepdims=True) + EPS) * lng_ref[...]
        oc_ref[0, :, h * DVC:(h + 1) * DVC] = (on * _silu(rc[:, h * DVC:(h + 1) * DVC])).astype(BF16)

    us = p_ref[0, :, OFF_XBC:OFF_XBC + SSD_CONV_DIM]
    sbuf = sbuf_ref[0, 0]
    sconv = (sbuf[0:1] * scw_ref[0:1, :] + sbuf[1:2] * scw_ref[1:2, :]
             + sbuf[2:3] * scw_ref[2:3, :] + us * scw_ref[3:4, :])
    sbuf_out[0, 0:2, :] = sbuf[1:3]
    sbuf_out[0, 2:3, :] = us
    xbc = _silu(sconv + scb_ref[...])
    zd = p_ref[0, :, OFF_ZD:OFF_ZD + HD * PD]
    ys = []
    for h in range(HD):
        g = h // (HD // NG)
        bg = xbc[:, HD * PD + g * NSTATE:HD * PD + (g + 1) * NSTATE]
        cg = xbc[:, HD * PD + NG * NSTATE + g * NSTATE:HD * PD + NG * NSTATE + (g + 1) * NSTATE]
        xh = xbc[:, h * PD:(h + 1) * PD]
        dt = _softplus(sm[:, SM_DT + h:SM_DT + h + 1] + sdtb_ref[h])
        lc = dt * aneg_ref[h]
        el = jnp.exp(lc)
        s = ss_ref[0, 0, h]
        score = jnp.sum(cg * bg, axis=-1, keepdims=True) * dt
        y = score * xh + _dot_nt(_row8(cg * el), s, HIGHEST)[0:1]
        ss_out[0, h] = el * s + _col_from_row(xh * dt) * bg
        ys.append((y + dvec_ref[h] * xh) * _silu(zd[:, h * PD:(h + 1) * PD]))
    gw = HD * PD // NG
    for g in range(NG):
        yg = jnp.concatenate(ys[g * (HD // NG):(g + 1) * (HD // NG)], axis=-1)
        yn = yg * lax.rsqrt(jnp.mean(yg * yg, axis=-1, keepdims=True) + EPS)
        od_ref[0, :, g * gw:(g + 1) * gw] = (yn * sng_ref[:, g * gw:(g + 1) * gw]).astype(BF16)
    del l


def sample_mixers(proj3, layer, state_gdn, state_gdn_conv, state_gla, state_ssd, state_ssd_conv, prm):
    nb = proj3.shape[0]
    smem = pl.BlockSpec(memory_space=pltpu.SMEM)
    full = lambda shape: pl.BlockSpec(shape, lambda b: (0,) * len(shape))
    l = layer
    outs = pl.pallas_call(
        _sample_mixers_kernel,
        grid=(nb,),
        in_specs=[smem, smem, smem, smem, smem,
                  pl.BlockSpec((1, 1, N_PROJ), lambda b: (b, 0, 0)),
                  pl.BlockSpec((1, 1, HA, DKA, DVA), lambda b: (l, b, 0, 0, 0)),
                  pl.BlockSpec((1, 1, CONV_W - 1, GDN_CONV_DIM), lambda b: (l, b, 0, 0)),
                  pl.BlockSpec((1, 1, HC, DKC, DVC), lambda b: (l, b, 0, 0, 0)),
                  pl.BlockSpec((1, 1, HD, PD, NSTATE), lambda b: (l, b, 0, 0, 0)),
                  pl.BlockSpec((1, 1, CONV_W - 1, SSD_CONV_DIM), lambda b: (l, b, 0, 0)),
                  full((CONV_W, GDN_CONV_DIM)), full((1, DVA)),
                  full((128, HC * DKC)), full((1, HC * DKC)), full((1, DVC)),
                  full((CONV_W, SSD_CONV_DIM)), full((1, SSD_CONV_DIM)), full((1, HD * PD))],
        out_specs=[pl.BlockSpec((1, 1, HA * DVA), lambda b: (b, 0, 0)),
                   pl.BlockSpec((1, 1, HC * DVC), lambda b: (b, 0, 0)),
                   pl.BlockSpec((1, 1, HD * PD), lambda b: (b, 0, 0)),
                   pl.BlockSpec((1, HA, DKA, DVA), lambda b: (b, 0, 0, 0)),
                   pl.BlockSpec((1, CONV_W - 1, GDN_CONV_DIM), lambda b: (b, 0, 0)),
                   pl.BlockSpec((1, HC, DKC, DVC), lambda b: (b, 0, 0, 0)),
                   pl.BlockSpec((1, HD, PD, NSTATE), lambda b: (b, 0, 0, 0)),
                   pl.BlockSpec((1, CONV_W - 1, SSD_CONV_DIM), lambda b: (b, 0, 0))],
        out_shape=[jax.ShapeDtypeStruct((nb, 1, HA * DVA), BF16),
                   jax.ShapeDtypeStruct((nb, 1, HC * DVC), BF16),
                   jax.ShapeDtypeStruct((nb, 1, HD * PD), BF16),
                   jax.ShapeDtypeStruct((nb, HA, DKA, DVA), F32),
                   jax.ShapeDtypeStruct((nb, CONV_W - 1, GDN_CONV_DIM), F32),
                   jax.ShapeDtypeStruct((nb, HC, DKC, DVC), F32),
                   jax.ShapeDtypeStruct((nb, HD, PD, NSTATE), F32),
                   jax.ShapeDtypeStruct((nb, CONV_W - 1, SSD_CONV_DIM), F32)],
        compiler_params=_cparams(("parallel",)),
        name="sample_mixers",
    )(prm["gdn_negA"], prm["gdn_dt_bias"], prm["ssd_negA"], prm["ssd_dt_bias"], prm["ssd_D"],
      proj3, state_gdn, state_gdn_conv, state_gla, state_ssd, state_ssd_conv,
      prm["gdn_conv_w"], prm["gdn_norm_g"].reshape(1, DVA),
      prm["w2pad"], prm["gla_b2"].reshape(1, HC * DKC), prm["gla_norm_g"].reshape(1, DVC),
      prm["ssd_conv_w"], prm["ssd_conv_b"].reshape(1, SSD_CONV_DIM),
      prm["ssd_norm_g"].reshape(1, HD * PD))
    return outs


def _sample_attn_kernel(pt_ref, bias_ref, q_ref, k_ref, v_ref, o_ref, acc_scr, run_scr):
    j = pl.program_id(1)
    npg = pl.num_programs(1)

    @pl.when(j == 0)
    def _():
        acc_scr[...] = jnp.zeros_like(acc_scr)
        run_scr[...] = jnp.zeros_like(run_scr)

    w = HB * DB
    q = q_ref[0]
    rows = _iota2((8, w), 0)
    lanes = _iota2((8, w), 1)
    qrows = jnp.where(lanes // DB == rows, jnp.broadcast_to(q, (8, w)), 0.0).astype(BF16)
    r8 = _iota2((8, 1), 0)
    bias = jnp.zeros((8, 1), F32)
    for h in range(HB):
        bias = jnp.where(r8 == h, bias_ref[h], bias)
    kp = k_ref[0, 0].astype(BF16)
    vp = v_ref[0, 0].astype(BF16)
    z = _dot_nt(qrows, kp) * (DB ** -0.5) + bias
    sp = _softplus(z)
    lf = -sp
    pi = _iota2((PAGE_SIZE, PAGE_SIZE), 0)
    pj = _iota2((PAGE_SIZE, PAGE_SIZE), 1)
    tri = (pi > pj).astype(BF16)
    hi = lf.astype(BF16)
    lo = (lf - hi.astype(F32)).astype(BF16)
    after = (_dot(hi, tri) + _dot(lo, tri)) + run_scr[...]
    a = jnp.exp((z - sp) + after)
    acc_scr[...] += _dot(a.astype(BF16), vp)
    run_scr[...] += jnp.sum(lf, axis=-1, keepdims=True)

    @pl.when(j == npg - 1)
    def _():
        acc = acc_scr[...]
        for h in range(HB):
            o_ref[0, :, h * DB:(h + 1) * DB] = acc[h:h + 1, h * DB:(h + 1) * DB].astype(BF16)


def sample_attn(proj3, layer, cache_k4, cache_v4, page_table, bias):
    nb, npg = page_table.shape
    w = HB * DB
    l = layer
    grid_spec = pltpu.PrefetchScalarGridSpec(
        num_scalar_prefetch=1,
        grid=(nb, npg),
        in_specs=[pl.BlockSpec(memory_space=pltpu.SMEM),
                  pl.BlockSpec((1, 1, w), lambda b, j, pt: (b, 0, OFF_QB // w)),
                  pl.BlockSpec((1, 1, PAGE_SIZE, w), lambda b, j, pt: (l, pt[b, npg - 1 - j], 0, 0)),
                  pl.BlockSpec((1, 1, PAGE_SIZE, w), lambda b, j, pt: (l, pt[b, npg - 1 - j], 0, 0))],
        out_specs=pl.BlockSpec((1, 1, w), lambda b, j, pt: (b, 0, 0)),
        scratch_shapes=[pltpu.VMEM((8, w), F32), pltpu.VMEM((8, 1), F32)],
    )
    return pl.pallas_call(
        _sample_attn_kernel,
        grid_spec=grid_spec,
        out_shape=jax.ShapeDtypeStruct((nb, 1, w), BF16),
        compiler_params=_cparams(("parallel", "arbitrary")),
        name="sample_attn",
    )(page_table, bias, proj3, cache_k4, cache_v4)


def _out_proj_kernel(a_ref, b_ref, c_ref, d_ref, w_ref, x_ref, g1_ref, n2_ref, sc_ref, sh_ref,
                     xo_ref, h_ref):
    gw = GROUP_WIDTH
    acc = _dot(a_ref[...], w_ref[0:gw, :])
    acc = acc + _dot(b_ref[...], w_ref[gw:2 * gw, :])
    acc = acc + _dot(c_ref[...], w_ref[2 * gw:3 * gw, :])
    acc = acc + _dot(d_ref[...], w_ref[3 * gw:4 * gw, :])
    x = x_ref[...] + g1_ref[0] * acc
    xo_ref[...] = x
    y = x * lax.rsqrt(jnp.mean(x * x, axis=-1, keepdims=True) + EPS) * n2_ref[...]
    h_ref[...] = (y * (1.0 + sc_ref[0]) + sh_ref[0]).astype(BF16)


def out_proj(oa, ob, oc, od, w_out, x, g1, n2, sc2, sh2, rows_per_batch):
    t, d = x.shape
    tm = _row_tile(t, 512, g1, rows_per_batch)
    part = pl.BlockSpec((tm, GROUP_WIDTH), lambda i: (i, 0))
    return pl.pallas_call(
        _out_proj_kernel,
        grid=(t // tm,),
        in_specs=[part, part, part, part,
                  pl.BlockSpec((d, d), lambda i: (0, 0)),
                  pl.BlockSpec((tm, d), lambda i: (i, 0)),
                  _mod_spec(g1, tm, rows_per_batch),
                  pl.BlockSpec((1, d), lambda i: (0, 0)),
                  _mod_spec(sc2, tm, rows_per_batch),
                  _mod_spec(sh2, tm, rows_per_batch)],
        out_specs=[pl.BlockSpec((tm, d), lambda i: (i, 0)),
                   pl.BlockSpec((tm, d), lambda i: (i, 0))],
        out_shape=[jax.ShapeDtypeStruct((t, d), F32), jax.ShapeDtypeStruct((t, d), BF16)],
        compiler_params=_cparams(("parallel",), VMEM_LIMIT),
        name="out_proj",
    )(oa, ob, oc, od, w_out, x, g1, n2.reshape(1, d), sc2, sh2)


ROUTE_SUB = 128


def _topk_rows(vals, k):
    r = vals.shape[0]
    riota = _iota2(vals.shape, 0).astype(F32)
    out_v, out_i = [], []
    for _ in range(k):
        m = jnp.max(vals, axis=0, keepdims=True)
        idx = jnp.min(jnp.where(vals == m, riota, float(r)), axis=0, keepdims=True)
        out_v.append(m)
        out_i.append(idx)
        vals = jnp.where(riota == idx, -jnp.inf, vals)
    return jnp.concatenate(out_v, axis=0), jnp.concatenate(out_i, axis=0)


def _select_rows(table, sel):
    out = jnp.zeros(sel.shape, F32)
    for a in range(table.shape[0]):
        out = jnp.where(sel == float(a), table[a:a + 1, :], out)
    return out


def _peer_route_kernel(h_ref, wq_ref, sk_ref, e1_ref, e2_ref, gt_ref, q_scr, e1_scr, e2_scr, gt_scr):
    tm = h_ref.shape[0]
    q = _dot(h_ref[...], wq_ref[...])
    for cgrp in range(2 * PEER_HEADS):
        q_scr[cgrp] = q[:, cgrp * PK_DIM:(cgrp + 1) * PK_DIM]
    sk0 = sk_ref[0]
    sk1 = sk_ref[1]
    kk = PEER_TOPK

    for sub in range(tm // ROUTE_SUB) if tm >= ROUTE_SUB else range(1):
        n = min(tm, ROUTE_SUB)
        r0 = sub * ROUTE_SUB

        def head_body(hd, _):
            qa = q_scr[2 * hd, r0:r0 + n, :]
            qb = q_scr[2 * hd + 1, r0:r0 + n, :]
            s0 = _dot_nt(sk0, qa)
            s1 = _dot_nt(sk1, qb)
            sv0, si0 = _topk_rows(s0, kk)
            sv1, si1 = _topk_rows(s1, kk)
            cand = (sv0[:, None, :] + sv1[None, :, :]).reshape(kk * kk, n)
            cv, cidx = _topk_rows(cand, kk)
            ia = jnp.floor(cidx * (1.0 / kk))
            ib = cidx - ia * kk
            e1 = _select_rows(si0, ia)
            e2 = _select_rows(si1, ib)
            ex = jnp.exp(cv - jnp.max(cv, axis=0, keepdims=True))
            gates = ex / jnp.sum(ex, axis=0, keepdims=True)
            ro = pl.multiple_of(hd * kk, kk)
            e1_scr[pl.ds(ro, kk), r0:r0 + n] = e1
            e2_scr[pl.ds(ro, kk), r0:r0 + n] = e2
            gt_scr[pl.ds(ro, kk), r0:r0 + n] = gates
            return 0

        lax.fori_loop(0, PEER_HEADS, head_body, 0)

    e1_ref[...] = e1_scr[...].T
    e2_ref[...] = e2_scr[...].T
    gt_ref[...] = gt_scr[...].T


def peer_route(h2, wq, sub_keys):
    t, d = h2.shape
    tm = min(t, 256)
    nq = wq.shape[1]
    nj = PEER_HEADS * PEER_TOPK
    out = jax.ShapeDtypeStruct((t, nj), F32)
    ospec = pl.BlockSpec((tm, nj), lambda i: (i, 0))
    return pl.pallas_call(
        _peer_route_kernel,
        grid=(t // tm,),
        in_specs=[pl.BlockSpec((tm, d), lambda i: (i, 0)),
                  pl.BlockSpec((d, nq), lambda i: (0, 0)),
                  pl.BlockSpec((2, N_KEYS, PK_DIM), lambda i: (0, 0, 0))],
        out_specs=[ospec, ospec, ospec],
        out_shape=[out, out, out],
        scratch_shapes=[pltpu.VMEM((2 * PEER_HEADS, tm, PK_DIM), F32),
                        pltpu.VMEM((nj, tm), F32), pltpu.VMEM((nj, tm), F32),
                        pltpu.VMEM((nj, tm), F32)],
        compiler_params=_cparams(("parallel",), VMEM_LIMIT),
        name="peer_route",
    )(h2, wq, sub_keys)


def _peer_gate_kernel(e1_ref, e2_ref, gt_ref, o_ref):
    tg = e1_ref.shape[0]
    riota = _iota2((N_KEYS, PEER_HEADS * PEER_TOPK), 0).astype(F32)

    def body(t, _):
        e1 = e1_ref[pl.ds(t, 1), :]
        e2 = e2_ref[pl.ds(t, 1), :]
        g = gt_ref[pl.ds(t, 1), :]
        pt = jnp.where(riota == e1, g, 0.0).astype(BF16)
        qt = jnp.where(riota == e2, 1.0, 0.0).astype(BF16)
        o_ref[t] = _dot_nt(pt, qt).astype(BF16)
        return 0

    lax.fori_loop(0, tg, body, 0)


def peer_gate(e1, e2, gt):
    t, nj = e1.shape
    tg = min(t, 64)
    ispec = pl.BlockSpec((tg, nj), lambda i: (i, 0))
    g = pl.pallas_call(
        _peer_gate_kernel,
        grid=(t // tg,),
        in_specs=[ispec, ispec, ispec],
        out_specs=pl.BlockSpec((tg, N_KEYS, N_KEYS), lambda i: (i, 0, 0)),
        out_shape=jax.ShapeDtypeStruct((t, N_KEYS, N_KEYS), BF16),
        compiler_params=_cparams(("parallel",)),
        name="peer_gate",
    )(e1, e2, gt)
    return g.reshape(t, N_EXPERTS)


def _peer_dense_kernel(h_ref, u_ref, v_ref, g_ref, x_ref, g2_ref, o_ref, acc_scr):
    e = pl.program_id(1)
    ne = pl.num_programs(1)

    @pl.when(e == 0)
    def _():
        acc_scr[...] = jnp.zeros_like(acc_scr)

    a = _dot_nt(h_ref[...], u_ref[...])
    act = a * (lax.erf(a * (0.5 ** 0.5)) + 1.0) * 0.5
    hh = (g_ref[...].astype(F32) * act).astype(BF16)
    acc_scr[...] += _dot(hh, v_ref[...])

    @pl.when(e == ne - 1)
    def _():
        o_ref[...] = x_ref[...] + g2_ref[0] * acc_scr[...]


def peer_dense(h2, u, v, gmat, x, g2, rows_per_batch):
    t, d = x.shape
    tm = _row_tile(t, 512, g2, rows_per_batch)
    te = 512
    return pl.pallas_call(
        _peer_dense_kernel,
        grid=(t // tm, N_EXPERTS // te),
        in_specs=[pl.BlockSpec((tm, d), lambda i, e: (i, 0)),
                  pl.BlockSpec((te, d), lambda i, e: (e, 0)),
                  pl.BlockSpec((te, d), lambda i, e: (e, 0)),
                  pl.BlockSpec((tm, te), lambda i, e: (i, e)),
                  pl.BlockSpec((tm, d), lambda i, e: (i, 0)),
                  _mod_spec(g2, tm, rows_per_batch)],
        out_specs=pl.BlockSpec((tm, d), lambda i, e: (i, 0)),
        out_shape=jax.ShapeDtypeStruct((t, d), F32),
        scratch_shapes=[pltpu.VMEM((tm, d), F32)],
        compiler_params=_cparams(("parallel", "arbitrary"), VMEM_LIMIT),
        name="peer_dense",
    )(h2, u, v, gmat, x, g2)


def _final_norm_kernel(x_ref, g_ref, o_ref):
    x = x_ref[...]
    o_ref[...] = x * lax.rsqrt(jnp.mean(x * x, axis=-1, keepdims=True) + EPS) * g_ref[...]


def final_norm(x, g):
    t, d = x.shape
    tm = min(t, 512)
    return pl.pallas_call(
        _final_norm_kernel,
        grid=(t // tm,),
        in_specs=[pl.BlockSpec((tm, d), lambda i: (i, 0)), pl.BlockSpec((1, d), lambda i: (0, 0))],
        out_specs=pl.BlockSpec((tm, d), lambda i: (i, 0)),
        out_shape=jax.ShapeDtypeStruct((t, d), F32),
        compiler_params=_cparams(("parallel",)),
        name="final_norm",
    )(x, g.reshape(1, d))


def _permute_w_in(w):
    d = w.shape[0]
    small = jnp.concatenate([w[:, 2048:2056], w[:, 5128:5144], w[:, 6680:6688]], axis=1)
    pad = jnp.zeros((d, N_PROJ - OFF_SMALL - small.shape[1]), w.dtype)
    return jnp.concatenate([w[:, 0:2048], w[:, 2056:5128], w[:, 5144:6680], small, pad],
                           axis=1).astype(BF16)


def _small_row(vals, off):
    return jnp.zeros((1, 128), F32).at[0, off:off + vals.shape[0]].set(vals.astype(F32))


def _layer_params(l, ada_w, ada_b, norm1_g, norm2_g, w_in, w_out, gdn_conv_w, gdn_A_log, gdn_dt_bias,
                  gdn_norm_g, sb_bias, gla_w2, gla_b2, gla_norm_g, ssd_conv_w, ssd_conv_b, ssd_A_log,
                  ssd_dt_bias, ssd_D, ssd_norm_g, peer_w_query, peer_sub_keys, peer_u, peer_v):
    del ada_w, ada_b
    w2pad = jnp.zeros((128, HC * DKC), F32).at[SM_G:SM_G + GLA_RANK, :].set(gla_w2[l])
    return {
        "norm1_g": norm1_g[l], "norm2_g": norm2_g[l],
        "w_in": _permute_w_in(w_in[l]), "w_out": w_out[l].astype(BF16),
        "gdn_conv_w": gdn_conv_w[l], "gdn_negA": -jnp.exp(gdn_A_log[l]), "gdn_dt_bias": gdn_dt_bias[l],
        "gdn_gcoef": _small_row(-jnp.exp(gdn_A_log[l]), SM_A),
        "gdn_dtb": _small_row(gdn_dt_bias[l], SM_A),
        "gdn_norm_g": gdn_norm_g[l], "sb_bias": sb_bias[l],
        "w2pad": w2pad, "gla_b2": gla_b2[l], "gla_norm_g": gla_norm_g[l],
        "ssd_conv_w": ssd_conv_w[l], "ssd_conv_b": ssd_conv_b[l],
        "ssd_negA": -jnp.exp(ssd_A_log[l]), "ssd_dt_bias": ssd_dt_bias[l],
        "ssd_arow": _small_row(-jnp.exp(ssd_A_log[l]), SM_DT),
        "ssd_dtb": _small_row(ssd_dt_bias[l], SM_DT),
        "ssd_D": ssd_D[l], "ssd_norm_g": ssd_norm_g[l],
        "wq": peer_w_query[l].astype(BF16), "sub_keys": peer_sub_keys[l],
        "peer_u": peer_u[l].astype(BF16), "peer_v": peer_v[l].astype(BF16),
    }


def _split_mod(mod):
    return [mod[:, i * D_MODEL:(i + 1) * D_MODEL] for i in range(6)]


def _peer_block(h2, x, g2, prm, rows_per_batch):
    e1, e2, gt = peer_route(h2, prm["wq"], prm["sub_keys"])
    gmat = peer_gate(e1, e2, gt)
    return peer_dense(h2, prm["peer_u"], prm["peer_v"], gmat, x, g2, rows_per_batch)


def _peer_block_padded(h2, x, g2, prm):
    t = x.shape[0]
    tpad = -(-t // ROUTE_SUB) * ROUTE_SUB
    pad = lambda a: jnp.concatenate([a, jnp.zeros((tpad - t,) + a.shape[1:], a.dtype)], axis=0)
    g2p = pad(g2.reshape(t, -1)).reshape(1, tpad, -1)
    return _peer_block(pad(h2), pad(x), g2p, prm, 1)[:t]


def kernel(x_prompt, x_sample, cache_k, cache_v, state_gdn, state_gdn_conv, state_gla, state_ssd, state_ssd_conv, page_table, c_prompt, c_sample, ada_w, ada_b, norm1_g, norm2_g, w_in, w_out, gdn_conv_w, gdn_A_log, gdn_dt_bias, gdn_norm_g, sb_bias, gla_w2, gla_b2, gla_norm_g, ssd_conv_w, ssd_conv_b, ssd_A_log, ssd_dt_bias, ssd_D, ssd_norm_g, peer_w_query, peer_sub_keys, peer_u, peer_v, final_norm_g):
    bsz, seq, d = x_prompt.shape
    nb = x_sample.shape[0]
    tp = bsz * seq
    n_pool = cache_k.shape[1]
    cache_k4 = cache_k.reshape(DEPTH, n_pool, PAGE_SIZE, HB * DB)
    cache_v4 = cache_v.reshape(DEPTH, n_pool, PAGE_SIZE, HB * DB)

    n_c = bsz + nb
    r_pad = -(-n_c // 8) * 8
    c_all = jnp.concatenate([c_prompt, c_sample, jnp.zeros((r_pad - n_c, d), F32)], axis=0)
    mod = ada_mod(c_all, ada_w, ada_b)

    xp = x_prompt.reshape(tp, d)
    xs = x_sample.reshape(nb, d)
    outs_p, outs_s = [], []
    for l in range(DEPTH):
        prm = _layer_params(l, ada_w, ada_b, norm1_g, norm2_g, w_in, w_out, gdn_conv_w, gdn_A_log,
                            gdn_dt_bias, gdn_norm_g, sb_bias, gla_w2, gla_b2, gla_norm_g, ssd_conv_w,
                            ssd_conv_b, ssd_A_log, ssd_dt_bias, ssd_D, ssd_norm_g, peer_w_query,
                            peer_sub_keys, peer_u, peer_v)
        mp = [m.reshape(bsz, 1, d) for m in _split_mod(mod[l, 0:bsz])]
        ms = [m.reshape(1, nb, d) for m in _split_mod(mod[l, bsz:bsz + nb])]

        proj = in_proj(xp, prm["norm1_g"], mp[1], mp[0], prm["w_in"], seq)
        oa, gdn_s = gdn_prompt(proj, bsz, seq, prm["gdn_conv_w"], prm["gdn_gcoef"], prm["gdn_dtb"],
                               prm["gdn_norm_g"])
        ob = sb_prompt(proj, bsz, seq, prm["sb_bias"])
        oc, gla_s = gla_prompt(proj, bsz, seq, prm["w2pad"], prm["gla_b2"], prm["gla_norm_g"])
        od, ssd_s = ssd_prompt(proj, bsz, seq, prm["ssd_conv_w"], prm["ssd_conv_b"], prm["ssd_arow"],
                               prm["ssd_dtb"], prm["ssd_D"], prm["ssd_norm_g"])
        p3 = proj.reshape(bsz, seq, N_PROJ)
        outs_p.append((p3[:, :, OFF_KB:OFF_KB + HB * DB].reshape(bsz, seq, HB, DB),
                       p3[:, :, OFF_VB:OFF_VB + HB * DB].reshape(bsz, seq, HB, DB),
                       gdn_s, p3[:, seq - (CONV_W - 1):, OFF_QKVA:OFF_QKVA + GDN_CONV_DIM],
                       gla_s, ssd_s, p3[:, seq - (CONV_W - 1):, OFF_XBC:OFF_XBC + SSD_CONV_DIM]))
        xp, h2 = out_proj(oa, ob, oc, od, prm["w_out"], xp, mp[2], prm["norm2_g"], mp[4], mp[3], seq)
        xp = _peer_block(h2, xp, mp[5], prm, seq)

        proj_s = in_proj(xs, prm["norm1_g"], ms[1], ms[0], prm["w_in"], 1)
        ps3 = proj_s.reshape(nb, 1, N_PROJ)
        (oa_s, oc_s, od_s, gdn_n, gbuf_n, gla_n, ssd_n, sbuf_n) = sample_mixers(
            ps3, l, state_gdn, state_gdn_conv, state_gla, state_ssd, state_ssd_conv, prm)
        ob_s = sample_attn(ps3, l, cache_k4, cache_v4, page_table, prm["sb_bias"])
        outs_s.append((proj_s[:, OFF_KB:OFF_KB + HB * DB].reshape(nb, 1, HB, DB),
                       proj_s[:, OFF_VB:OFF_VB + HB * DB].reshape(nb, 1, HB, DB),
                       gdn_n, gbuf_n, gla_n, ssd_n, sbuf_n))
        xs, h2s = out_proj(oa_s.reshape(nb, -1), ob_s.reshape(nb, -1), oc_s.reshape(nb, -1),
                           od_s.reshape(nb, -1), prm["w_out"], xs, ms[2], prm["norm2_g"], ms[4], ms[3], 1)
        xs = _peer_block_padded(h2s, xs, ms[5], prm)

    y_prompt = final_norm(xp, final_norm_g).reshape(bsz, seq, d)
    y_sample = final_norm(xs, final_norm_g).reshape(nb, 1, d)
    stk = lambda lst, i: jnp.stack([s[i] for s in lst], axis=0)
    return (y_prompt, y_sample, stk(outs_p, 0), stk(outs_p, 1), stk(outs_s, 0), stk(outs_s, 1),
            stk(outs_p, 2), stk(outs_s, 2), stk(outs_p, 3), stk(outs_s, 3), stk(outs_p, 4), stk(outs_s, 4),
            stk(outs_p, 5), stk(outs_s, 5), stk(outs_p, 6), stk(outs_s, 6))
```

```python
import functools
import math

import jax
import jax.numpy as jnp
import numpy as np
from jax import lax
from jax.experimental import pallas as pl
from jax.experimental.pallas import tpu as pltpu

F32 = jnp.float32
BF16 = jnp.bfloat16
HIGHEST = lax.Precision.HIGHEST

D_MODEL = 2048
DEPTH = 2
PAGE_SIZE = 128
GROUP_WIDTH = D_MODEL // 4
HA, DKA, DVA = 4, 128, 128
HB, DB = 4, 128
HC, DKC, DVC = 4, 64, 128
GLA_RANK = 16
GLA_TAU = 16.0
HD, PD, NG, NSTATE = 8, 64, 2, 128
CONV_W = 4
CHUNK = 64
SB_BLOCK = 128
PEER_HEADS = 8
N_KEYS = 128
N_EXPERTS = N_KEYS * N_KEYS
PK_DIM = 128
PEER_TOPK = 16
EPS = 1e-6
GDN_CONV_DIM = 2 * HA * DKA + HA * DVA
SSD_CONV_DIM = HD * PD + 2 * NG * NSTATE

OFF_QKVA = 0
OFF_ZA = 1536
OFF_QB = 2048
OFF_KB = 2560
OFF_VB = 3072
OFF_QC = 3584
OFF_KC = 3840
OFF_VC = 4096
OFF_RC = 4608
OFF_XBC = 5120
OFF_ZD = 6144
OFF_SMALL = 6656
SM_A, SM_B, SM_G, SM_DT = 0, 4, 8, 24
N_PROJ = 7168

VMEM_LIMIT = 56 * 1024 * 1024


def _cparams(sem, vmem=None):
    return pltpu.CompilerParams(dimension_semantics=sem, vmem_limit_bytes=vmem)


def _softplus(x):
    return jnp.maximum(x, 0.0) + jnp.log1p(jnp.exp(-jnp.abs(x)))


def _softplus_log(x):
    return jnp.maximum(x, 0.0) + jnp.log(1.0 + jnp.exp(-jnp.abs(x)))


def _silu(x):
    return x * jax.nn.sigmoid(x)


def _dotb(a, b):
    return _dot(a.astype(BF16), b.astype(BF16))


def _dotb_nt(a, b):
    return _dot_nt(a.astype(BF16), b.astype(BF16))


def _split2(x):
    hi = x.astype(BF16)
    return hi, (x - hi.astype(F32)).astype(BF16)


def _dot3(a, b):
    ah, al = _split2(a)
    bh, bl = _split2(b)
    return _dot(ah, bh) + (_dot(ah, bl) + _dot(al, bh))


def _dot(a, b, precision=None):
    return jnp.dot(a, b, precision=precision, preferred_element_type=F32)


def _dot_nt(a, b, precision=None):
    return lax.dot_general(a, b, (((1,), (1,)), ((), ())), precision=precision,
                           preferred_element_type=F32)


def _split3(x):
    hi = x.astype(BF16)
    r1 = x - hi.astype(F32)
    mid = r1.astype(BF16)
    lo = (r1 - mid.astype(F32)).astype(BF16)
    return hi, mid, lo


def _tri_dot_left(tri_bf16, x):
    hi, mid, lo = _split3(x)
    return (_dot(tri_bf16, hi) + _dot(tri_bf16, mid)) + _dot(tri_bf16, lo)


def _tri_dot_right(x, tri_bf16):
    hi, mid, lo = _split3(x)
    return (_dot(hi, tri_bf16) + _dot(mid, tri_bf16)) + _dot(lo, tri_bf16)


def _iota2(shape, dim):
    return lax.broadcasted_iota(jnp.int32, shape, dim)


def _col_from_row(r):
    n = r.shape[1]
    eye = _iota2((n, n), 0) == _iota2((n, n), 1)
    return jnp.sum(jnp.where(eye, jnp.broadcast_to(r, (n, n)), 0.0), axis=1, keepdims=True)


def _ada_kernel(c_ref, w_ref, b_ref, o_ref):
    c = c_ref[...]
    o_ref[0] = _dot(_silu(c), w_ref[0]) + b_ref[0]


def ada_mod(c_all, ada_w, ada_b):
    r = c_all.shape[0]
    n = ada_w.shape[2]
    tn = 1024
    return pl.pallas_call(
        _ada_kernel,
        grid=(DEPTH, n // tn),
        in_specs=[pl.BlockSpec((r, D_MODEL), lambda l, j: (0, 0)),
                  pl.BlockSpec((1, D_MODEL, tn), lambda l, j: (l, 0, j)),
                  pl.BlockSpec((1, 1, tn), lambda l, j: (l, 0, j))],
        out_specs=pl.BlockSpec((1, r, tn), lambda l, j: (l, 0, j)),
        out_shape=jax.ShapeDtypeStruct((DEPTH, r, n), F32),
        compiler_params=_cparams(("parallel", "parallel"), VMEM_LIMIT),
        name="ada_mod",
    )(c_all, ada_w, ada_b.reshape(DEPTH, 1, n))


def _in_proj_kernel(x_ref, g_ref, sc_ref, sh_ref, w_ref, o_ref, h_scr):
    @pl.when(pl.program_id(1) == 0)
    def _():
        x = x_ref[...]
        y = x * lax.rsqrt(jnp.mean(x * x, axis=-1, keepdims=True) + EPS) * g_ref[...]
        h_scr[...] = (y * (1.0 + sc_ref[0]) + sh_ref[0]).astype(BF16)

    o_ref[...] = _dot(h_scr[...], w_ref[...])


def _row_tile(t, cap, mod, rows_per_batch):
    return min(t, cap, rows_per_batch) if mod.shape[1] == 1 else min(t, cap)


def _mod_spec(mod, tm, rows_per_batch):
    nb, r, d = mod.shape
    if r == 1:
        return pl.BlockSpec((1, 1, d), lambda i, *_: ((i * tm) // rows_per_batch, 0, 0))
    return pl.BlockSpec((1, r, d), lambda i, *_: (0, 0, 0))


def in_proj(x, g, sc, sh, w, rows_per_batch):
    t, d = x.shape
    n = w.shape[1]
    tm = _row_tile(t, 1024, sc, rows_per_batch)
    tn = 512
    return pl.pallas_call(
        _in_proj_kernel,
        grid=(t // tm, n // tn),
        in_specs=[pl.BlockSpec((tm, d), lambda i, j: (i, 0)),
                  pl.BlockSpec((1, d), lambda i, j: (0, 0)),
                  _mod_spec(sc, tm, rows_per_batch),
                  _mod_spec(sh, tm, rows_per_batch),
                  pl.BlockSpec((d, tn), lambda i, j: (0, j))],
        out_specs=pl.BlockSpec((tm, tn), lambda i, j: (i, j)),
        out_shape=jax.ShapeDtypeStruct((t, n), F32),
        scratch_shapes=[pltpu.VMEM((tm, d), BF16)],
        compiler_params=_cparams(("parallel", "arbitrary"), VMEM_LIMIT),
        name="in_proj",
    )(x, g.reshape(1, d), sc, sh, w)


def _causal_conv_chunk(u, prev_ref, w_ref):
    c = u.shape[0]
    rows = _iota2(u.shape, 0)
    prev = prev_ref[...]
    out = u * w_ref[CONV_W - 1:CONV_W, :]
    for k in range(1, CONV_W):
        shifted = jnp.where(rows >= k, pltpu.roll(u, k, 0), pltpu.roll(prev, k, 0))
        out = out + shifted * w_ref[CONV_W - 1 - k:CONV_W - k, :]
    prev_ref[...] = u
    del c
    return out


def _inv_unit_lower(m):
    c = m.shape[0]
    eye = (_iota2((c, c), 0) == _iota2((c, c), 1)).astype(F32)
    x = -m
    p = eye + x
    steps = int(math.ceil(math.log2(c))) - 1
    for _ in range(steps):
        x = _dot3(x, x)
        p = p + _dot3(p, x)
    return p


def _gdn_prompt_kernel(qkv_ref, z_ref, sm_ref, cw_ref, gcoef_ref, dtb_ref, ng_ref,
                       o_ref, s_out_ref, s_scr, prev_scr):
    ci = pl.program_id(1)
    nc = pl.num_programs(1)

    @pl.when(ci == 0)
    def _():
        s_scr[...] = jnp.zeros_like(s_scr)
        prev_scr[...] = jnp.zeros_like(prev_scr)

    c = CHUNK
    x = _silu(_causal_conv_chunk(qkv_ref[...], prev_scr, cw_ref))
    sm = sm_ref[:, 0:128]
    g = gcoef_ref[...] * _softplus(sm + dtb_ref[...])
    beta = jax.nn.sigmoid(sm)
    ri = _iota2((c, c), 0)
    cj = _iota2((c, c), 1)
    incl = cj <= ri
    strict = cj < ri
    tril = incl.astype(BF16)
    gcum = _tri_dot_left(tril, g)
    gcum_t = gcum.T
    z = z_ref[...]
    for h in range(HA):
        q = x[:, h * DKA:(h + 1) * DKA]
        k = x[:, HA * DKA + h * DKA:HA * DKA + (h + 1) * DKA]
        v = x[:, 2 * HA * DKA + h * DVA:2 * HA * DKA + (h + 1) * DVA]
        q = q * lax.rsqrt(jnp.sum(q * q, axis=-1, keepdims=True) + EPS) * (DKA ** -0.5)
        k = k * lax.rsqrt(jnp.sum(k * k, axis=-1, keepdims=True) + EPS)
        gc = gcum[:, SM_A + h:SM_A + h + 1]
        gr = gcum_t[SM_A + h:SM_A + h + 1, :]
        bc = beta[:, SM_B + h:SM_B + h + 1]
        decay = jnp.exp(jnp.where(incl, gc - gr, -jnp.inf))
        eg = jnp.exp(gc)
        kb = k.astype(BF16)
        kk = _dot_nt(kb, kb)
        m = jnp.where(strict, decay, 0.0) * kk * bc
        tinv = _inv_unit_lower(m)
        w = _dotb(tinv, (bc * eg) * k)
        u = _dotb(tinv, bc * v)
        s = s_scr[h]
        sb = s.astype(BF16)
        u = u - _dot(w.astype(BF16), sb)
        attn = _dot_nt(q.astype(BF16), kb) * decay
        o = _dot((q * eg).astype(BF16), sb) + _dotb(attn, u)
        g_last = gcum[c - 1:c, SM_A + h:SM_A + h + 1]
        kd = k * jnp.exp(g_last - gc)
        s_scr[h] = jnp.exp(g_last) * s + _dotb(kd.T, u)
        on = o * lax.rsqrt(jnp.mean(o * o, axis=-1, keepdims=True) + EPS) * ng_ref[...]
        o_ref[:, h * DVA:(h + 1) * DVA] = (on * _silu(z[:, h * DVA:(h + 1) * DVA])).astype(BF16)

    @pl.when(ci == nc - 1)
    def _():
        s_out_ref[0] = s_scr[...]


def gdn_prompt(proj, bsz, seq, conv_w, gcoef, dtb, norm_g):
    nc = seq // CHUNK
    c = CHUNK
    row = lambda b, i: b * nc + i
    return pl.pallas_call(
        _gdn_prompt_kernel,
        grid=(bsz, nc),
        in_specs=[pl.BlockSpec((c, GDN_CONV_DIM), lambda b, i: (row(b, i), OFF_QKVA // GDN_CONV_DIM)),
                  pl.BlockSpec((c, 512), lambda b, i: (row(b, i), OFF_ZA // 512)),
                  pl.BlockSpec((c, 512), lambda b, i: (row(b, i), OFF_SMALL // 512)),
                  pl.BlockSpec((CONV_W, GDN_CONV_DIM), lambda b, i: (0, 0)),
                  pl.BlockSpec((1, 128), lambda b, i: (0, 0)),
                  pl.BlockSpec((1, 128), lambda b, i: (0, 0)),
                  pl.BlockSpec((1, DVA), lambda b, i: (0, 0))],
        out_specs=[pl.BlockSpec((c, HA * DVA), lambda b, i: (row(b, i), 0)),
                   pl.BlockSpec((1, HA, DKA, DVA), lambda b, i: (b, 0, 0, 0))],
        out_shape=[jax.ShapeDtypeStruct((bsz * seq, HA * DVA), BF16),
                   jax.ShapeDtypeStruct((bsz, HA, DKA, DVA), F32)],
        scratch_shapes=[pltpu.VMEM((HA, DKA, DVA), F32), pltpu.VMEM((c, GDN_CONV_DIM), F32)],
        compiler_params=_cparams(("parallel", "arbitrary")),
        name="gdn_prompt",
    )(proj, proj, proj, conv_w, gcoef, dtb, norm_g.reshape(1, DVA))


SB_TQ = 512
SB_TK = 256


def _sb_prompt_kernel(bias_ref, q_ref, k_ref, v_ref, o_ref):
    h = pl.program_id(1)
    qi = pl.program_id(2)
    tq = q_ref.shape[0]
    tk = min(SB_TK, tq)
    bias = bias_ref[h]
    q = q_ref[...].astype(BF16)
    ri = _iota2((tq, tk), 0)
    cj = _iota2((tq, tk), 1)
    tri = (_iota2((tk, tk), 0) > _iota2((tk, tk), 1)).astype(BF16)
    nkb = (qi + 1) * (tq // tk)

    def body(jj, carry):
        acc, run = carry
        j = nkb - 1 - jj
        off = pl.multiple_of(j * tk, tk)
        kb = k_ref[pl.ds(off, tk), :].astype(BF16)
        vb = v_ref[pl.ds(off, tk), :].astype(BF16)
        z = _dot_nt(q, kb) * (DB ** -0.5) + bias
        mask = (j * tk + cj) < (qi * tq + ri)
        sp = _softplus_log(z)
        lf = jnp.where(mask, -sp, 0.0)
        hi, lo = _split2(lf)
        after = (_dot(hi, tri) + _dot(lo, tri)) + run
        a = jnp.where(mask, jnp.exp((z - sp) + after), 0.0)
        acc = acc + _dot(a.astype(BF16), vb)
        run = run + jnp.sum(lf, axis=-1, keepdims=True)
        return acc, run

    acc, _ = lax.fori_loop(0, nkb, body,
                           (jnp.zeros((tq, DB), F32), jnp.zeros((tq, 1), F32)))
    o_ref[...] = acc.astype(BF16)


def sb_prompt(proj, bsz, seq, bias):
    tq = min(SB_TQ, seq)
    nq = seq // tq
    return pl.pallas_call(
        _sb_prompt_kernel,
        grid=(bsz, HB, nq),
        in_specs=[pl.BlockSpec(memory_space=pltpu.SMEM),
                  pl.BlockSpec((tq, DB), lambda b, h, i: (b * nq + i, OFF_QB // DB + h)),
                  pl.BlockSpec((seq, DB), lambda b, h, i: (b, OFF_KB // DB + h)),
                  pl.BlockSpec((seq, DB), lambda b, h, i: (b, OFF_VB // DB + h))],
        out_specs=pl.BlockSpec((tq, DB), lambda b, h, i: (b * nq + i, h)),
        out_shape=jax.ShapeDtypeStruct((bsz * seq, HB * DB), BF16),
        compiler_params=_cparams(("parallel", "parallel", "arbitrary")),
        name="sb_prompt",
    )(bias, proj, proj, proj)


GLA_SUB = 16


def _gla_prompt_kernel(q_ref, k_ref, v_ref, r_ref, sm_ref, w2_ref, b2_ref, ng_ref,
                       o_ref, s_out_ref, s_scr):
    ci = pl.program_id(1)
    nc = pl.num_programs(1)

    @pl.when(ci == 0)
    def _():
        s_scr[...] = jnp.zeros_like(s_scr)

    c = CHUNK
    sm = sm_ref[:, 0:128]
    pre = _dot3(sm, w2_ref[...]) + b2_ref[...]
    loga = -_softplus(-pre) * (1.0 / GLA_TAU)
    ri = _iota2((c, c), 0)
    cj = _iota2((c, c), 1)
    incl = cj <= ri
    tril = incl.astype(BF16)
    bcum = _tri_dot_left(tril, loga)
    bcum_t = bcum.T
    qa = q_ref[...]
    ka = k_ref[...]
    va = v_ref[...]
    ra = r_ref[...]
    jrow = _iota2((c, DKC), 0)
    for h in range(HC):
        q = qa[:, h * DKC:(h + 1) * DKC] * (DKC ** -0.5)
        k = ka[:, h * DKC:(h + 1) * DKC]
        v = va[:, h * DVC:(h + 1) * DVC]
        bc = bcum[:, h * DKC:(h + 1) * DKC]
        rows = []
        for sb in range(c // GLA_SUB):
            i0 = sb * GLA_SUB
            ref = bc[i0:i0 + 1, :]
            qe = q[i0:i0 + GLA_SUB] * jnp.exp(bc[i0:i0 + GLA_SUB] - ref)
            ke = k * jnp.exp(jnp.where(jrow < i0 + GLA_SUB, ref - bc, 0.0))
            rows.append(_dotb_nt(qe, ke))
        attn = jnp.where(incl, jnp.concatenate(rows, axis=0), 0.0)
        s = s_scr[h]
        o = _dotb(q * jnp.exp(bc), s) + _dotb(attn, v)
        b_last = bc[c - 1:c, :]
        kd = k * jnp.exp(b_last - bc)
        b_last_col = bcum_t[h * DKC:(h + 1) * DKC, c - 1:c]
        s_scr[h] = jnp.exp(b_last_col) * s + _dotb(kd.T, v)
        on = o * lax.rsqrt(jnp.mean(o * o, axis=-1, keepdims=True) + EPS) * ng_ref[...]
        o_ref[:, h * DVC:(h + 1) * DVC] = (on * _silu(ra[:, h * DVC:(h + 1) * DVC])).astype(BF16)

    @pl.when(ci == nc - 1)
    def _():
        s_out_ref[0] = s_scr[...]


def gla_prompt(proj, bsz, seq, w2pad, b2, norm_g):
    nc = seq // CHUNK
    c = CHUNK
    row = lambda b, i: b * nc + i
    return pl.pallas_call(
        _gla_prompt_kernel,
        grid=(bsz, nc),
        in_specs=[pl.BlockSpec((c, HC * DKC), lambda b, i: (row(b, i), OFF_QC // (HC * DKC))),
                  pl.BlockSpec((c, HC * DKC), lambda b, i: (row(b, i), OFF_KC // (HC * DKC))),
                  pl.BlockSpec((c, HC * DVC), lambda b, i: (row(b, i), OFF_VC // (HC * DVC))),
                  pl.BlockSpec((c, HC * DVC), lambda b, i: (row(b, i), OFF_RC // (HC * DVC))),
                  pl.BlockSpec((c, 512), lambda b, i: (row(b, i), OFF_SMALL // 512)),
                  pl.BlockSpec((128, HC * DKC), lambda b, i: (0, 0)),
                  pl.BlockSpec((1, HC * DKC), lambda b, i: (0, 0)),
                  pl.BlockSpec((1, DVC), lambda b, i: (0, 0))],
        out_specs=[pl.BlockSpec((c, HC * DVC), lambda b, i: (row(b, i), 0)),
                   pl.BlockSpec((1, HC, DKC, DVC), lambda b, i: (b, 0, 0, 0))],
        out_shape=[jax.ShapeDtypeStruct((bsz * seq, HC * DVC), BF16),
                   jax.ShapeDtypeStruct((bsz, HC, DKC, DVC), F32)],
        scratch_shapes=[pltpu.VMEM((HC, DKC, DVC), F32)],
        compiler_params=_cparams(("parallel", "arbitrary")),
        name="gla_prompt",
    )(proj, proj, proj, proj, proj, w2pad, b2.reshape(1, HC * DKC), norm_g.reshape(1, DVC))


def _ssd_prompt_kernel(dvec_ref, xbc_ref, z_ref, sm_ref, cw_ref, cb_ref, arow_ref, dtb_ref, ng_ref,
                       o_ref, s_out_ref, s_scr, prev_scr):
    ci = pl.program_id(1)
    nc = pl.num_programs(1)

    @pl.when(ci == 0)
    def _():
        s_scr[...] = jnp.zeros_like(s_scr)
        prev_scr[...] = jnp.zeros_like(prev_scr)

    c = CHUNK
    xbc = _silu(_causal_conv_chunk(xbc_ref[...], prev_scr, cw_ref) + cb_ref[...])
    sm = sm_ref[:, 0:128]
    dt = _softplus(sm + dtb_ref[...])
    ri = _iota2((c, c), 0)
    cj = _iota2((c, c), 1)
    incl = cj <= ri
    tril = incl.astype(BF16)
    lcum = _tri_dot_left(tril, dt * arow_ref[...])
    lcum_t = lcum.T
    dt_t = dt.T
    z = z_ref[...]
    cb = []
    for g in range(NG):
        bg = xbc[:, HD * PD + g * NSTATE:HD * PD + (g + 1) * NSTATE]
        cg = xbc[:, HD * PD + NG * NSTATE + g * NSTATE:HD * PD + NG * NSTATE + (g + 1) * NSTATE]
        cb.append((bg, cg, _dotb_nt(cg, bg)))
    ys = []
    for h in range(HD):
        bg, cg, cbg = cb[h // (HD // NG)]
        xh = xbc[:, h * PD:(h + 1) * PD]
        lane = SM_DT + h
        li = lcum[:, lane:lane + 1]
        lj = lcum_t[lane:lane + 1, :]
        dti = dt[:, lane:lane + 1]
        dtj = dt_t[lane:lane + 1, :]
        decay = jnp.exp(jnp.where(incl, li - lj, -jnp.inf))
        scores = cbg * decay * dtj
        s = s_scr[h]
        y = _dotb(scores, xh) + _dotb_nt(cg * jnp.exp(li), s)
        l_last = lcum[c - 1:c, lane:lane + 1]
        xs = xh * (dti * jnp.exp(l_last - li))
        s_scr[h] = jnp.exp(l_last) * s + _dotb(xs.T, bg)
        ys.append((y + dvec_ref[h] * xh) * _silu(z[:, h * PD:(h + 1) * PD]))
    gw = HD * PD // NG
    for g in range(NG):
        yg = jnp.concatenate(ys[g * (HD // NG):(g + 1) * (HD // NG)], axis=-1)
        yn = yg * lax.rsqrt(jnp.mean(yg * yg, axis=-1, keepdims=True) + EPS)
        o_ref[:, g * gw:(g + 1) * gw] = (yn * ng_ref[:, g * gw:(g + 1) * gw]).astype(BF16)

    @pl.when(ci == nc - 1)
    def _():
        s_out_ref[0] = s_scr[...]


def ssd_prompt(proj, bsz, seq, conv_w, conv_b, arow, dtb, dvec, norm_g):
    nc = seq // CHUNK
    c = CHUNK
    row = lambda b, i: b * nc + i
    return pl.pallas_call(
        _ssd_prompt_kernel,
        grid=(bsz, nc),
        in_specs=[pl.BlockSpec(memory_space=pltpu.SMEM),
                  pl.BlockSpec((c, SSD_CONV_DIM), lambda b, i: (row(b, i), OFF_XBC // SSD_CONV_DIM)),
                  pl.BlockSpec((c, 512), lambda b, i: (row(b, i), OFF_ZD // 512)),
                  pl.BlockSpec((c, 512), lambda b, i: (row(b, i), OFF_SMALL // 512)),
                  pl.BlockSpec((CONV_W, SSD_CONV_DIM), lambda b, i: (0, 0)),
                  pl.BlockSpec((1, SSD_CONV_DIM), lambda b, i: (0, 0)),
                  pl.BlockSpec((1, 128), lambda b, i: (0, 0)),
                  pl.BlockSpec((1, 128), lambda b, i: (0, 0)),
                  pl.BlockSpec((1, HD * PD), lambda b, i: (0, 0))],
        out_specs=[pl.BlockSpec((c, HD * PD), lambda b, i: (row(b, i), 0)),
                   pl.BlockSpec((1, HD, PD, NSTATE), lambda b, i: (b, 0, 0, 0))],
        out_shape=[jax.ShapeDtypeStruct((bsz * seq, HD * PD), BF16),
                   jax.ShapeDtypeStruct((bsz, HD, PD, NSTATE), F32)],
        scratch_shapes=[pltpu.VMEM((HD, PD, NSTATE), F32), pltpu.VMEM((c, SSD_CONV_DIM), F32)],
        compiler_params=_cparams(("parallel", "arbitrary")),
        name="ssd_prompt",
    )(dvec, proj, proj, proj, conv_w, conv_b.reshape(1, SSD_CONV_DIM), arow, dtb,
      norm_g.reshape(1, HD * PD))


def _row8(r):
    return jnp.concatenate([r, jnp.zeros((7, r.shape[1]), F32)], axis=0)


def _sample_mixers_kernel(gneg_ref, gdtb_ref, aneg_ref, sdtb_ref, dvec_ref,
                          p_ref, gs_ref, gbuf_ref, ls_ref, ss_ref, sbuf_ref,
                          gcw_ref, gng_ref, w2_ref, b2_ref, lng_ref, scw_ref, scb_ref, sng_ref,
                          oa_ref, oc_ref, od_ref, gs_out, gbuf_out, ls_out, ss_out, sbuf_out):
    l = 0
    sm = p_ref[0, :, OFF_SMALL:OFF_SMALL + 128]

    u = p_ref[0, :, OFF_QKVA:OFF_QKVA + GDN_CONV_DIM]
    buf = gbuf_ref[0, 0]
    conv = (buf[0:1] * gcw_ref[0:1, :] + buf[1:2] * gcw_ref[1:2, :]
            + buf[2:3] * gcw_ref[2:3, :] + u * gcw_ref[3:4, :])
    gbuf_out[0, 0:2, :] = buf[1:3]
    gbuf_out[0, 2:3, :] = u
    x = _silu(conv)
    za = p_ref[0, :, OFF_ZA:OFF_ZA + HA * DVA]
    for h in range(HA):
        q = x[:, h * DKA:(h + 1) * DKA]
        k = x[:, HA * DKA + h * DKA:HA * DKA + (h + 1) * DKA]
        v = x[:, 2 * HA * DKA + h * DVA:2 * HA * DKA + (h + 1) * DVA]
        q = q * lax.rsqrt(jnp.sum(q * q, axis=-1, keepdims=True) + EPS) * (DKA ** -0.5)
        k = k * lax.rsqrt(jnp.sum(k * k, axis=-1, keepdims=True) + EPS)
        g = gneg_ref[h] * _softplus(sm[:, SM_A + h:SM_A + h + 1] + gdtb_ref[h])
        b = jax.nn.sigmoid(sm[:, SM_B + h:SM_B + h + 1])
        eg = jnp.exp(g)
        s = gs_ref[0, 0, h]
        lhs = jnp.concatenate([k, q * eg, jnp.zeros((6, DKA), F32)], axis=0)
        kq = _dot(lhs, s, HIGHEST)
        ks, qs = kq[0:1], kq[1:2]
        uu = b * v - (b * eg) * ks
        o = qs + jnp.sum(q * k, axis=-1, keepdims=True) * uu
        gs_out[0, h] = eg * s + _col_from_row(k) * uu
        on = o * lax.rsqrt(jnp.mean(o * o, axis=-1, keepdims=True) + EPS) * gng_ref[...]
        oa_ref[0, :, h * DVA:(h + 1) * DVA] = (on * _silu(za[:, h * DVA:(h + 1) * DVA])).astype(BF16)

    pre = _dot(jnp.broadcast_to(sm, (8, 128)), w2_ref[...], HIGHEST)[0:1] + b2_ref[...]
    loga = -_softplus(-pre) * (1.0 / GLA_TAU)
    qc = p_ref[0, :, OFF_QC:OFF_QC + HC * DKC]
    kc = p_ref[0, :, OFF_KC:OFF_KC + HC * DKC]
    vc = p_ref[0, :, OFF_VC:OFF_VC + HC * DVC]
    rc = p_ref[0, :, OFF_RC:OFF_RC + HC * DVC]
    for h in range(HC):
        q = qc[:, h * DKC:(h + 1) * DKC] * (DKC ** -0.5)
        k = kc[:, h * DKC:(h + 1) * DKC]
        v = vc[:, h * DVC:(h + 1) * DVC]
        ea = jnp.exp(loga[:, h * DKC:(h + 1) * DKC])
        s = ls_ref[0, 0, h]
        o = _dot(_row8(q * ea), s, HIGHEST)[0:1] + jnp.sum(q * k, axis=-1, keepdims=True) * v
        ls_out[0, h] = _col_from_row(ea) * s + _col_from_row(k) * v
        on = o * lax.rsqrt(jnp.mean(o * o, axis=-1, keepdims=True) + EPS) * lng_ref[...]
        oc_ref[0, :, h * DVC:(h + 1) * DVC] = (on * _silu(rc[:, h * DVC:(h + 1) * DVC])).astype(BF16)

    us = p_ref[0, :, OFF_XBC:OFF_XBC + SSD_CONV_DIM]
    sbuf = sbuf_ref[0, 0]
    sconv = (sbuf[0:1] * scw_ref[0:1, :] + sbuf[1:2] * scw_ref[1:2, :]
             + sbuf[2:3] * scw_ref[2:3, :] + us * scw_ref[3:4, :])
    sbuf_out[0, 0:2, :] = sbuf[1:3]
    sbuf_out[0, 2:3, :] = us
    xbc = _silu(sconv + scb_ref[...])
    zd = p_ref[0, :, OFF_ZD:OFF_ZD + HD * PD]
    ys = []
    for h in range(HD):
        g = h // (HD // NG)
        bg = xbc[:, HD * PD + g * NSTATE:HD * PD + (g + 1) * NSTATE]
        cg = xbc[:, HD * PD + NG * NSTATE + g * NSTATE:HD * PD + NG * NSTATE + (g + 1) * NSTATE]
        xh = xbc[:, h * PD:(h + 1) * PD]
        dt = _softplus(sm[:, SM_DT + h:SM_DT + h + 1] + sdtb_ref[h])
        lc = dt * aneg_ref[h]
        el = jnp.exp(lc)
        s = ss_ref[0, 0, h]
        score = jnp.sum(cg * bg, axis=-1, keepdims=True) * dt
        y = score * xh + _dot_nt(_row8(cg * el), s, HIGHEST)[0:1]
        ss_out[0, h] = el * s + _col_from_row(xh * dt) * bg
        ys.append((y + dvec_ref[h] * xh) * _silu(zd[:, h * PD:(h + 1) * PD]))
    gw = HD * PD // NG
    for g in range(NG):
        yg = jnp.concatenate(ys[g * (HD // NG):(g + 1) * (HD // NG)], axis=-1)
        yn = yg * lax.rsqrt(jnp.mean(yg * yg, axis=-1, keepdims=True) + EPS)
        od_ref[0, :, g * gw:(g + 1) * gw] = (yn * sng_ref[:, g * gw:(g + 1) * gw]).astype(BF16)
    del l


def sample_mixers(proj3, layer, state_gdn, state_gdn_conv, state_gla, state_ssd, state_ssd_conv, prm):
    nb = proj3.shape[0]
    smem = pl.BlockSpec(memory_space=pltpu.SMEM)
    full = lambda shape: pl.BlockSpec(shape, lambda b: (0,) * len(shape))
    l = layer
    outs = pl.pallas_call(
        _sample_mixers_kernel,
        grid=(nb,),
        in_specs=[smem, smem, smem, smem, smem,
                  pl.BlockSpec((1, 1, N_PROJ), lambda b: (b, 0, 0)),
                  pl.BlockSpec((1, 1, HA, DKA, DVA), lambda b: (l, b, 0, 0, 0)),
                  pl.BlockSpec((1, 1, CONV_W - 1, GDN_CONV_DIM), lambda b: (l, b, 0, 0)),
                  pl.BlockSpec((1, 1, HC, DKC, DVC), lambda b: (l, b, 0, 0, 0)),
                  pl.BlockSpec((1, 1, HD, PD, NSTATE), lambda b: (l, b, 0, 0, 0)),
                  pl.BlockSpec((1, 1, CONV_W - 1, SSD_CONV_DIM), lambda b: (l, b, 0, 0)),
                  full((CONV_W, GDN_CONV_DIM)), full((1, DVA)),
                  full((128, HC * DKC)), full((1, HC * DKC)), full((1, DVC)),
                  full((CONV_W, SSD_CONV_DIM)), full((1, SSD_CONV_DIM)), full((1, HD * PD))],
        out_specs=[pl.BlockSpec((1, 1, HA * DVA), lambda b: (b, 0, 0)),
                   pl.BlockSpec((1, 1, HC * DVC), lambda b: (b, 0, 0)),
                   pl.BlockSpec((1, 1, HD * PD), lambda b: (b, 0, 0)),
                   pl.BlockSpec((1, HA, DKA, DVA), lambda b: (b, 0, 0, 0)),
                   pl.BlockSpec((1, CONV_W - 1, GDN_CONV_DIM), lambda b: (b, 0, 0)),
                   pl.BlockSpec((1, HC, DKC, DVC), lambda b: (b, 0, 0, 0)),
                   pl.BlockSpec((1, HD, PD, NSTATE), lambda b: (b, 0, 0, 0)),
                   pl.BlockSpec((1, CONV_W - 1, SSD_CONV_DIM), lambda b: (b, 0, 0))],
        out_shape=[jax.ShapeDtypeStruct((nb, 1, HA * DVA), BF16),
                   jax.ShapeDtypeStruct((nb, 1, HC * DVC), BF16),
                   jax.ShapeDtypeStruct((nb, 1, HD * PD), BF16),
                   jax.ShapeDtypeStruct((nb, HA, DKA, DVA), F32),
                   jax.ShapeDtypeStruct((nb, CONV_W - 1, GDN_CONV_DIM), F32),
                   jax.ShapeDtypeStruct((nb, HC, DKC, DVC), F32),
                   jax.ShapeDtypeStruct((nb, HD, PD, NSTATE), F32),
                   jax.ShapeDtypeStruct((nb, CONV_W - 1, SSD_CONV_DIM), F32)],
        compiler_params=_cparams(("parallel",)),
        name="sample_mixers",
    )(prm["gdn_negA"], prm["gdn_dt_bias"], prm["ssd_negA"], prm["ssd_dt_bias"], prm["ssd_D"],
      proj3, state_gdn, state_gdn_conv, state_gla, state_ssd, state_ssd_conv,
      prm["gdn_conv_w"], prm["gdn_norm_g"].reshape(1, DVA),
      prm["w2pad"], prm["gla_b2"].reshape(1, HC * DKC), prm["gla_norm_g"].reshape(1, DVC),
      prm["ssd_conv_w"], prm["ssd_conv_b"].reshape(1, SSD_CONV_DIM),
      prm["ssd_norm_g"].reshape(1, HD * PD))
    return outs


SA_PAGES = 8


def _sample_attn_kernel(pt_ref, q_ref, brow_ref, *refs):
    k_refs = refs[0:SA_PAGES]
    v_refs = refs[SA_PAGES:2 * SA_PAGES]
    o_ref, acc_scr, run_scr = refs[2 * SA_PAGES:]
    j = pl.program_id(1)
    nj = pl.num_programs(1)

    @pl.when(j == 0)
    def _():
        acc_scr[...] = jnp.zeros_like(acc_scr)
        run_scr[...] = jnp.zeros_like(run_scr)

    w = PAGE_SIZE * HB
    q = q_ref[0]
    q8 = jnp.concatenate([q[:, h * DB:(h + 1) * DB] for h in range(HB)]
                         + [jnp.zeros((8 - HB, DB), F32)], axis=0).astype(BF16)
    rows = _iota2((8, w), 0)
    lanes = _iota2((8, w), 1)
    sel = (lanes % HB) == rows
    zs = []
    for p in range(SA_PAGES):
        kp = k_refs[p][0, 0].astype(BF16)
        zz = _dot_nt(q8, kp)
        zs.append(jnp.sum(jnp.where(sel, zz, 0.0), axis=0, keepdims=True))
    z = jnp.concatenate(zs, axis=0) * (DB ** -0.5) + brow_ref[...]
    sp = _softplus_log(z)
    lf = -sp
    suf = jnp.where(lanes < w - HB, pltpu.roll(lf, w - HB, 1), 0.0)
    tot = lf
    step = HB
    while step < w:
        suf = suf + jnp.where(lanes < w - step, pltpu.roll(suf, w - step, 1), 0.0)
        tot = tot + pltpu.roll(tot, step, 1)
        step *= 2
    run = run_scr[0:1, :]
    runs = []
    for p in range(SA_PAGES):
        runs.append(run)
        run = run + tot[p:p + 1, :]
    run_scr[...] = jnp.broadcast_to(run, run_scr.shape)
    a = jnp.exp((z - sp) + (suf + jnp.concatenate(runs, axis=0)))
    acc = acc_scr[...]
    for p in range(SA_PAGES):
        vp = v_refs[p][0, 0].astype(BF16)
        ap = jnp.where(sel, jnp.broadcast_to(a[p:p + 1, :], (8, w)), 0.0).astype(BF16)
        acc = acc + _dot(ap, vp)
    acc_scr[...] = acc

    @pl.when(j == nj - 1)
    def _():
        o_ref[0] = acc[0:HB].astype(BF16)


def sample_attn(proj3, layer, cache_k4, cache_v4, page_table, bias):
    nb, npg = page_table.shape
    l = layer
    nj = npg // SA_PAGES

    def page_spec(p):
        return pl.BlockSpec((1, 1, PAGE_SIZE * HB, DB),
                            lambda b, j, pt: (l, pt[b, npg - 1 - (j * SA_PAGES + p)], 0, 0))

    grid_spec = pltpu.PrefetchScalarGridSpec(
        num_scalar_prefetch=1,
        grid=(nb, nj),
        in_specs=([pl.BlockSpec((1, 1, HB * DB), lambda b, j, pt: (b, 0, OFF_QB // (HB * DB))),
                   pl.BlockSpec((1, PAGE_SIZE * HB), lambda b, j, pt: (0, 0))]
                  + [page_spec(p) for p in range(SA_PAGES)]
                  + [page_spec(p) for p in range(SA_PAGES)]),
        out_specs=pl.BlockSpec((1, HB, DB), lambda b, j, pt: (b, 0, 0)),
        scratch_shapes=[pltpu.VMEM((8, DB), F32), pltpu.VMEM((8, PAGE_SIZE * HB), F32)],
    )
    brow = jnp.tile(bias.astype(F32), PAGE_SIZE).reshape(1, PAGE_SIZE * HB)
    return pl.pallas_call(
        _sample_attn_kernel,
        grid_spec=grid_spec,
        out_shape=jax.ShapeDtypeStruct((nb, HB, DB), BF16),
        compiler_params=_cparams(("parallel", "arbitrary")),
        name="sample_attn",
    )(page_table, proj3, brow, *([cache_k4] * SA_PAGES), *([cache_v4] * SA_PAGES))


def _out_proj_kernel(a_ref, b_ref, c_ref, d_ref, w_ref, x_ref, g1_ref, n2_ref, sc_ref, sh_ref,
                     xo_ref, h_ref):
    gw = GROUP_WIDTH
    acc = _dot(a_ref[...], w_ref[0:gw, :])
    acc = acc + _dot(b_ref[...], w_ref[gw:2 * gw, :])
    acc = acc + _dot(c_ref[...], w_ref[2 * gw:3 * gw, :])
    acc = acc + _dot(d_ref[...], w_ref[3 * gw:4 * gw, :])
    x = x_ref[...] + g1_ref[0] * acc
    xo_ref[...] = x
    y = x * lax.rsqrt(jnp.mean(x * x, axis=-1, keepdims=True) + EPS) * n2_ref[...]
    h_ref[...] = (y * (1.0 + sc_ref[0]) + sh_ref[0]).astype(BF16)


def out_proj(oa, ob, oc, od, w_out, x, g1, n2, sc2, sh2, rows_per_batch):
    t, d = x.shape
    tm = _row_tile(t, 512, g1, rows_per_batch)
    part = pl.BlockSpec((tm, GROUP_WIDTH), lambda i: (i, 0))
    return pl.pallas_call(
        _out_proj_kernel,
        grid=(t // tm,),
        in_specs=[part, part, part, part,
                  pl.BlockSpec((d, d), lambda i: (0, 0)),
                  pl.BlockSpec((tm, d), lambda i: (i, 0)),
                  _mod_spec(g1, tm, rows_per_batch),
                  pl.BlockSpec((1, d), lambda i: (0, 0)),
                  _mod_spec(sc2, tm, rows_per_batch),
                  _mod_spec(sh2, tm, rows_per_batch)],
        out_specs=[pl.BlockSpec((tm, d), lambda i: (i, 0)),
                   pl.BlockSpec((tm, d), lambda i: (i, 0))],
        out_shape=[jax.ShapeDtypeStruct((t, d), F32), jax.ShapeDtypeStruct((t, d), BF16)],
        compiler_params=_cparams(("parallel",), VMEM_LIMIT),
        name="out_proj",
    )(oa, ob, oc, od, w_out, x, g1, n2.reshape(1, d), sc2, sh2)


ROUTE_SUB = 128


def _topk_rows(vals, k, ids=None):
    if ids is None:
        ids = _iota2(vals.shape, 0).astype(F32)
    out_v, out_i = [], []
    for _ in range(k):
        m = jnp.max(vals, axis=0, keepdims=True)
        idx = jnp.min(jnp.where(vals == m, ids, 1e9), axis=0, keepdims=True)
        out_v.append(m)
        out_i.append(idx)
        vals = jnp.where(ids == idx, -jnp.inf, vals)
    return jnp.concatenate(out_v, axis=0), jnp.concatenate(out_i, axis=0)


_CAND_ROWS = [(a, PEER_TOPK // (a + 1)) for a in range(PEER_TOPK)]
_N_CAND = sum(nb for _, nb in _CAND_ROWS)
_N_CAND_PAD = -(-_N_CAND // 8) * 8


def _cand_ids(n):
    r = _iota2((_N_CAND_PAD, n), 0)
    ids = jnp.full((_N_CAND_PAD, n), 1e9, F32)
    start = 0
    for a, nb in _CAND_ROWS:
        ids = jnp.where((r >= start) & (r < start + nb), (a * PEER_TOPK + r - start).astype(F32), ids)
        start += nb
    return ids


def _select_rows(table, sel):
    out = jnp.zeros(sel.shape, F32)
    for a in range(table.shape[0]):
        out = jnp.where(sel == float(a), table[a:a + 1, :], out)
    return out


def _peer_route_kernel(h_ref, wq_ref, sk_ref, e1_ref, e2_ref, gt_ref, q_scr, e1_scr, e2_scr, gt_scr):
    tm = h_ref.shape[0]
    q = _dot(h_ref[...], wq_ref[...])
    for cgrp in range(2 * PEER_HEADS):
        q_scr[cgrp] = q[:, cgrp * PK_DIM:(cgrp + 1) * PK_DIM]
    sk0 = sk_ref[0]
    sk1 = sk_ref[1]
    kk = PEER_TOPK

    n = min(tm, ROUTE_SUB)
    cand_ids = _cand_ids(n)
    for sub in range(tm // n):
        r0 = sub * n

        def head_body(hd, _):
            qa = q_scr[2 * hd, r0:r0 + n, :]
            qb = q_scr[2 * hd + 1, r0:r0 + n, :]
            s0 = _dot_nt(sk0, qa)
            s1 = _dot_nt(sk1, qb)
            sv0, si0 = _topk_rows(s0, kk)
            sv1, si1 = _topk_rows(s1, kk)
            cand = jnp.concatenate(
                [sv0[a:a + 1, :] + sv1[0:nb, :] for a, nb in _CAND_ROWS]
                + [jnp.full((_N_CAND_PAD - _N_CAND, n), -jnp.inf, F32)], axis=0)
            cv, cidx = _topk_rows(cand, kk, cand_ids)
            ia = jnp.floor(cidx * (1.0 / kk))
            ib = cidx - ia * kk
            e1 = _select_rows(si0, ia)
            e2 = _select_rows(si1, ib)
            ex = jnp.exp(cv - jnp.max(cv, axis=0, keepdims=True))
            gates = ex / jnp.sum(ex, axis=0, keepdims=True)
            ro = pl.multiple_of(hd * kk, kk)
            e1_scr[pl.ds(ro, kk), r0:r0 + n] = e1
            e2_scr[pl.ds(ro, kk), r0:r0 + n] = e2
            gt_scr[pl.ds(ro, kk), r0:r0 + n] = gates
            return 0

        lax.fori_loop(0, PEER_HEADS, head_body, 0)

    e1_ref[...] = e1_scr[...].T
    e2_ref[...] = e2_scr[...].T
    gt_ref[...] = gt_scr[...].T


def peer_route(h2, wq, sub_keys):
    t, d = h2.shape
    tm = min(t, 256)
    nq = wq.shape[1]
    nj = PEER_HEADS * PEER_TOPK
    out = jax.ShapeDtypeStruct((t, nj), F32)
    ospec = pl.BlockSpec((tm, nj), lambda i: (i, 0))
    return pl.pallas_call(
        _peer_route_kernel,
        grid=(t // tm,),
        in_specs=[pl.BlockSpec((tm, d), lambda i: (i, 0)),
                  pl.BlockSpec((d, nq), lambda i: (0, 0)),
                  pl.BlockSpec((2, N_KEYS, PK_DIM), lambda i: (0, 0, 0))],
        out_specs=[ospec, ospec, ospec],
        out_shape=[out, out, out],
        scratch_shapes=[pltpu.VMEM((2 * PEER_HEADS, tm, PK_DIM), F32),
                        pltpu.VMEM((nj, tm), F32), pltpu.VMEM((nj, tm), F32),
                        pltpu.VMEM((nj, tm), F32)],
        compiler_params=_cparams(("parallel",), VMEM_LIMIT),
        name="peer_route",
    )(h2, wq, sub_keys)


GATE_GRP = 16


def _build_gates(e1_ref, e2_ref, gt_ref, g_scr, stage_scr):
    tm = e1_ref.shape[0]
    nj = PEER_HEADS * PEER_TOPK
    riota = _iota2((N_KEYS, nj), 0).astype(F32)

    def group(gi, _):
        t0 = pl.multiple_of(gi * GATE_GRP, GATE_GRP)
        for tt in range(GATE_GRP):
            e1 = e1_ref[pl.ds(t0 + tt, 1), :]
            e2 = e2_ref[pl.ds(t0 + tt, 1), :]
            g = gt_ref[pl.ds(t0 + tt, 1), :]
            pt = jnp.where(riota == e1, g, 0.0).astype(BF16)
            qt = jnp.where(riota == e2, 1.0, 0.0).astype(BF16)
            stage_scr[tt * N_KEYS:(tt + 1) * N_KEYS, :] = _dot_nt(pt, qt)
        for e1i in range(N_KEYS):
            rows = stage_scr[pl.ds(e1i, GATE_GRP, stride=N_KEYS), :]
            g_scr[e1i, pl.ds(t0, GATE_GRP), :] = rows.astype(BF16)
        return 0

    lax.fori_loop(0, tm // GATE_GRP, group, 0)


PEER_TE = 512
PEER_TE_SUB = 256


def _peer_dense_kernel(h_ref, u_ref, v_ref, e1_ref, e2_ref, gt_ref, x_ref, g2_ref, o_ref,
                       g_scr, stage_scr):
    e = pl.program_id(1)
    ne = pl.num_programs(1)

    @pl.when(e == 0)
    def _():
        o_ref[...] = jnp.zeros_like(o_ref)
        _build_gates(e1_ref, e2_ref, gt_ref, g_scr, stage_scr)

    h = h_ref[...]
    part = None
    per_sub = PEER_TE_SUB // N_KEYS
    for c in range(PEER_TE // PEER_TE_SUB):
        rows = slice(c * PEER_TE_SUB, (c + 1) * PEER_TE_SUB)
        a = _dot_nt(h, u_ref[rows, :])
        act = a * (lax.erf(a * (0.5 ** 0.5)) + 1.0) * 0.5
        g0 = e * (PEER_TE // N_KEYS) + c * per_sub
        g = jnp.concatenate([g_scr[g0 + k] for k in range(per_sub)], axis=1)
        hh = (g.astype(F32) * act).astype(BF16)
        p = _dot(hh, v_ref[rows, :])
        part = p if part is None else part + p
    o_ref[...] += part

    @pl.when(e == ne - 1)
    def _():
        o_ref[...] = x_ref[...] + g2_ref[0] * o_ref[...]


def peer_dense(h2, u, v, e1, e2, gt, x, g2, rows_per_batch):
    t, d = x.shape
    tm = _row_tile(t, 512, g2, rows_per_batch)
    te = PEER_TE
    nj = PEER_HEADS * PEER_TOPK
    rspec = pl.BlockSpec((tm, nj), lambda i, e: (i, 0))
    return pl.pallas_call(
        _peer_dense_kernel,
        grid=(t // tm, N_EXPERTS // te),
        in_specs=[pl.BlockSpec((tm, d), lambda i, e: (i, 0)),
                  pl.BlockSpec((te, d), lambda i, e: (e, 0)),
                  pl.BlockSpec((te, d), lambda i, e: (e, 0)),
                  rspec, rspec, rspec,
                  pl.BlockSpec((tm, d), lambda i, e: (i, 0)),
                  _mod_spec(g2, tm, rows_per_batch)],
        out_specs=pl.BlockSpec((tm, d), lambda i, e: (i, 0)),
        out_shape=jax.ShapeDtypeStruct((t, d), F32),
        scratch_shapes=[pltpu.VMEM((N_KEYS, tm, N_KEYS), BF16),
                        pltpu.VMEM((GATE_GRP * N_KEYS, N_KEYS), F32)],
        compiler_params=_cparams(("parallel", "arbitrary"), VMEM_LIMIT),
        name="peer_dense",
    )(h2, u, v, e1, e2, gt, x, g2)


def _final_norm_kernel(x_ref, g_ref, o_ref):
    x = x_ref[...]
    o_ref[...] = x * lax.rsqrt(jnp.mean(x * x, axis=-1, keepdims=True) + EPS) * g_ref[...]


def final_norm(x, g):
    t, d = x.shape
    tm = min(t, 512)
    return pl.pallas_call(
        _final_norm_kernel,
        grid=(t // tm,),
        in_specs=[pl.BlockSpec((tm, d), lambda i: (i, 0)), pl.BlockSpec((1, d), lambda i: (0, 0))],
        out_specs=pl.BlockSpec((tm, d), lambda i: (i, 0)),
        out_shape=jax.ShapeDtypeStruct((t, d), F32),
        compiler_params=_cparams(("parallel",)),
        name="final_norm",
    )(x, g.reshape(1, d))


def _permute_w_in(w):
    d = w.shape[0]
    small = jnp.concatenate([w[:, 2048:2056], w[:, 5128:5144], w[:, 6680:6688]], axis=1)
    pad = jnp.zeros((d, N_PROJ - OFF_SMALL - small.shape[1]), w.dtype)
    return jnp.concatenate([w[:, 0:2048], w[:, 2056:5128], w[:, 5144:6680], small, pad],
                           axis=1).astype(BF16)


def _small_row(vals, off):
    return jnp.zeros((1, 128), F32).at[0, off:off + vals.shape[0]].set(vals.astype(F32))


def _layer_params(l, ada_w, ada_b, norm1_g, norm2_g, w_in, w_out, gdn_conv_w, gdn_A_log, gdn_dt_bias,
                  gdn_norm_g, sb_bias, gla_w2, gla_b2, gla_norm_g, ssd_conv_w, ssd_conv_b, ssd_A_log,
                  ssd_dt_bias, ssd_D, ssd_norm_g, peer_w_query, peer_sub_keys, peer_u, peer_v):
    del ada_w, ada_b
    w2pad = jnp.zeros((128, HC * DKC), F32).at[SM_G:SM_G + GLA_RANK, :].set(gla_w2[l])
    return {
        "norm1_g": norm1_g[l], "norm2_g": norm2_g[l],
        "w_in": _permute_w_in(w_in[l]), "w_out": w_out[l].astype(BF16),
        "gdn_conv_w": gdn_conv_w[l], "gdn_negA": -jnp.exp(gdn_A_log[l]), "gdn_dt_bias": gdn_dt_bias[l],
        "gdn_gcoef": _small_row(-jnp.exp(gdn_A_log[l]), SM_A),
        "gdn_dtb": _small_row(gdn_dt_bias[l], SM_A),
        "gdn_norm_g": gdn_norm_g[l], "sb_bias": sb_bias[l],
        "w2pad": w2pad, "gla_b2": gla_b2[l], "gla_norm_g": gla_norm_g[l],
        "ssd_conv_w": ssd_conv_w[l], "ssd_conv_b": ssd_conv_b[l],
        "ssd_negA": -jnp.exp(ssd_A_log[l]), "ssd_dt_bias": ssd_dt_bias[l],
        "ssd_arow": _small_row(-jnp.exp(ssd_A_log[l]), SM_DT),
        "ssd_dtb": _small_row(ssd_dt_bias[l], SM_DT),
        "ssd_D": ssd_D[l], "ssd_norm_g": ssd_norm_g[l],
        "wq": peer_w_query[l].astype(BF16), "sub_keys": peer_sub_keys[l],
        "peer_u": peer_u[l].astype(BF16), "peer_v": peer_v[l].astype(BF16),
    }


def _split_mod(mod):
    return [mod[:, i * D_MODEL:(i + 1) * D_MODEL] for i in range(6)]


def _peer_block(h2, x, g2, prm, rows_per_batch):
    e1, e2, gt = peer_route(h2, prm["wq"], prm["sub_keys"])
    return peer_dense(h2, prm["peer_u"], prm["peer_v"], e1, e2, gt, x, g2, rows_per_batch)


def _peer_block_padded(h2, x, g2, prm):
    t = x.shape[0]
    tpad = -(-t // ROUTE_SUB) * ROUTE_SUB
    pad = lambda a: jnp.concatenate([a, jnp.zeros((tpad - t,) + a.shape[1:], a.dtype)], axis=0)
    g2p = pad(g2.reshape(t, -1)).reshape(1, tpad, -1)
    return _peer_block(pad(h2), pad(x), g2p, prm, 1)[:t]


def kernel(x_prompt, x_sample, cache_k, cache_v, state_gdn, state_gdn_conv, state_gla, state_ssd, state_ssd_conv, page_table, c_prompt, c_sample, ada_w, ada_b, norm1_g, norm2_g, w_in, w_out, gdn_conv_w, gdn_A_log, gdn_dt_bias, gdn_norm_g, sb_bias, gla_w2, gla_b2, gla_norm_g, ssd_conv_w, ssd_conv_b, ssd_A_log, ssd_dt_bias, ssd_D, ssd_norm_g, peer_w_query, peer_sub_keys, peer_u, peer_v, final_norm_g):
    bsz, seq, d = x_prompt.shape
    nb = x_sample.shape[0]
    tp = bsz * seq
    n_pool = cache_k.shape[1]
    cache_k4 = cache_k.reshape(DEPTH, n_pool, PAGE_SIZE * HB, DB)
    cache_v4 = cache_v.reshape(DEPTH, n_pool, PAGE_SIZE * HB, DB)

    n_c = bsz + nb
    r_pad = -(-n_c // 8) * 8
    c_all = jnp.concatenate([c_prompt, c_sample, jnp.zeros((r_pad - n_c, d), F32)], axis=0)
    mod = ada_mod(c_all, ada_w, ada_b)

    xp = x_prompt.reshape(tp, d)
    xs = x_sample.reshape(nb, d)
    outs_p, outs_s = [], []
    for l in range(DEPTH):
        prm = _layer_params(l, ada_w, ada_b, norm1_g, norm2_g, w_in, w_out, gdn_conv_w, gdn_A_log,
                            gdn_dt_bias, gdn_norm_g, sb_bias, gla_w2, gla_b2, gla_norm_g, ssd_conv_w,
                            ssd_conv_b, ssd_A_log, ssd_dt_bias, ssd_D, ssd_norm_g, peer_w_query,
                            peer_sub_keys, peer_u, peer_v)
        mp = [m.reshape(bsz, 1, d) for m in _split_mod(mod[l, 0:bsz])]
        ms = [m.reshape(1, nb, d) for m in _split_mod(mod[l, bsz:bsz + nb])]

        proj = in_proj(xp, prm["norm1_g"], mp[1], mp[0], prm["w_in"], seq)
        oa, gdn_s = gdn_prompt(proj, bsz, seq, prm["gdn_conv_w"], prm["gdn_gcoef"], prm["gdn_dtb"],
                               prm["gdn_norm_g"])
        ob = sb_prompt(proj, bsz, seq, prm["sb_bias"])
        oc, gla_s = gla_prompt(proj, bsz, seq, prm["w2pad"], prm["gla_b2"], prm["gla_norm_g"])
        od, ssd_s = ssd_prompt(proj, bsz, seq, prm["ssd_conv_w"], prm["ssd_conv_b"], prm["ssd_arow"],
                               prm["ssd_dtb"], prm["ssd_D"], prm["ssd_norm_g"])
        p3 = proj.reshape(bsz, seq, N_PROJ)
        outs_p.append((p3[:, :, OFF_KB:OFF_KB + HB * DB].reshape(bsz, seq, HB, DB),
                       p3[:, :, OFF_VB:OFF_VB + HB * DB].reshape(bsz, seq, HB, DB),
                       gdn_s, p3[:, seq - (CONV_W - 1):, OFF_QKVA:OFF_QKVA + GDN_CONV_DIM],
                       gla_s, ssd_s, p3[:, seq - (CONV_W - 1):, OFF_XBC:OFF_XBC + SSD_CONV_DIM]))
        xp, h2 = out_proj(oa, ob, oc, od, prm["w_out"], xp, mp[2], prm["norm2_g"], mp[4], mp[3], seq)
        xp = _peer_block(h2, xp, mp[5], prm, seq)

        proj_s = in_proj(xs, prm["norm1_g"], ms[1], ms[0], prm["w_in"], 1)
        ps3 = proj_s.reshape(nb, 1, N_PROJ)
        (oa_s, oc_s, od_s, gdn_n, gbuf_n, gla_n, ssd_n, sbuf_n) = sample_mixers(
            ps3, l, state_gdn, state_gdn_conv, state_gla, state_ssd, state_ssd_conv, prm)
        ob_s = sample_attn(ps3, l, cache_k4, cache_v4, page_table, prm["sb_bias"])
        outs_s.append((proj_s[:, OFF_KB:OFF_KB + HB * DB].reshape(nb, 1, HB, DB),
                       proj_s[:, OFF_VB:OFF_VB + HB * DB].reshape(nb, 1, HB, DB),
                       gdn_n, gbuf_n, gla_n, ssd_n, sbuf_n))
        xs, h2s = out_proj(oa_s.reshape(nb, -1), ob_s.reshape(nb, -1), oc_s.reshape(nb, -1),
                           od_s.reshape(nb, -1), prm["w_out"], xs, ms[2], prm["norm2_g"], ms[4], ms[3], 1)
        xs = _peer_block_padded(h2s, xs, ms[5], prm)

    y_prompt = final_norm(xp, final_norm_g).reshape(bsz, seq, d)
    y_sample = final_norm(xs, final_norm_g).reshape(nb, 1, d)
    stk = lambda lst, i: jnp.stack([s[i] for s in lst], axis=0)
    return (y_prompt, y_sample, stk(outs_p, 0), stk(outs_p, 1), stk(outs_s, 0), stk(outs_s, 1),
            stk(outs_p, 2), stk(outs_s, 2), stk(outs_p, 3), stk(outs_s, 3), stk(outs_p, 4), stk(outs_s, 4),
            stk(outs_p, 5), stk(outs_s, 5), stk(outs_p, 6), stk(outs_s, 6))
```

```python
import functools
import math

import jax
import jax.numpy as jnp
import numpy as np
from jax import lax
from jax.experimental import pallas as pl
from jax.experimental.pallas import tpu as pltpu

F32 = jnp.float32
BF16 = jnp.bfloat16
HIGHEST = lax.Precision.HIGHEST

D_MODEL = 2048
DEPTH = 2
PAGE_SIZE = 128
GROUP_WIDTH = D_MODEL // 4
HA, DKA, DVA = 4, 128, 128
HB, DB = 4, 128
HC, DKC, DVC = 4, 64, 128
GLA_RANK = 16
GLA_TAU = 16.0
HD, PD, NG, NSTATE = 8, 64, 2, 128
CONV_W = 4
CHUNK = 64
SB_BLOCK = 128
PEER_HEADS = 8
N_KEYS = 128
N_EXPERTS = N_KEYS * N_KEYS
PK_DIM = 128
PEER_TOPK = 16
EPS = 1e-6
GDN_CONV_DIM = 2 * HA * DKA + HA * DVA
SSD_CONV_DIM = HD * PD + 2 * NG * NSTATE

OFF_QKVA = 0
OFF_ZA = 1536
OFF_QB = 2048
OFF_KB = 2560
OFF_VB = 3072
OFF_QC = 3584
OFF_KC = 3840
OFF_VC = 4096
OFF_RC = 4608
OFF_XBC = 5120
OFF_ZD = 6144
OFF_SMALL = 6656
SM_A, SM_B, SM_G, SM_DT = 0, 4, 8, 24
N_PROJ = 7168

VMEM_LIMIT = 56 * 1024 * 1024


def _cparams(sem, vmem=None):
    return pltpu.CompilerParams(dimension_semantics=sem, vmem_limit_bytes=vmem)


def _softplus(x):
    return jnp.maximum(x, 0.0) + jnp.log1p(jnp.exp(-jnp.abs(x)))


def _softplus_log(x):
    return jnp.maximum(x, 0.0) + jnp.log(1.0 + jnp.exp(-jnp.abs(x)))


def _silu(x):
    return x * jax.nn.sigmoid(x)


def _dotb(a, b):
    return _dot(a.astype(BF16), b.astype(BF16))


def _dotb_nt(a, b):
    return _dot_nt(a.astype(BF16), b.astype(BF16))


def _split2(x):
    hi = x.astype(BF16)
    return hi, (x - hi.astype(F32)).astype(BF16)


def _dot3(a, b):
    ah, al = _split2(a)
    bh, bl = _split2(b)
    return _dot(ah, bh) + (_dot(ah, bl) + _dot(al, bh))


def _dot(a, b, precision=None):
    return jnp.dot(a, b, precision=precision, preferred_element_type=F32)


def _dot_nt(a, b, precision=None):
    return lax.dot_general(a, b, (((1,), (1,)), ((), ())), precision=precision,
                           preferred_element_type=F32)


def _split3(x):
    hi = x.astype(BF16)
    r1 = x - hi.astype(F32)
    mid = r1.astype(BF16)
    lo = (r1 - mid.astype(F32)).astype(BF16)
    return hi, mid, lo


def _tri_dot_left(tri_bf16, x):
    hi, mid, lo = _split3(x)
    return (_dot(tri_bf16, hi) + _dot(tri_bf16, mid)) + _dot(tri_bf16, lo)


def _tri_dot_right(x, tri_bf16):
    hi, mid, lo = _split3(x)
    return (_dot(hi, tri_bf16) + _dot(mid, tri_bf16)) + _dot(lo, tri_bf16)


def _iota2(shape, dim):
    return lax.broadcasted_iota(jnp.int32, shape, dim)


def _col_from_row(r):
    n = r.shape[1]
    eye = _iota2((n, n), 0) == _iota2((n, n), 1)
    return jnp.sum(jnp.where(eye, jnp.broadcast_to(r, (n, n)), 0.0), axis=1, keepdims=True)


def _ada_kernel(c_ref, w_ref, b_ref, o_ref):
    c = c_ref[...]
    o_ref[0] = _dot(_silu(c), w_ref[0]) + b_ref[0]


def ada_mod(c_all, ada_w, ada_b):
    r = c_all.shape[0]
    n = ada_w.shape[2]
    tn = 1024
    return pl.pallas_call(
        _ada_kernel,
        grid=(DEPTH, n // tn),
        in_specs=[pl.BlockSpec((r, D_MODEL), lambda l, j: (0, 0)),
                  pl.BlockSpec((1, D_MODEL, tn), lambda l, j: (l, 0, j)),
                  pl.BlockSpec((1, 1, tn), lambda l, j: (l, 0, j))],
        out_specs=pl.BlockSpec((1, r, tn), lambda l, j: (l, 0, j)),
        out_shape=jax.ShapeDtypeStruct((DEPTH, r, n), F32),
        compiler_params=_cparams(("parallel", "parallel"), VMEM_LIMIT),
        name="ada_mod",
    )(c_all, ada_w, ada_b.reshape(DEPTH, 1, n))


def _in_proj_kernel(x_ref, g_ref, sc_ref, sh_ref, w_ref, o_ref, h_scr):
    @pl.when(pl.program_id(1) == 0)
    def _():
        x = x_ref[...]
        y = x * lax.rsqrt(jnp.mean(x * x, axis=-1, keepdims=True) + EPS) * g_ref[...]
        h_scr[...] = (y * (1.0 + sc_ref[0]) + sh_ref[0]).astype(BF16)

    o_ref[...] = _dot(h_scr[...], w_ref[...])


def _row_tile(t, cap, mod, rows_per_batch):
    return min(t, cap, rows_per_batch) if mod.shape[1] == 1 else min(t, cap)


def _mod_spec(mod, tm, rows_per_batch):
    nb, r, d = mod.shape
    if r == 1:
        return pl.BlockSpec((1, 1, d), lambda i, *_: ((i * tm) // rows_per_batch, 0, 0))
    return pl.BlockSpec((1, r, d), lambda i, *_: (0, 0, 0))


def in_proj(x, g, sc, sh, w, rows_per_batch):
    t, d = x.shape
    n = w.shape[1]
    tm = _row_tile(t, 512, sc, rows_per_batch)
    tn = 1024
    return pl.pallas_call(
        _in_proj_kernel,
        grid=(t // tm, n // tn),
        in_specs=[pl.BlockSpec((tm, d), lambda i, j: (i, 0)),
                  pl.BlockSpec((1, d), lambda i, j: (0, 0)),
                  _mod_spec(sc, tm, rows_per_batch),
                  _mod_spec(sh, tm, rows_per_batch),
                  pl.BlockSpec((d, tn), lambda i, j: (0, j))],
        out_specs=pl.BlockSpec((tm, tn), lambda i, j: (i, j)),
        out_shape=jax.ShapeDtypeStruct((t, n), F32),
        scratch_shapes=[pltpu.VMEM((tm, d), BF16)],
        compiler_params=_cparams(("parallel", "arbitrary"), VMEM_LIMIT),
        name="in_proj",
    )(x, g.reshape(1, d), sc, sh, w)


def _causal_conv_chunk(u, prev_ref, w_ref):
    c = u.shape[0]
    rows = _iota2(u.shape, 0)
    prev = prev_ref[...]
    out = u * w_ref[CONV_W - 1:CONV_W, :]
    for k in range(1, CONV_W):
        shifted = jnp.where(rows >= k, pltpu.roll(u, k, 0), pltpu.roll(prev, k, 0))
        out = out + shifted * w_ref[CONV_W - 1 - k:CONV_W - k, :]
    prev_ref[...] = u
    del c
    return out


def _inv_unit_lower_many(ms):
    c = ms[0].shape[0]
    n = range(len(ms))
    eye = (_iota2((c, c), 0) == _iota2((c, c), 1)).astype(F32)
    x = [-m for m in ms]
    p = [eye + x[i] for i in n]
    steps = int(math.ceil(math.log2(c))) - 1
    xs = [_split2(x[i]) for i in n]
    for _ in range(steps):
        x = [_dot(xs[i][0], xs[i][0]) + (_dot(xs[i][0], xs[i][1]) + _dot(xs[i][1], xs[i][0])) for i in n]
        xs = [_split2(x[i]) for i in n]
        ps = [_split2(p[i]) for i in n]
        p = [p[i] + (_dot(ps[i][0], xs[i][0]) + (_dot(ps[i][0], xs[i][1]) + _dot(ps[i][1], xs[i][0])))
             for i in n]
    return p


def _run_sequences(body, seq_refs, shared_refs, s_out_ref, scratch):
    ci = pl.program_id(1)
    nc = pl.num_programs(1)

    @pl.when(ci == 0)
    def _():
        for s in scratch:
            s[...] = jnp.zeros_like(s)

    nseq = s_out_ref.shape[0]
    body([tuple(r.at[0, sq] for r in seq_refs) for sq in range(nseq)], shared_refs,
         [tuple(s.at[sq] for s in scratch) for sq in range(nseq)])

    @pl.when(ci == nc - 1)
    def _():
        s_out_ref[...] = scratch[0][...]


def _gdn_prompt_kernel(qkv_ref, z_ref, sm_ref, cw_ref, gcoef_ref, dtb_ref, ng_ref,
                       o_ref, s_out_ref, s_scr, prev_scr):
    _run_sequences(_gdn_chunk, (qkv_ref, z_ref, sm_ref, o_ref), (cw_ref, gcoef_ref, dtb_ref, ng_ref),
                   s_out_ref, (s_scr, prev_scr))


def _gdn_chunk(seqs, shared, scr):
    cw_ref, gcoef_ref, dtb_ref, ng_ref = shared
    c = CHUNK
    ri = _iota2((c, c), 0)
    cj = _iota2((c, c), 1)
    incl = cj <= ri
    strict = cj < ri
    tril = incl.astype(BF16)
    q, k, v, gc, gr, bc, g_last, zs, outs, states = [], [], [], [], [], [], [], [], [], []
    for (qkv_ref, z_ref, sm_ref, o_ref), (s_scr, prev_scr) in zip(seqs, scr):
        x = _silu(_causal_conv_chunk(qkv_ref[...], prev_scr, cw_ref))
        sm = sm_ref[:, 0:128]
        g = gcoef_ref[...] * _softplus(sm + dtb_ref[...])
        beta = jax.nn.sigmoid(sm)
        gcum = _tri_dot_left(tril, g)
        gcum_t = gcum.T
        z = z_ref[...]
        for h in range(HA):
            q.append(x[:, h * DKA:(h + 1) * DKA])
            k.append(x[:, HA * DKA + h * DKA:HA * DKA + (h + 1) * DKA])
            v.append(x[:, 2 * HA * DKA + h * DVA:2 * HA * DKA + (h + 1) * DVA])
            gc.append(gcum[:, SM_A + h:SM_A + h + 1])
            gr.append(gcum_t[SM_A + h:SM_A + h + 1, :])
            bc.append(beta[:, SM_B + h:SM_B + h + 1])
            g_last.append(gcum[c - 1:c, SM_A + h:SM_A + h + 1])
            zs.append(z[:, h * DVA:(h + 1) * DVA])
            outs.append((o_ref, h))
            states.append((s_scr, h))
    n = range(len(q))
    q = [q[i] * lax.rsqrt(jnp.sum(q[i] * q[i], axis=-1, keepdims=True) + EPS) * (DKA ** -0.5) for i in n]
    k = [k[i] * lax.rsqrt(jnp.sum(k[i] * k[i], axis=-1, keepdims=True) + EPS) for i in n]
    decay = [jnp.exp(jnp.where(incl, gc[i] - gr[i], -jnp.inf)) for i in n]
    eg = [jnp.exp(gc[i]) for i in n]
    kb = [k[i].astype(BF16) for i in n]
    kk = [_dot_nt(kb[i], kb[i]) for i in n]
    m = [jnp.where(strict, decay[i], 0.0) * kk[i] * bc[i] for i in n]
    tinv = _inv_unit_lower_many(m)
    tb = [tinv[i].astype(BF16) for i in n]
    w = [_dot(tb[i], ((bc[i] * eg[i]) * k[i]).astype(BF16)) for i in n]
    u = [_dot(tb[i], (bc[i] * v[i]).astype(BF16)) for i in n]
    s = [ref[h] for ref, h in states]
    sb = [s[i].astype(BF16) for i in n]
    u = [u[i] - _dot(w[i].astype(BF16), sb[i]) for i in n]
    ub = [u[i].astype(BF16) for i in n]
    attn = [_dot_nt(q[i].astype(BF16), kb[i]) * decay[i] for i in n]
    o = [_dot((q[i] * eg[i]).astype(BF16), sb[i]) + _dot(attn[i].astype(BF16), ub[i]) for i in n]
    kd = [(k[i] * jnp.exp(g_last[i] - gc[i])).T.astype(BF16) for i in n]
    s_new = [jnp.exp(g_last[i]) * s[i] + _dot(kd[i], ub[i]) for i in n]
    for i in n:
        ref, h = states[i]
        ref[h] = s_new[i]
    for i in n:
        on = o[i] * lax.rsqrt(jnp.mean(o[i] * o[i], axis=-1, keepdims=True) + EPS) * ng_ref[...]
        ref, h = outs[i]
        ref[:, h * DVA:(h + 1) * DVA] = (on * _silu(zs[i])).astype(BF16)


def _seqs_per_step(bsz):
    return 2 if bsz % 2 == 0 else 1


def _seq_spec(nseq, width, col_block):
    return pl.BlockSpec((1, nseq, CHUNK, width), lambda b, i: (b, 0, i, col_block))


def _fixed_spec(shape):
    return pl.BlockSpec(shape, lambda b, i: (0,) * len(shape))


def gdn_prompt(proj, bsz, seq, conv_w, gcoef, dtb, norm_g):
    nc = seq // CHUNK
    c = CHUNK
    ns = _seqs_per_step(bsz)
    proj4 = proj.reshape(bsz // ns, ns, seq, N_PROJ)
    o, s = pl.pallas_call(
        _gdn_prompt_kernel,
        grid=(bsz // ns, nc),
        in_specs=[_seq_spec(ns, GDN_CONV_DIM, OFF_QKVA // GDN_CONV_DIM),
                  _seq_spec(ns, 512, OFF_ZA // 512),
                  _seq_spec(ns, 512, OFF_SMALL // 512),
                  _fixed_spec((CONV_W, GDN_CONV_DIM)), _fixed_spec((1, 128)), _fixed_spec((1, 128)),
                  _fixed_spec((1, DVA))],
        out_specs=[_seq_spec(ns, HA * DVA, 0),
                   pl.BlockSpec((ns, HA, DKA, DVA), lambda b, i: (b, 0, 0, 0))],
        out_shape=[jax.ShapeDtypeStruct((bsz // ns, ns, seq, HA * DVA), BF16),
                   jax.ShapeDtypeStruct((bsz, HA, DKA, DVA), F32)],
        scratch_shapes=[pltpu.VMEM((ns, HA, DKA, DVA), F32), pltpu.VMEM((ns, c, GDN_CONV_DIM), F32)],
        compiler_params=_cparams(("parallel", "arbitrary")),
        name="gdn_prompt",
    )(proj4, proj4, proj4, conv_w, gcoef, dtb, norm_g.reshape(1, DVA))
    return o.reshape(bsz * seq, HA * DVA), s


SB_TQ = 512
SB_TK = 256


def _sb_prompt_kernel(bias_ref, q_ref, k_ref, v_ref, o_ref):
    h = pl.program_id(1)
    qi = pl.program_id(2)
    tq = q_ref.shape[0]
    tk = min(SB_TK, tq)
    bias = bias_ref[h]
    q = q_ref[...].astype(BF16)
    ri = _iota2((tq, tk), 0)
    cj = _iota2((tq, tk), 1)
    tri = (_iota2((tk, tk), 0) > _iota2((tk, tk), 1)).astype(BF16)
    nkb = (qi + 1) * (tq // tk)

    def body(jj, carry):
        acc, run = carry
        j = nkb - 1 - jj
        off = pl.multiple_of(j * tk, tk)
        kb = k_ref[pl.ds(off, tk), :].astype(BF16)
        vb = v_ref[pl.ds(off, tk), :].astype(BF16)
        z = _dot_nt(q, kb) * (DB ** -0.5) + bias
        mask = (j * tk + cj) < (qi * tq + ri)
        sp = _softplus_log(z)
        lf = jnp.where(mask, -sp, 0.0)
        hi, lo = _split2(lf)
        after = (_dot(hi, tri) + _dot(lo, tri)) + run
        a = jnp.where(mask, jnp.exp((z - sp) + after), 0.0)
        acc = acc + _dot(a.astype(BF16), vb)
        run = run + jnp.sum(lf, axis=-1, keepdims=True)
        return acc, run

    acc, _ = lax.fori_loop(0, nkb, body,
                           (jnp.zeros((tq, DB), F32), jnp.zeros((tq, 1), F32)))
    o_ref[...] = acc.astype(BF16)


def sb_prompt(proj, bsz, seq, bias):
    tq = min(SB_TQ, seq)
    nq = seq // tq
    return pl.pallas_call(
        _sb_prompt_kernel,
        grid=(bsz, HB, nq),
        in_specs=[pl.BlockSpec(memory_space=pltpu.SMEM),
                  pl.BlockSpec((tq, DB), lambda b, h, i: (b * nq + i, OFF_QB // DB + h)),
                  pl.BlockSpec((seq, DB), lambda b, h, i: (b, OFF_KB // DB + h)),
                  pl.BlockSpec((seq, DB), lambda b, h, i: (b, OFF_VB // DB + h))],
        out_specs=pl.BlockSpec((tq, DB), lambda b, h, i: (b * nq + i, h)),
        out_shape=jax.ShapeDtypeStruct((bsz * seq, HB * DB), BF16),
        compiler_params=_cparams(("parallel", "parallel", "arbitrary")),
        name="sb_prompt",
    )(bias, proj, proj, proj)


GLA_SUB = 16


def _gla_prompt_kernel(q_ref, k_ref, v_ref, r_ref, sm_ref, w2_ref, b2_ref, ng_ref,
                       o_ref, s_out_ref, s_scr):
    _run_sequences(_gla_chunk, (q_ref, k_ref, v_ref, r_ref, sm_ref, o_ref), (w2_ref, b2_ref, ng_ref),
                   s_out_ref, (s_scr,))


def _gla_chunk(seqs, shared, scr):
    w2_ref, b2_ref, ng_ref = shared
    c = CHUNK
    ri = _iota2((c, c), 0)
    cj = _iota2((c, c), 1)
    incl = cj <= ri
    tril = incl.astype(BF16)
    jrow = _iota2((c, DKC), 0)
    q, k, v, r, bc, b_last_col, outs, states = [], [], [], [], [], [], [], []
    for (q_ref, k_ref, v_ref, r_ref, sm_ref, o_ref), (s_scr,) in zip(seqs, scr):
        sm = sm_ref[:, 0:128]
        pre = _dot3(sm, w2_ref[...]) + b2_ref[...]
        loga = -_softplus(-pre) * (1.0 / GLA_TAU)
        bcum = _tri_dot_left(tril, loga)
        bcum_t = bcum.T
        qa, ka, va, ra = q_ref[...], k_ref[...], v_ref[...], r_ref[...]
        for h in range(HC):
            q.append(qa[:, h * DKC:(h + 1) * DKC] * (DKC ** -0.5))
            k.append(ka[:, h * DKC:(h + 1) * DKC])
            v.append(va[:, h * DVC:(h + 1) * DVC])
            r.append(ra[:, h * DVC:(h + 1) * DVC])
            bc.append(bcum[:, h * DKC:(h + 1) * DKC])
            b_last_col.append(bcum_t[h * DKC:(h + 1) * DKC, c - 1:c])
            outs.append((o_ref, h))
            states.append((s_scr, h))
    n = range(len(q))
    rows = [[] for _ in n]
    for sb in range(c // GLA_SUB):
        i0 = sb * GLA_SUB
        ref = [bc[i][i0:i0 + 1, :] for i in n]
        qe = [(q[i][i0:i0 + GLA_SUB] * jnp.exp(bc[i][i0:i0 + GLA_SUB] - ref[i])).astype(BF16) for i in n]
        ke = [(k[i] * jnp.exp(jnp.where(jrow < i0 + GLA_SUB, ref[i] - bc[i], 0.0))).astype(BF16) for i in n]
        for i in n:
            rows[i].append(_dot_nt(qe[i], ke[i]))
    attn = [jnp.where(incl, jnp.concatenate(rows[i], axis=0), 0.0).astype(BF16) for i in n]
    s = [ref_[h] for ref_, h in states]
    vb = [v[i].astype(BF16) for i in n]
    o = [_dot((q[i] * jnp.exp(bc[i])).astype(BF16), s[i].astype(BF16)) + _dot(attn[i], vb[i]) for i in n]
    kd = [(k[i] * jnp.exp(bc[i][c - 1:c, :] - bc[i])).T.astype(BF16) for i in n]
    s_new = [jnp.exp(b_last_col[i]) * s[i] + _dot(kd[i], vb[i]) for i in n]
    for i in n:
        ref_, h = states[i]
        ref_[h] = s_new[i]
    for i in n:
        on = o[i] * lax.rsqrt(jnp.mean(o[i] * o[i], axis=-1, keepdims=True) + EPS) * ng_ref[...]
        ref_, h = outs[i]
        ref_[:, h * DVC:(h + 1) * DVC] = (on * _silu(r[i])).astype(BF16)


def gla_prompt(proj, bsz, seq, w2pad, b2, norm_g):
    nc = seq // CHUNK
    ns = _seqs_per_step(bsz)
    proj4 = proj.reshape(bsz // ns, ns, seq, N_PROJ)
    o, s = pl.pallas_call(
        _gla_prompt_kernel,
        grid=(bsz // ns, nc),
        in_specs=[_seq_spec(ns, HC * DKC, OFF_QC // (HC * DKC)),
                  _seq_spec(ns, HC * DKC, OFF_KC // (HC * DKC)),
                  _seq_spec(ns, HC * DVC, OFF_VC // (HC * DVC)),
                  _seq_spec(ns, HC * DVC, OFF_RC // (HC * DVC)),
                  _seq_spec(ns, 512, OFF_SMALL // 512),
                  _fixed_spec((128, HC * DKC)), _fixed_spec((1, HC * DKC)), _fixed_spec((1, DVC))],
        out_specs=[_seq_spec(ns, HC * DVC, 0),
                   pl.BlockSpec((ns, HC, DKC, DVC), lambda b, i: (b, 0, 0, 0))],
        out_shape=[jax.ShapeDtypeStruct((bsz // ns, ns, seq, HC * DVC), BF16),
                   jax.ShapeDtypeStruct((bsz, HC, DKC, DVC), F32)],
        scratch_shapes=[pltpu.VMEM((ns, HC, DKC, DVC), F32)],
        compiler_params=_cparams(("parallel", "arbitrary")),
        name="gla_prompt",
    )(proj4, proj4, proj4, proj4, proj4, w2pad, b2.reshape(1, HC * DKC), norm_g.reshape(1, DVC))
    return o.reshape(bsz * seq, HC * DVC), s


def _ssd_prompt_kernel(dvec_ref, xbc_ref, z_ref, sm_ref, cw_ref, cb_ref, arow_ref, dtb_ref, ng_ref,
                       o_ref, s_out_ref, s_scr, prev_scr):
    _run_sequences(_ssd_chunk, (xbc_ref, z_ref, sm_ref, o_ref),
                   (dvec_ref, cw_ref, cb_ref, arow_ref, dtb_ref, ng_ref), s_out_ref, (s_scr, prev_scr))


def _ssd_chunk(seqs, shared, scr):
    dvec_ref, cw_ref, cb_ref, arow_ref, dtb_ref, ng_ref = shared
    c = CHUNK
    ri = _iota2((c, c), 0)
    cj = _iota2((c, c), 1)
    incl = cj <= ri
    tril = incl.astype(BF16)
    hpg = HD // NG
    xh, zh, bgb, cg, cbg, li, lj, dti, dtj, l_last, dcoef, states = ([] for _ in range(12))
    for (xbc_ref, z_ref, sm_ref, o_ref), (s_scr, prev_scr) in zip(seqs, scr):
        xbc = _silu(_causal_conv_chunk(xbc_ref[...], prev_scr, cw_ref) + cb_ref[...])
        sm = sm_ref[:, 0:128]
        dt = _softplus(sm + dtb_ref[...])
        lcum = _tri_dot_left(tril, dt * arow_ref[...])
        lcum_t = lcum.T
        dt_t = dt.T
        z = z_ref[...]
        grp = []
        for g in range(NG):
            b_g = xbc[:, HD * PD + g * NSTATE:HD * PD + (g + 1) * NSTATE]
            c_g = xbc[:, HD * PD + NG * NSTATE + g * NSTATE:HD * PD + NG * NSTATE + (g + 1) * NSTATE]
            b_gb = b_g.astype(BF16)
            grp.append((b_gb, c_g, _dot_nt(c_g.astype(BF16), b_gb)))
        for h in range(HD):
            b_gb, c_g, cb_g = grp[h // hpg]
            lane = SM_DT + h
            xh.append(xbc[:, h * PD:(h + 1) * PD])
            zh.append(z[:, h * PD:(h + 1) * PD])
            bgb.append(b_gb)
            cg.append(c_g)
            cbg.append(cb_g)
            li.append(lcum[:, lane:lane + 1])
            lj.append(lcum_t[lane:lane + 1, :])
            dti.append(dt[:, lane:lane + 1])
            dtj.append(dt_t[lane:lane + 1, :])
            l_last.append(lcum[c - 1:c, lane:lane + 1])
            dcoef.append(dvec_ref[h])
            states.append((s_scr, h))
    n = range(len(xh))
    scores = [(cbg[i] * jnp.exp(jnp.where(incl, li[i] - lj[i], -jnp.inf)) * dtj[i]).astype(BF16) for i in n]
    s = [ref[h] for ref, h in states]
    xb = [xh[i].astype(BF16) for i in n]
    y = [_dot(scores[i], xb[i]) + _dot_nt((cg[i] * jnp.exp(li[i])).astype(BF16), s[i].astype(BF16))
         for i in n]
    xs = [(xh[i] * (dti[i] * jnp.exp(l_last[i] - li[i]))).T.astype(BF16) for i in n]
    s_new = [jnp.exp(l_last[i]) * s[i] + _dot(xs[i], bgb[i]) for i in n]
    for i in n:
        ref, h = states[i]
        ref[h] = s_new[i]
    ys = [(y[i] + dcoef[i] * xh[i]) * _silu(zh[i]) for i in n]
    gw = HD * PD // NG
    for sq, (_, _, _, o_ref) in enumerate(seqs):
        for g in range(NG):
            first = sq * HD + g * hpg
            yg = jnp.concatenate(ys[first:first + hpg], axis=-1)
            yn = yg * lax.rsqrt(jnp.mean(yg * yg, axis=-1, keepdims=True) + EPS)
            o_ref[:, g * gw:(g + 1) * gw] = (yn * ng_ref[:, g * gw:(g + 1) * gw]).astype(BF16)


def ssd_prompt(proj, bsz, seq, conv_w, conv_b, arow, dtb, dvec, norm_g):
    nc = seq // CHUNK
    c = CHUNK
    ns = _seqs_per_step(bsz)
    proj4 = proj.reshape(bsz // ns, ns, seq, N_PROJ)
    o, s = pl.pallas_call(
        _ssd_prompt_kernel,
        grid=(bsz // ns, nc),
        in_specs=[pl.BlockSpec(memory_space=pltpu.SMEM),
                  _seq_spec(ns, SSD_CONV_DIM, OFF_XBC // SSD_CONV_DIM),
                  _seq_spec(ns, 512, OFF_ZD // 512),
                  _seq_spec(ns, 512, OFF_SMALL // 512),
                  _fixed_spec((CONV_W, SSD_CONV_DIM)), _fixed_spec((1, SSD_CONV_DIM)),
                  _fixed_spec((1, 128)), _fixed_spec((1, 128)), _fixed_spec((1, HD * PD))],
        out_specs=[_seq_spec(ns, HD * PD, 0),
                   pl.BlockSpec((ns, HD, PD, NSTATE), lambda b, i: (b, 0, 0, 0))],
        out_shape=[jax.ShapeDtypeStruct((bsz // ns, ns, seq, HD * PD), BF16),
                   jax.ShapeDtypeStruct((bsz, HD, PD, NSTATE), F32)],
        scratch_shapes=[pltpu.VMEM((ns, HD, PD, NSTATE), F32), pltpu.VMEM((ns, c, SSD_CONV_DIM), F32)],
        compiler_params=_cparams(("parallel", "arbitrary")),
        name="ssd_prompt",
    )(dvec, proj4, proj4, proj4, conv_w, conv_b.reshape(1, SSD_CONV_DIM), arow, dtb,
      norm_g.reshape(1, HD * PD))
    return o.reshape(bsz * seq, HD * PD), s


def _row8(r):
    return jnp.concatenate([r, jnp.zeros((7, r.shape[1]), F32)], axis=0)


def _sample_mixers_kernel(gneg_ref, gdtb_ref, aneg_ref, sdtb_ref, dvec_ref,
                          p_ref, gs_ref, gbuf_ref, ls_ref, ss_ref, sbuf_ref,
                          gcw_ref, gng_ref, w2_ref, b2_ref, lng_ref, scw_ref, scb_ref, sng_ref,
                          oa_ref, oc_ref, od_ref, gs_out, gbuf_out, ls_out, ss_out, sbuf_out):
    l = 0
    sm = p_ref[0, :, OFF_SMALL:OFF_SMALL + 128]

    u = p_ref[0, :, OFF_QKVA:OFF_QKVA + GDN_CONV_DIM]
    buf = gbuf_ref[0, 0]
    conv = (buf[0:1] * gcw_ref[0:1, :] + buf[1:2] * gcw_ref[1:2, :]
            + buf[2:3] * gcw_ref[2:3, :] + u * gcw_ref[3:4, :])
    gbuf_out[0, 0:2, :] = buf[1:3]
    gbuf_out[0, 2:3, :] = u
    x = _silu(conv)
    za = p_ref[0, :, OFF_ZA:OFF_ZA + HA * DVA]
    for h in range(HA):
        q = x[:, h * DKA:(h + 1) * DKA]
        k = x[:, HA * DKA + h * DKA:HA * DKA + (h + 1) * DKA]
        v = x[:, 2 * HA * DKA + h * DVA:2 * HA * DKA + (h + 1) * DVA]
        q = q * lax.rsqrt(jnp.sum(q * q, axis=-1, keepdims=True) + EPS) * (DKA ** -0.5)
        k = k * lax.rsqrt(jnp.sum(k * k, axis=-1, keepdims=True) + EPS)
        g = gneg_ref[h] * _softplus(sm[:, SM_A + h:SM_A + h + 1] + gdtb_ref[h])
        b = jax.nn.sigmoid(sm[:, SM_B + h:SM_B + h + 1])
        eg = jnp.exp(g)
        s = gs_ref[0, 0, h]
        lhs = jnp.concatenate([k, q * eg, jnp.zeros((6, DKA), F32)], axis=0)
        kq = _dot(lhs, s, HIGHEST)
        ks, qs = kq[0:1], kq[1:2]
        uu = b * v - (b * eg) * ks
        o = qs + jnp.sum(q * k, axis=-1, keepdims=True) * uu
        gs_out[0, h] = eg * s + _col_from_row(k) * uu
        on = o * lax.rsqrt(jnp.mean(o * o, axis=-1, keepdims=True) + EPS) * gng_ref[...]
        oa_ref[0, :, h * DVA:(h + 1) * DVA] = (on * _silu(za[:, h * DVA:(h + 1) * DVA])).astype(BF16)

    pre = _dot(jnp.broadcast_to(sm, (8, 128)), w2_ref[...], HIGHEST)[0:1] + b2_ref[...]
    loga = -_softplus(-pre) * (1.0 / GLA_TAU)
    qc = p_ref[0, :, OFF_QC:OFF_QC + HC * DKC]
    kc = p_ref[0, :, OFF_KC:OFF_KC + HC * DKC]
    vc = p_ref[0, :, OFF_VC:OFF_VC + HC * DVC]
    rc = p_ref[0, :, OFF_RC:OFF_RC + HC * DVC]
    for h in range(HC):
        q = qc[:, h * DKC:(h + 1) * DKC] * (DKC ** -0.5)
        k = kc[:, h * DKC:(h + 1) * DKC]
        v = vc[:, h * DVC:(h + 1) * DVC]
        ea = jnp.exp(loga[:, h * DKC:(h + 1) * DKC])
        s = ls_ref[0, 0, h]
        o = _dot(_row8(q * ea), s, HIGHEST)[0:1] + jnp.sum(q * k, axis=-1, keepdims=True) * v
        ls_out[0, h] = _col_from_row(ea) * s + _col_from_row(k) * v
        on = o * lax.rsqrt(jnp.mean(o * o, axis=-1, keepdims=True) + EPS) * lng_ref[...]
        oc_ref[0, :, h * DVC:(h + 1) * DVC] = (on * _silu(rc[:, h * DVC:(h + 1) * DVC])).astype(BF16)

    us = p_ref[0, :, OFF_XBC:OFF_XBC + SSD_CONV_DIM]
    sbuf = sbuf_ref[0, 0]
    sconv = (sbuf[0:1] * scw_ref[0:1, :] + sbuf[1:2] * scw_ref[1:2, :]
             + sbuf[2:3] * scw_ref[2:3, :] + us * scw_ref[3:4, :])
    sbuf_out[0, 0:2, :] = sbuf[1:3]
    sbuf_out[0, 2:3, :] = us
    xbc = _silu(sconv + scb_ref[...])
    zd = p_ref[0, :, OFF_ZD:OFF_ZD + HD * PD]
    ys = []
    for h in range(HD):
        g = h // (HD // NG)
        bg = xbc[:, HD * PD + g * NSTATE:HD * PD + (g + 1) * NSTATE]
        cg = xbc[:, HD * PD + NG * NSTATE + g * NSTATE:HD * PD + NG * NSTATE + (g + 1) * NSTATE]
        xh = xbc[:, h * PD:(h + 1) * PD]
        dt = _softplus(sm[:, SM_DT + h:SM_DT + h + 1] + sdtb_ref[h])
        lc = dt * aneg_ref[h]
        el = jnp.exp(lc)
        s = ss_ref[0, 0, h]
        score = jnp.sum(cg * bg, axis=-1, keepdims=True) * dt
        y = score * xh + _dot_nt(_row8(cg * el), s, HIGHEST)[0:1]
        ss_out[0, h] = el * s + _col_from_row(xh * dt) * bg
        ys.append((y + dvec_ref[h] * xh) * _silu(zd[:, h * PD:(h + 1) * PD]))
    gw = HD * PD // NG
    for g in range(NG):
        yg = jnp.concatenate(ys[g * (HD // NG):(g + 1) * (HD // NG)], axis=-1)
        yn = yg * lax.rsqrt(jnp.mean(yg * yg, axis=-1, keepdims=True) + EPS)
        od_ref[0, :, g * gw:(g + 1) * gw] = (yn * sng_ref[:, g * gw:(g + 1) * gw]).astype(BF16)
    del l


def sample_mixers(proj3, layer, state_gdn, state_gdn_conv, state_gla, state_ssd, state_ssd_conv, prm):
    nb = proj3.shape[0]
    smem = pl.BlockSpec(memory_space=pltpu.SMEM)
    full = lambda shape: pl.BlockSpec(shape, lambda b: (0,) * len(shape))
    l = layer
    outs = pl.pallas_call(
        _sample_mixers_kernel,
        grid=(nb,),
        in_specs=[smem, smem, smem, smem, smem,
                  pl.BlockSpec((1, 1, N_PROJ), lambda b: (b, 0, 0)),
                  pl.BlockSpec((1, 1, HA, DKA, DVA), lambda b: (l, b, 0, 0, 0)),
                  pl.BlockSpec((1, 1, CONV_W - 1, GDN_CONV_DIM), lambda b: (l, b, 0, 0)),
                  pl.BlockSpec((1, 1, HC, DKC, DVC), lambda b: (l, b, 0, 0, 0)),
                  pl.BlockSpec((1, 1, HD, PD, NSTATE), lambda b: (l, b, 0, 0, 0)),
                  pl.BlockSpec((1, 1, CONV_W - 1, SSD_CONV_DIM), lambda b: (l, b, 0, 0)),
                  full((CONV_W, GDN_CONV_DIM)), full((1, DVA)),
                  full((128, HC * DKC)), full((1, HC * DKC)), full((1, DVC)),
                  full((CONV_W, SSD_CONV_DIM)), full((1, SSD_CONV_DIM)), full((1, HD * PD))],
        out_specs=[pl.BlockSpec((1, 1, HA * DVA), lambda b: (b, 0, 0)),
                   pl.BlockSpec((1, 1, HC * DVC), lambda b: (b, 0, 0)),
                   pl.BlockSpec((1, 1, HD * PD), lambda b: (b, 0, 0)),
                   pl.BlockSpec((1, HA, DKA, DVA), lambda b: (b, 0, 0, 0)),
                   pl.BlockSpec((1, CONV_W - 1, GDN_CONV_DIM), lambda b: (b, 0, 0)),
                   pl.BlockSpec((1, HC, DKC, DVC), lambda b: (b, 0, 0, 0)),
                   pl.BlockSpec((1, HD, PD, NSTATE), lambda b: (b, 0, 0, 0)),
                   pl.BlockSpec((1, CONV_W - 1, SSD_CONV_DIM), lambda b: (b, 0, 0))],
        out_shape=[jax.ShapeDtypeStruct((nb, 1, HA * DVA), BF16),
                   jax.ShapeDtypeStruct((nb, 1, HC * DVC), BF16),
                   jax.ShapeDtypeStruct((nb, 1, HD * PD), BF16),
                   jax.ShapeDtypeStruct((nb, HA, DKA, DVA), F32),
                   jax.ShapeDtypeStruct((nb, CONV_W - 1, GDN_CONV_DIM), F32),
                   jax.ShapeDtypeStruct((nb, HC, DKC, DVC), F32),
                   jax.ShapeDtypeStruct((nb, HD, PD, NSTATE), F32),
                   jax.ShapeDtypeStruct((nb, CONV_W - 1, SSD_CONV_DIM), F32)],
        compiler_params=_cparams(("parallel",)),
        name="sample_mixers",
    )(prm["gdn_negA"], prm["gdn_dt_bias"], prm["ssd_negA"], prm["ssd_dt_bias"], prm["ssd_D"],
      proj3, state_gdn, state_gdn_conv, state_gla, state_ssd, state_ssd_conv,
      prm["gdn_conv_w"], prm["gdn_norm_g"].reshape(1, DVA),
      prm["w2pad"], prm["gla_b2"].reshape(1, HC * DKC), prm["gla_norm_g"].reshape(1, DVC),
      prm["ssd_conv_w"], prm["ssd_conv_b"].reshape(1, SSD_CONV_DIM),
      prm["ssd_norm_g"].reshape(1, HD * PD))
    return outs


SA_PAGES = 8


def _sample_attn_kernel(pt_ref, q_ref, brow_ref, *refs):
    k_refs = refs[0:SA_PAGES]
    v_refs = refs[SA_PAGES:2 * SA_PAGES]
    o_ref, acc_scr, run_scr = refs[2 * SA_PAGES:]
    j = pl.program_id(1)
    nj = pl.num_programs(1)

    @pl.when(j == 0)
    def _():
        acc_scr[...] = jnp.zeros_like(acc_scr)
        run_scr[...] = jnp.zeros_like(run_scr)

    w = PAGE_SIZE * HB
    q = q_ref[0]
    q8 = jnp.concatenate([q[:, h * DB:(h + 1) * DB] for h in range(HB)]
                         + [jnp.zeros((8 - HB, DB), F32)], axis=0).astype(BF16)
    rows = _iota2((8, w), 0)
    lanes = _iota2((8, w), 1)
    sel = (lanes % HB) == rows
    zs = []
    for p in range(SA_PAGES):
        kp = k_refs[p][0, 0].astype(BF16)
        zz = _dot_nt(q8, kp)
        zs.append(jnp.sum(jnp.where(sel, zz, 0.0), axis=0, keepdims=True))
    z = jnp.concatenate(zs, axis=0) * (DB ** -0.5) + brow_ref[...]
    sp = _softplus_log(z)
    lf = -sp
    suf = jnp.where(lanes < w - HB, pltpu.roll(lf, w - HB, 1), 0.0)
    tot = lf
    step = HB
    while step < w:
        suf = suf + jnp.where(lanes < w - step, pltpu.roll(suf, w - step, 1), 0.0)
        tot = tot + pltpu.roll(tot, step, 1)
        step *= 2
    run = run_scr[0:1, :]
    runs = []
    for p in range(SA_PAGES):
        runs.append(run)
        run = run + tot[p:p + 1, :]
    run_scr[...] = jnp.broadcast_to(run, run_scr.shape)
    a = jnp.exp((z - sp) + (suf + jnp.concatenate(runs, axis=0)))
    acc = acc_scr[...]
    for p in range(SA_PAGES):
        vp = v_refs[p][0, 0].astype(BF16)
        ap = jnp.where(sel, jnp.broadcast_to(a[p:p + 1, :], (8, w)), 0.0).astype(BF16)
        acc = acc + _dot(ap, vp)
    acc_scr[...] = acc

    @pl.when(j == nj - 1)
    def _():
        o_ref[0] = acc[0:HB].astype(BF16)


def sample_attn(proj3, layer, cache_k4, cache_v4, page_table, bias):
    nb, npg = page_table.shape
    l = layer
    nj = npg // SA_PAGES

    def page_spec(p):
        return pl.BlockSpec((1, 1, PAGE_SIZE * HB, DB),
                            lambda b, j, pt: (l, pt[b, npg - 1 - (j * SA_PAGES + p)], 0, 0))

    grid_spec = pltpu.PrefetchScalarGridSpec(
        num_scalar_prefetch=1,
        grid=(nb, nj),
        in_specs=([pl.BlockSpec((1, 1, HB * DB), lambda b, j, pt: (b, 0, OFF_QB // (HB * DB))),
                   pl.BlockSpec((1, PAGE_SIZE * HB), lambda b, j, pt: (0, 0))]
                  + [page_spec(p) for p in range(SA_PAGES)]
                  + [page_spec(p) for p in range(SA_PAGES)]),
        out_specs=pl.BlockSpec((1, HB, DB), lambda b, j, pt: (b, 0, 0)),
        scratch_shapes=[pltpu.VMEM((8, DB), F32), pltpu.VMEM((8, PAGE_SIZE * HB), F32)],
    )
    brow = jnp.tile(bias.astype(F32), PAGE_SIZE).reshape(1, PAGE_SIZE * HB)
    return pl.pallas_call(
        _sample_attn_kernel,
        grid_spec=grid_spec,
        out_shape=jax.ShapeDtypeStruct((nb, HB, DB), BF16),
        compiler_params=_cparams(("parallel", "arbitrary")),
        name="sample_attn",
    )(page_table, proj3, brow, *([cache_k4] * SA_PAGES), *([cache_v4] * SA_PAGES))


def _out_proj_kernel(a_ref, b_ref, c_ref, d_ref, w_ref, x_ref, g1_ref, n2_ref, sc_ref, sh_ref,
                     xo_ref, h_ref):
    gw = GROUP_WIDTH
    acc = _dot(a_ref[...], w_ref[0:gw, :])
    acc = acc + _dot(b_ref[...], w_ref[gw:2 * gw, :])
    acc = acc + _dot(c_ref[...], w_ref[2 * gw:3 * gw, :])
    acc = acc + _dot(d_ref[...], w_ref[3 * gw:4 * gw, :])
    x = x_ref[...] + g1_ref[0] * acc
    xo_ref[...] = x
    y = x * lax.rsqrt(jnp.mean(x * x, axis=-1, keepdims=True) + EPS) * n2_ref[...]
    h_ref[...] = (y * (1.0 + sc_ref[0]) + sh_ref[0]).astype(BF16)


def out_proj(oa, ob, oc, od, w_out, x, g1, n2, sc2, sh2, rows_per_batch):
    t, d = x.shape
    tm = _row_tile(t, 512, g1, rows_per_batch)
    part = pl.BlockSpec((tm, GROUP_WIDTH), lambda i: (i, 0))
    return pl.pallas_call(
        _out_proj_kernel,
        grid=(t // tm,),
        in_specs=[part, part, part, part,
                  pl.BlockSpec((d, d), lambda i: (0, 0)),
                  pl.BlockSpec((tm, d), lambda i: (i, 0)),
                  _mod_spec(g1, tm, rows_per_batch),
                  pl.BlockSpec((1, d), lambda i: (0, 0)),
                  _mod_spec(sc2, tm, rows_per_batch),
                  _mod_spec(sh2, tm, rows_per_batch)],
        out_specs=[pl.BlockSpec((tm, d), lambda i: (i, 0)),
                   pl.BlockSpec((tm, d), lambda i: (i, 0))],
        out_shape=[jax.ShapeDtypeStruct((t, d), F32), jax.ShapeDtypeStruct((t, d), BF16)],
        compiler_params=_cparams(("parallel",), VMEM_LIMIT),
        name="out_proj",
    )(oa, ob, oc, od, w_out, x, g1, n2.reshape(1, d), sc2, sh2)


ROUTE_SUB = 128


def _topk_rows_many(vals, k, ids=None):
    if ids is None:
        ids = _iota2(vals[0].shape, 0).astype(F32)
    n = range(len(vals))
    out_v = [[] for _ in n]
    out_i = [[] for _ in n]
    for _ in range(k):
        m = [jnp.max(vals[i], axis=0, keepdims=True) for i in n]
        idx = [jnp.min(jnp.where(vals[i] == m[i], ids, 1e9), axis=0, keepdims=True) for i in n]
        vals = [jnp.where(ids == idx[i], -jnp.inf, vals[i]) for i in n]
        for i in n:
            out_v[i].append(m[i])
            out_i[i].append(idx[i])
    return [(jnp.concatenate(out_v[i], axis=0), jnp.concatenate(out_i[i], axis=0)) for i in n]


_CAND_ROWS = [(a, PEER_TOPK // (a + 1)) for a in range(PEER_TOPK)]
_N_CAND = sum(nb for _, nb in _CAND_ROWS)
_N_CAND_PAD = -(-_N_CAND // 8) * 8


def _cand_ids(n):
    r = _iota2((_N_CAND_PAD, n), 0)
    ids = jnp.full((_N_CAND_PAD, n), 1e9, F32)
    start = 0
    for a, nb in _CAND_ROWS:
        ids = jnp.where((r >= start) & (r < start + nb), (a * PEER_TOPK + r - start).astype(F32), ids)
        start += nb
    return ids


def _select_rows(table, sel):
    out = jnp.zeros(sel.shape, F32)
    for a in range(table.shape[0]):
        out = jnp.where(sel == float(a), table[a:a + 1, :], out)
    return out


def _peer_route_kernel(h_ref, wq_ref, sk_ref, e1_ref, e2_ref, gt_ref, q_scr, e1_scr, e2_scr, gt_scr):
    tm = h_ref.shape[0]
    q = _dot(h_ref[...], wq_ref[...])
    for cgrp in range(2 * PEER_HEADS):
        q_scr[cgrp] = q[:, cgrp * PK_DIM:(cgrp + 1) * PK_DIM]
    sk0 = sk_ref[0]
    sk1 = sk_ref[1]
    kk = PEER_TOPK

    n = min(tm, ROUTE_SUB)
    cand_ids = _cand_ids(n)
    subs = range(tm // n)

    def head_body(hd, _):
        s = []
        for sub in subs:
            r0 = sub * n
            s.append(_dot_nt(sk0, q_scr[2 * hd, r0:r0 + n, :]))
            s.append(_dot_nt(sk1, q_scr[2 * hd + 1, r0:r0 + n, :]))
        top = _topk_rows_many(s, kk)
        cand = []
        for sub in subs:
            sv0, sv1 = top[2 * sub][0], top[2 * sub + 1][0]
            cand.append(jnp.concatenate(
                [sv0[a:a + 1, :] + sv1[0:nb, :] for a, nb in _CAND_ROWS]
                + [jnp.full((_N_CAND_PAD - _N_CAND, n), -jnp.inf, F32)], axis=0))
        ctop = _topk_rows_many(cand, kk, cand_ids)
        ro = pl.multiple_of(hd * kk, kk)
        for sub in subs:
            r0 = sub * n
            cv, cidx = ctop[sub]
            ia = jnp.floor(cidx * (1.0 / kk))
            ib = cidx - ia * kk
            e1 = _select_rows(top[2 * sub][1], ia)
            e2 = _select_rows(top[2 * sub + 1][1], ib)
            ex = jnp.exp(cv - jnp.max(cv, axis=0, keepdims=True))
            gates = ex / jnp.sum(ex, axis=0, keepdims=True)
            e1_scr[pl.ds(ro, kk), r0:r0 + n] = e1
            e2_scr[pl.ds(ro, kk), r0:r0 + n] = e2
            gt_scr[pl.ds(ro, kk), r0:r0 + n] = gates
        return 0

    lax.fori_loop(0, PEER_HEADS, head_body, 0)

    e1_ref[...] = e1_scr[...].T
    e2_ref[...] = e2_scr[...].T
    gt_ref[...] = gt_scr[...].T


def peer_route(h2, wq, sub_keys):
    t, d = h2.shape
    tm = min(t, 256)
    nq = wq.shape[1]
    nj = PEER_HEADS * PEER_TOPK
    out = jax.ShapeDtypeStruct((t, nj), F32)
    ospec = pl.BlockSpec((tm, nj), lambda i: (i, 0))
    return pl.pallas_call(
        _peer_route_kernel,
        grid=(t // tm,),
        in_specs=[pl.BlockSpec((tm, d), lambda i: (i, 0)),
                  pl.BlockSpec((d, nq), lambda i: (0, 0)),
                  pl.BlockSpec((2, N_KEYS, PK_DIM), lambda i: (0, 0, 0))],
        out_specs=[ospec, ospec, ospec],
        out_shape=[out, out, out],
        scratch_shapes=[pltpu.VMEM((2 * PEER_HEADS, tm, PK_DIM), F32),
                        pltpu.VMEM((nj, tm), F32), pltpu.VMEM((nj, tm), F32),
                        pltpu.VMEM((nj, tm), F32)],
        compiler_params=_cparams(("parallel",), VMEM_LIMIT),
        name="peer_route",
    )(h2, wq, sub_keys)


GATE_GRP = 16


def _transpose8(vs):
    sub = _iota2(vs[0].shape, 0)
    vs = list(vs)
    for d in (4, 2, 1):
        keep = (sub & d) == 0
        out = list(vs)
        for i in range(8):
            if i & d:
                continue
            a, b = vs[i], vs[i + d]
            out[i] = jnp.where(keep, a, pltpu.roll(b, d, 0))
            out[i + d] = jnp.where(keep, pltpu.roll(a, 8 - d, 0), b)
        vs = out
    return vs


def _build_gates(e1_ref, e2_ref, gt_ref, g_scr, stage_scr):
    tm = e1_ref.shape[0]
    nj = PEER_HEADS * PEER_TOPK
    riota = _iota2((N_KEYS, nj), 0).astype(F32)

    half = GATE_GRP // 2

    def group(gi, _):
        t0 = pl.multiple_of(gi * GATE_GRP, GATE_GRP)
        for part in range(2):
            toks = [part * half + i for i in range(half)]
            e1 = [e1_ref[pl.ds(t0 + tt, 1), :] for tt in toks]
            e2 = [e2_ref[pl.ds(t0 + tt, 1), :] for tt in toks]
            g = [gt_ref[pl.ds(t0 + tt, 1), :] for tt in toks]
            pt = [jnp.where(riota == e1[i], g[i], 0.0).astype(BF16) for i in range(half)]
            qt = [jnp.where(riota == e2[i], 1.0, 0.0).astype(BF16) for i in range(half)]
            gm = [_dot_nt(pt[i], qt[i]) for i in range(half)]
            for i, tt in enumerate(toks):
                stage_scr[tt * N_KEYS:(tt + 1) * N_KEYS, :] = gm[i]

        def flush(eb, _):
            r0 = pl.multiple_of(eb * 8, 8)
            tiles = [stage_scr[pl.ds(tt * N_KEYS + r0, 8), :] for tt in range(GATE_GRP)]
            lo = _transpose8(tiles[0:8])
            hi = _transpose8(tiles[8:16])
            for r in range(8):
                rows = jnp.concatenate([lo[r], hi[r]], axis=0)
                g_scr[r0 + r, pl.ds(t0, GATE_GRP), :] = rows.astype(BF16)
            return 0

        lax.fori_loop(0, N_KEYS // 8, flush, 0)
        return 0

    lax.fori_loop(0, tm // GATE_GRP, group, 0)


PEER_TE = 1024
PEER_TE_SUB = 512


def _peer_dense_kernel(h_ref, u_ref, v_ref, e1_ref, e2_ref, gt_ref, x_ref, g2_ref, o_ref,
                       g_scr, stage_scr):
    e = pl.program_id(1)
    ne = pl.num_programs(1)

    @pl.when(e == 0)
    def _():
        o_ref[...] = jnp.zeros_like(o_ref)
        _build_gates(e1_ref, e2_ref, gt_ref, g_scr, stage_scr)

    h = h_ref[...]
    part = None
    per_sub = PEER_TE_SUB // N_KEYS
    for c in range(PEER_TE // PEER_TE_SUB):
        rows = slice(c * PEER_TE_SUB, (c + 1) * PEER_TE_SUB)
        a = _dot_nt(h, u_ref[0, rows, :])
        act = a * (lax.erf(a * (0.5 ** 0.5)) + 1.0) * 0.5
        g0 = e * (PEER_TE // N_KEYS) + c * per_sub
        g = jnp.concatenate([g_scr[g0 + k] for k in range(per_sub)], axis=1)
        hh = (g.astype(F32) * act).astype(BF16)
        p = _dot(hh, v_ref[0, rows, :])
        part = p if part is None else part + p
    o_ref[...] += part

    @pl.when(e == ne - 1)
    def _():
        o_ref[...] = x_ref[...] + g2_ref[0] * o_ref[...]


def peer_dense(h2, u_all, v_all, layer, e1, e2, gt, x, g2, rows_per_batch):
    t, d = x.shape
    tm = _row_tile(t, 512, g2, rows_per_batch)
    te = PEER_TE
    nj = PEER_HEADS * PEER_TOPK
    l = layer
    once = pl.Buffered(1)
    rspec = pl.BlockSpec((tm, nj), lambda i, e: (i, 0), pipeline_mode=once)
    return pl.pallas_call(
        _peer_dense_kernel,
        grid=(t // tm, N_EXPERTS // te),
        in_specs=[pl.BlockSpec((tm, d), lambda i, e: (i, 0), pipeline_mode=once),
                  pl.BlockSpec((1, te, d), lambda i, e: (l, e, 0)),
                  pl.BlockSpec((1, te, d), lambda i, e: (l, e, 0)),
                  rspec, rspec, rspec,
                  pl.BlockSpec((tm, d), lambda i, e: (i, 0), pipeline_mode=once),
                  _mod_spec(g2, tm, rows_per_batch)],
        out_specs=pl.BlockSpec((tm, d), lambda i, e: (i, 0)),
        out_shape=jax.ShapeDtypeStruct((t, d), F32),
        scratch_shapes=[pltpu.VMEM((N_KEYS, tm, N_KEYS), BF16),
                        pltpu.VMEM((GATE_GRP * N_KEYS, N_KEYS), F32)],
        compiler_params=_cparams(("parallel", "arbitrary"), VMEM_LIMIT),
        name="peer_dense",
    )(h2, u_all, v_all, e1, e2, gt, x, g2)


def _cast_kernel(x_ref, o_ref):
    o_ref[...] = x_ref[...].astype(BF16)


def cast_bf16(x):
    depth, n, d = x.shape
    tn = 1024
    return pl.pallas_call(
        _cast_kernel,
        grid=(depth, n // tn),
        in_specs=[pl.BlockSpec((1, tn, d), lambda l, i: (l, i, 0))],
        out_specs=pl.BlockSpec((1, tn, d), lambda l, i: (l, i, 0)),
        out_shape=jax.ShapeDtypeStruct(x.shape, BF16),
        compiler_params=_cparams(("parallel", "parallel"), VMEM_LIMIT),
        name="cast_bf16",
    )(x)


def _final_norm_kernel(x_ref, g_ref, o_ref):
    x = x_ref[...]
    o_ref[...] = x * lax.rsqrt(jnp.mean(x * x, axis=-1, keepdims=True) + EPS) * g_ref[...]


def final_norm(x, g):
    t, d = x.shape
    tm = min(t, 512)
    return pl.pallas_call(
        _final_norm_kernel,
        grid=(t // tm,),
        in_specs=[pl.BlockSpec((tm, d), lambda i: (i, 0)), pl.BlockSpec((1, d), lambda i: (0, 0))],
        out_specs=pl.BlockSpec((tm, d), lambda i: (i, 0)),
        out_shape=jax.ShapeDtypeStruct((t, d), F32),
        compiler_params=_cparams(("parallel",)),
        name="final_norm",
    )(x, g.reshape(1, d))


def _permute_w_in(w):
    d = w.shape[0]
    small = jnp.concatenate([w[:, 2048:2056], w[:, 5128:5144], w[:, 6680:6688]], axis=1)
    pad = jnp.zeros((d, N_PROJ - OFF_SMALL - small.shape[1]), w.dtype)
    return jnp.concatenate([w[:, 0:2048], w[:, 2056:5128], w[:, 5144:6680], small, pad],
                           axis=1).astype(BF16)


def _small_row(vals, off):
    return jnp.zeros((1, 128), F32).at[0, off:off + vals.shape[0]].set(vals.astype(F32))


def _layer_params(l, ada_w, ada_b, norm1_g, norm2_g, w_in, w_out, gdn_conv_w, gdn_A_log, gdn_dt_bias,
                  gdn_norm_g, sb_bias, gla_w2, gla_b2, gla_norm_g, ssd_conv_w, ssd_conv_b, ssd_A_log,
                  ssd_dt_bias, ssd_D, ssd_norm_g, peer_w_query, peer_sub_keys, peer_u, peer_v):
    del ada_w, ada_b
    w2pad = jnp.zeros((128, HC * DKC), F32).at[SM_G:SM_G + GLA_RANK, :].set(gla_w2[l])
    return {
        "norm1_g": norm1_g[l], "norm2_g": norm2_g[l],
        "w_in": _permute_w_in(w_in[l]), "w_out": w_out[l].astype(BF16),
        "gdn_conv_w": gdn_conv_w[l], "gdn_negA": -jnp.exp(gdn_A_log[l]), "gdn_dt_bias": gdn_dt_bias[l],
        "gdn_gcoef": _small_row(-jnp.exp(gdn_A_log[l]), SM_A),
        "gdn_dtb": _small_row(gdn_dt_bias[l], SM_A),
        "gdn_norm_g": gdn_norm_g[l], "sb_bias": sb_bias[l],
        "w2pad": w2pad, "gla_b2": gla_b2[l], "gla_norm_g": gla_norm_g[l],
        "ssd_conv_w": ssd_conv_w[l], "ssd_conv_b": ssd_conv_b[l],
        "ssd_negA": -jnp.exp(ssd_A_log[l]), "ssd_dt_bias": ssd_dt_bias[l],
        "ssd_arow": _small_row(-jnp.exp(ssd_A_log[l]), SM_DT),
        "ssd_dtb": _small_row(ssd_dt_bias[l], SM_DT),
        "ssd_D": ssd_D[l], "ssd_norm_g": ssd_norm_g[l],
        "wq": peer_w_query[l].astype(BF16), "sub_keys": peer_sub_keys[l],
        "peer_u": peer_u, "peer_v": peer_v, "layer": l,
    }


def _split_mod(mod):
    return [mod[:, i * D_MODEL:(i + 1) * D_MODEL] for i in range(6)]


def _peer_block(h2, x, g2, prm, rows_per_batch):
    e1, e2, gt = peer_route(h2, prm["wq"], prm["sub_keys"])
    return peer_dense(h2, prm["peer_u"], prm["peer_v"], prm["layer"], e1, e2, gt, x, g2, rows_per_batch)


def _peer_block_padded(h2, x, g2, prm):
    t = x.shape[0]
    tpad = -(-t // ROUTE_SUB) * ROUTE_SUB
    pad = lambda a: jnp.concatenate([a, jnp.zeros((tpad - t,) + a.shape[1:], a.dtype)], axis=0)
    g2p = pad(g2.reshape(t, -1)).reshape(1, tpad, -1)
    return _peer_block(pad(h2), pad(x), g2p, prm, 1)[:t]


def kernel(x_prompt, x_sample, cache_k, cache_v, state_gdn, state_gdn_conv, state_gla, state_ssd, state_ssd_conv, page_table, c_prompt, c_sample, ada_w, ada_b, norm1_g, norm2_g, w_in, w_out, gdn_conv_w, gdn_A_log, gdn_dt_bias, gdn_norm_g, sb_bias, gla_w2, gla_b2, gla_norm_g, ssd_conv_w, ssd_conv_b, ssd_A_log, ssd_dt_bias, ssd_D, ssd_norm_g, peer_w_query, peer_sub_keys, peer_u, peer_v, final_norm_g):
    bsz, seq, d = x_prompt.shape
    nb = x_sample.shape[0]
    tp = bsz * seq
    n_pool = cache_k.shape[1]
    cache_k4 = cache_k.reshape(DEPTH, n_pool, PAGE_SIZE * HB, DB)
    cache_v4 = cache_v.reshape(DEPTH, n_pool, PAGE_SIZE * HB, DB)

    n_c = bsz + nb
    r_pad = -(-n_c // 8) * 8
    c_all = jnp.concatenate([c_prompt, c_sample, jnp.zeros((r_pad - n_c, d), F32)], axis=0)
    mod = ada_mod(c_all, ada_w, ada_b)

    peer_u = cast_bf16(peer_u)
    peer_v = cast_bf16(peer_v)
    xp = x_prompt.reshape(tp, d)
    xs = x_sample.reshape(nb, d)
    outs_p, outs_s = [], []
    for l in range(DEPTH):
        prm = _layer_params(l, ada_w, ada_b, norm1_g, norm2_g, w_in, w_out, gdn_conv_w, gdn_A_log,
                            gdn_dt_bias, gdn_norm_g, sb_bias, gla_w2, gla_b2, gla_norm_g, ssd_conv_w,
                            ssd_conv_b, ssd_A_log, ssd_dt_bias, ssd_D, ssd_norm_g, peer_w_query,
                            peer_sub_keys, peer_u, peer_v)
        mp = [m.reshape(bsz, 1, d) for m in _split_mod(mod[l, 0:bsz])]
        ms = [m.reshape(1, nb, d) for m in _split_mod(mod[l, bsz:bsz + nb])]

        proj = in_proj(xp, prm["norm1_g"], mp[1], mp[0], prm["w_in"], seq)
        oa, gdn_s = gdn_prompt(proj, bsz, seq, prm["gdn_conv_w"], prm["gdn_gcoef"], prm["gdn_dtb"],
                               prm["gdn_norm_g"])
        ob = sb_prompt(proj, bsz, seq, prm["sb_bias"])
        oc, gla_s = gla_prompt(proj, bsz, seq, prm["w2pad"], prm["gla_b2"], prm["gla_norm_g"])
        od, ssd_s = ssd_prompt(proj, bsz, seq, prm["ssd_conv_w"], prm["ssd_conv_b"], prm["ssd_arow"],
                               prm["ssd_dtb"], prm["ssd_D"], prm["ssd_norm_g"])
        p3 = proj.reshape(bsz, seq, N_PROJ)
        outs_p.append((p3[:, :, OFF_KB:OFF_KB + HB * DB].reshape(bsz, seq, HB, DB),
                       p3[:, :, OFF_VB:OFF_VB + HB * DB].reshape(bsz, seq, HB, DB),
                       gdn_s, p3[:, seq - (CONV_W - 1):, OFF_QKVA:OFF_QKVA + GDN_CONV_DIM],
                       gla_s, ssd_s, p3[:, seq - (CONV_W - 1):, OFF_XBC:OFF_XBC + SSD_CONV_DIM]))
        xp, h2 = out_proj(oa, ob, oc, od, prm["w_out"], xp, mp[2], prm["norm2_g"], mp[4], mp[3], seq)
        xp = _peer_block(h2, xp, mp[5], prm, seq)

        proj_s = in_proj(xs, prm["norm1_g"], ms[1], ms[0], prm["w_in"], 1)
        ps3 = proj_s.reshape(nb, 1, N_PROJ)
        (oa_s, oc_s, od_s, gdn_n, gbuf_n, gla_n, ssd_n, sbuf_n) = sample_mixers(
            ps3, l, state_gdn, state_gdn_conv, state_gla, state_ssd, state_ssd_conv, prm)
        ob_s = sample_attn(ps3, l, cache_k4, cache_v4, page_table, prm["sb_bias"])
        outs_s.append((proj_s[:, OFF_KB:OFF_KB + HB * DB].reshape(nb, 1, HB, DB),
                       proj_s[:, OFF_VB:OFF_VB + HB * DB].reshape(nb, 1, HB, DB),
                       gdn_n, gbuf_n, gla_n, ssd_n, sbuf_n))
        xs, h2s = out_proj(oa_s.reshape(nb, -1), ob_s.reshape(nb, -1), oc_s.reshape(nb, -1),
                           od_s.reshape(nb, -1), prm["w_out"], xs, ms[2], prm["norm2_g"], ms[4], ms[3], 1)
        xs = _peer_block_padded(h2s, xs, ms[5], prm)

    y_prompt = final_norm(xp, final_norm_g).reshape(bsz, seq, d)
    y_sample = final_norm(xs, final_norm_g).reshape(nb, 1, d)
    stk = lambda lst, i: jnp.stack([s[i] for s in lst], axis=0)
    return (y_prompt, y_sample, stk(outs_p, 0), stk(outs_p, 1), stk(outs_s, 0), stk(outs_s, 1),
            stk(outs_p, 2), stk(outs_s, 2), stk(outs_p, 3), stk(outs_s, 3), stk(outs_p, 4), stk(outs_s, 4),
            stk(outs_p, 5), stk(outs_s, 5), stk(outs_p, 6), stk(outs_s, 6))
```

```python
import functools
import math

import jax
import jax.numpy as jnp
import numpy as np
from jax import lax
from jax.experimental import pallas as pl
from jax.experimental.pallas import tpu as pltpu

F32 = jnp.float32
BF16 = jnp.bfloat16
HIGHEST = lax.Precision.HIGHEST

D_MODEL = 2048
DEPTH = 2
PAGE_SIZE = 128
GROUP_WIDTH = D_MODEL // 4
HA, DKA, DVA = 4, 128, 128
HB, DB = 4, 128
HC, DKC, DVC = 4, 64, 128
GLA_RANK = 16
GLA_TAU = 16.0
HD, PD, NG, NSTATE = 8, 64, 2, 128
CONV_W = 4
CHUNK = 64
SB_BLOCK = 128
PEER_HEADS = 8
N_KEYS = 128
N_EXPERTS = N_KEYS * N_KEYS
PK_DIM = 128
PEER_TOPK = 16
EPS = 1e-6
GDN_CONV_DIM = 2 * HA * DKA + HA * DVA
SSD_CONV_DIM = HD * PD + 2 * NG * NSTATE

OFF_QKVA = 0
OFF_ZA = 1536
OFF_QB = 2048
OFF_KB = 2560
OFF_VB = 3072
OFF_QC = 3584
OFF_KC = 3840
OFF_VC = 4096
OFF_RC = 4608
OFF_XBC = 5120
OFF_ZD = 6144
OFF_SMALL = 6656
SM_A, SM_B, SM_G, SM_DT = 0, 4, 8, 24
N_PROJ = 7168

VMEM_LIMIT = 56 * 1024 * 1024


def _cparams(sem, vmem=None):
    return pltpu.CompilerParams(dimension_semantics=sem, vmem_limit_bytes=vmem)


def _softplus(x):
    return jnp.maximum(x, 0.0) + jnp.log1p(jnp.exp(-jnp.abs(x)))


def _softplus_log(x):
    return jnp.maximum(x, 0.0) + jnp.log(1.0 + jnp.exp(-jnp.abs(x)))


def _silu(x):
    return x * jax.nn.sigmoid(x)


def _dotb(a, b):
    return _dot(a.astype(BF16), b.astype(BF16))


def _dotb_nt(a, b):
    return _dot_nt(a.astype(BF16), b.astype(BF16))


def _split2(x):
    hi = x.astype(BF16)
    return hi, (x - hi.astype(F32)).astype(BF16)


def _dot3(a, b):
    ah, al = _split2(a)
    bh, bl = _split2(b)
    return _dot(ah, bh) + (_dot(ah, bl) + _dot(al, bh))


def _dot(a, b, precision=None):
    return jnp.dot(a, b, precision=precision, preferred_element_type=F32)


def _dot_nt(a, b, precision=None):
    return lax.dot_general(a, b, (((1,), (1,)), ((), ())), precision=precision,
                           preferred_element_type=F32)


def _split3(x):
    hi = x.astype(BF16)
    r1 = x - hi.astype(F32)
    mid = r1.astype(BF16)
    lo = (r1 - mid.astype(F32)).astype(BF16)
    return hi, mid, lo


def _tri_dot_left(tri_bf16, x):
    hi, mid, lo = _split3(x)
    return (_dot(tri_bf16, hi) + _dot(tri_bf16, mid)) + _dot(tri_bf16, lo)


def _tri_dot_right(x, tri_bf16):
    hi, mid, lo = _split3(x)
    return (_dot(hi, tri_bf16) + _dot(mid, tri_bf16)) + _dot(lo, tri_bf16)


def _iota2(shape, dim):
    return lax.broadcasted_iota(jnp.int32, shape, dim)


def _col_from_row(r):
    n = r.shape[1]
    eye = _iota2((n, n), 0) == _iota2((n, n), 1)
    return jnp.sum(jnp.where(eye, jnp.broadcast_to(r, (n, n)), 0.0), axis=1, keepdims=True)


def _ada_kernel(c_ref, w_ref, b_ref, o_ref):
    c = c_ref[...]
    o_ref[0] = _dot(_silu(c), w_ref[0]) + b_ref[0]


def ada_mod(c_all, ada_w, ada_b):
    r = c_all.shape[0]
    n = ada_w.shape[2]
    tn = 1024
    return pl.pallas_call(
        _ada_kernel,
        grid=(DEPTH, n // tn),
        in_specs=[pl.BlockSpec((r, D_MODEL), lambda l, j: (0, 0)),
                  pl.BlockSpec((1, D_MODEL, tn), lambda l, j: (l, 0, j)),
                  pl.BlockSpec((1, 1, tn), lambda l, j: (l, 0, j))],
        out_specs=pl.BlockSpec((1, r, tn), lambda l, j: (l, 0, j)),
        out_shape=jax.ShapeDtypeStruct((DEPTH, r, n), F32),
        compiler_params=_cparams(("parallel", "parallel"), VMEM_LIMIT),
        name="ada_mod",
    )(c_all, ada_w, ada_b.reshape(DEPTH, 1, n))


def _in_proj_kernel(x_ref, g_ref, sc_ref, sh_ref, w_ref, o_ref, h_scr):
    @pl.when(pl.program_id(1) == 0)
    def _():
        x = x_ref[...]
        y = x * lax.rsqrt(jnp.mean(x * x, axis=-1, keepdims=True) + EPS) * g_ref[...]
        h_scr[...] = (y * (1.0 + sc_ref[0]) + sh_ref[0]).astype(BF16)

    o_ref[...] = _dot(h_scr[...], w_ref[...])


def _row_tile(t, cap, mod, rows_per_batch):
    return min(t, cap, rows_per_batch) if mod.shape[1] == 1 else min(t, cap)


def _mod_spec(mod, tm, rows_per_batch):
    nb, r, d = mod.shape
    if r == 1:
        return pl.BlockSpec((1, 1, d), lambda i, *_: ((i * tm) // rows_per_batch, 0, 0))
    return pl.BlockSpec((1, r, d), lambda i, *_: (0, 0, 0))


def in_proj(x, g, sc, sh, w, rows_per_batch):
    t, d = x.shape
    n = w.shape[1]
    tm = _row_tile(t, 1024, sc, rows_per_batch)
    tn = 1024
    return pl.pallas_call(
        _in_proj_kernel,
        grid=(t // tm, n // tn),
        in_specs=[pl.BlockSpec((tm, d), lambda i, j: (i, 0)),
                  pl.BlockSpec((1, d), lambda i, j: (0, 0)),
                  _mod_spec(sc, tm, rows_per_batch),
                  _mod_spec(sh, tm, rows_per_batch),
                  pl.BlockSpec((d, tn), lambda i, j: (0, j))],
        out_specs=pl.BlockSpec((tm, tn), lambda i, j: (i, j)),
        out_shape=jax.ShapeDtypeStruct((t, n), F32),
        scratch_shapes=[pltpu.VMEM((tm, d), BF16)],
        compiler_params=_cparams(("parallel", "arbitrary"), VMEM_LIMIT),
        name="in_proj",
    )(x, g.reshape(1, d), sc, sh, w)


def _causal_conv_chunk(u, prev_ref, w_ref):
    c = u.shape[0]
    rows = _iota2(u.shape, 0)
    prev = prev_ref[...]
    out = u * w_ref[CONV_W - 1:CONV_W, :]
    for k in range(1, CONV_W):
        shifted = jnp.where(rows >= k, pltpu.roll(u, k, 0), pltpu.roll(prev, k, 0))
        out = out + shifted * w_ref[CONV_W - 1 - k:CONV_W - k, :]
    prev_ref[...] = u
    del c
    return out


def _inv_unit_lower_many(ms):
    c = ms[0].shape[0]
    n = range(len(ms))
    eye = (_iota2((c, c), 0) == _iota2((c, c), 1)).astype(F32)
    x = [-m for m in ms]
    p = [eye + x[i] for i in n]
    steps = int(math.ceil(math.log2(c))) - 1
    xs = [_split2(x[i]) for i in n]
    for _ in range(steps):
        x = [_dot(xs[i][0], xs[i][0]) + (_dot(xs[i][0], xs[i][1]) + _dot(xs[i][1], xs[i][0])) for i in n]
        xs = [_split2(x[i]) for i in n]
        ps = [_split2(p[i]) for i in n]
        p = [p[i] + (_dot(ps[i][0], xs[i][0]) + (_dot(ps[i][0], xs[i][1]) + _dot(ps[i][1], xs[i][0])))
             for i in n]
    return p


def _run_sequences(body, seq_refs, shared_refs, s_out_ref, scratch):
    ci = pl.program_id(1)
    nc = pl.num_programs(1)

    @pl.when(ci == 0)
    def _():
        for s in scratch:
            s[...] = jnp.zeros_like(s)

    nseq = s_out_ref.shape[0]
    body([tuple(r.at[0, sq] for r in seq_refs) for sq in range(nseq)], shared_refs,
         [tuple(s.at[sq] for s in scratch) for sq in range(nseq)])

    @pl.when(ci == nc - 1)
    def _():
        s_out_ref[...] = scratch[0][...]


def _gdn_prompt_kernel(qkv_ref, z_ref, sm_ref, cw_ref, gcoef_ref, dtb_ref, ng_ref,
                       o_ref, s_out_ref, s_scr, prev_scr):
    _run_sequences(_gdn_chunk, (qkv_ref, z_ref, sm_ref, o_ref), (cw_ref, gcoef_ref, dtb_ref, ng_ref),
                   s_out_ref, (s_scr, prev_scr))


def _gdn_chunk(seqs, shared, scr):
    cw_ref, gcoef_ref, dtb_ref, ng_ref = shared
    c = CHUNK
    ri = _iota2((c, c), 0)
    cj = _iota2((c, c), 1)
    incl = cj <= ri
    strict = cj < ri
    tril = incl.astype(BF16)
    q, k, v, gc, gr, bc, g_last, zs, outs, states = [], [], [], [], [], [], [], [], [], []
    for (qkv_ref, z_ref, sm_ref, o_ref), (s_scr, prev_scr) in zip(seqs, scr):
        x = _silu(_causal_conv_chunk(qkv_ref[...], prev_scr, cw_ref))
        sm = sm_ref[:, 0:128]
        g = gcoef_ref[...] * _softplus(sm + dtb_ref[...])
        beta = jax.nn.sigmoid(sm)
        gcum = _tri_dot_left(tril, g)
        gcum_t = gcum.T
        z = z_ref[...]
        for h in range(HA):
            q.append(x[:, h * DKA:(h + 1) * DKA])
            k.append(x[:, HA * DKA + h * DKA:HA * DKA + (h + 1) * DKA])
            v.append(x[:, 2 * HA * DKA + h * DVA:2 * HA * DKA + (h + 1) * DVA])
            gc.append(gcum[:, SM_A + h:SM_A + h + 1])
            gr.append(gcum_t[SM_A + h:SM_A + h + 1, :])
            bc.append(beta[:, SM_B + h:SM_B + h + 1])
            g_last.append(gcum[c - 1:c, SM_A + h:SM_A + h + 1])
            zs.append(z[:, h * DVA:(h + 1) * DVA])
            outs.append((o_ref, h))
            states.append((s_scr, h))
    n = range(len(q))
    q = [q[i] * lax.rsqrt(jnp.sum(q[i] * q[i], axis=-1, keepdims=True) + EPS) * (DKA ** -0.5) for i in n]
    k = [k[i] * lax.rsqrt(jnp.sum(k[i] * k[i], axis=-1, keepdims=True) + EPS) for i in n]
    decay = [jnp.exp(jnp.where(incl, gc[i] - gr[i], -jnp.inf)) for i in n]
    eg = [jnp.exp(gc[i]) for i in n]
    kb = [k[i].astype(BF16) for i in n]
    kk = [_dot_nt(kb[i], kb[i]) for i in n]
    m = [jnp.where(strict, decay[i], 0.0) * kk[i] * bc[i] for i in n]
    tinv = _inv_unit_lower_many(m)
    tb = [tinv[i].astype(BF16) for i in n]
    w = [_dot(tb[i], ((bc[i] * eg[i]) * k[i]).astype(BF16)) for i in n]
    u = [_dot(tb[i], (bc[i] * v[i]).astype(BF16)) for i in n]
    s = [ref[h] for ref, h in states]
    sb = [s[i].astype(BF16) for i in n]
    u = [u[i] - _dot(w[i].astype(BF16), sb[i]) for i in n]
    ub = [u[i].astype(BF16) for i in n]
    attn = [_dot_nt(q[i].astype(BF16), kb[i]) * decay[i] for i in n]
    o = [_dot((q[i] * eg[i]).astype(BF16), sb[i]) + _dot(attn[i].astype(BF16), ub[i]) for i in n]
    kd = [(k[i] * jnp.exp(g_last[i] - gc[i])).T.astype(BF16) for i in n]
    s_new = [jnp.exp(g_last[i]) * s[i] + _dot(kd[i], ub[i]) for i in n]
    for i in n:
        ref, h = states[i]
        ref[h] = s_new[i]
    for i in n:
        on = o[i] * lax.rsqrt(jnp.mean(o[i] * o[i], axis=-1, keepdims=True) + EPS) * ng_ref[...]
        ref, h = outs[i]
        ref[:, h * DVA:(h + 1) * DVA] = (on * _silu(zs[i])).astype(BF16)


def _seqs_per_step(bsz):
    return 4 if bsz % 4 == 0 else (2 if bsz % 2 == 0 else 1)


def _seq_spec(nseq, width, col_block):
    return pl.BlockSpec((1, nseq, CHUNK, width), lambda b, i: (b, 0, i, col_block))


def _fixed_spec(shape):
    return pl.BlockSpec(shape, lambda b, i: (0,) * len(shape))


def gdn_prompt(proj, bsz, seq, conv_w, gcoef, dtb, norm_g):
    nc = seq // CHUNK
    c = CHUNK
    ns = _seqs_per_step(bsz)
    proj4 = proj.reshape(bsz // ns, ns, seq, N_PROJ)
    o, s = pl.pallas_call(
        _gdn_prompt_kernel,
        grid=(bsz // ns, nc),
        in_specs=[_seq_spec(ns, GDN_CONV_DIM, OFF_QKVA // GDN_CONV_DIM),
                  _seq_spec(ns, 512, OFF_ZA // 512),
                  _seq_spec(ns, 512, OFF_SMALL // 512),
                  _fixed_spec((CONV_W, GDN_CONV_DIM)), _fixed_spec((1, 128)), _fixed_spec((1, 128)),
                  _fixed_spec((1, DVA))],
        out_specs=[_seq_spec(ns, HA * DVA, 0),
                   pl.BlockSpec((ns, HA, DKA, DVA), lambda b, i: (b, 0, 0, 0))],
        out_shape=[jax.ShapeDtypeStruct((bsz // ns, ns, seq, HA * DVA), BF16),
                   jax.ShapeDtypeStruct((bsz, HA, DKA, DVA), F32)],
        scratch_shapes=[pltpu.VMEM((ns, HA, DKA, DVA), F32), pltpu.VMEM((ns, c, GDN_CONV_DIM), F32)],
        compiler_params=_cparams(("parallel", "arbitrary")),
        name="gdn_prompt",
    )(proj4, proj4, proj4, conv_w, gcoef, dtb, norm_g.reshape(1, DVA))
    return o.reshape(bsz * seq, HA * DVA), s


SB_TQ = 512
SB_TK = 256


def _sb_prompt_kernel(bias_ref, q_ref, k_ref, v_ref, o_ref):
    h = pl.program_id(1)
    qi = pl.program_id(2)
    tq = q_ref.shape[0]
    tk = min(SB_TK, tq)
    bias = bias_ref[h]
    q = q_ref[...].astype(BF16)
    ri = _iota2((tq, tk), 0)
    cj = _iota2((tq, tk), 1)
    tri = (_iota2((tk, tk), 0) > _iota2((tk, tk), 1)).astype(BF16)
    nkb = (qi + 1) * (tq // tk)

    def body(jj, carry):
        acc, run = carry
        j = nkb - 1 - jj
        off = pl.multiple_of(j * tk, tk)
        kb = k_ref[pl.ds(off, tk), :].astype(BF16)
        vb = v_ref[pl.ds(off, tk), :].astype(BF16)
        z = _dot_nt(q, kb) * (DB ** -0.5) + bias
        mask = (j * tk + cj) < (qi * tq + ri)
        sp = _softplus_log(z)
        lf = jnp.where(mask, -sp, 0.0)
        hi, lo = _split2(lf)
        after = (_dot(hi, tri) + _dot(lo, tri)) + run
        a = jnp.where(mask, jnp.exp((z - sp) + after), 0.0)
        acc = acc + _dot(a.astype(BF16), vb)
        run = run + jnp.sum(lf, axis=-1, keepdims=True)
        return acc, run

    acc, _ = lax.fori_loop(0, nkb, body,
                           (jnp.zeros((tq, DB), F32), jnp.zeros((tq, 1), F32)))
    o_ref[...] = acc.astype(BF16)


def sb_prompt(proj, bsz, seq, bias):
    tq = min(SB_TQ, seq)
    nq = seq // tq
    return pl.pallas_call(
        _sb_prompt_kernel,
        grid=(bsz, HB, nq),
        in_specs=[pl.BlockSpec(memory_space=pltpu.SMEM),
                  pl.BlockSpec((tq, DB), lambda b, h, i: (b * nq + i, OFF_QB // DB + h)),
                  pl.BlockSpec((seq, DB), lambda b, h, i: (b, OFF_KB // DB + h)),
                  pl.BlockSpec((seq, DB), lambda b, h, i: (b, OFF_VB // DB + h))],
        out_specs=pl.BlockSpec((tq, DB), lambda b, h, i: (b * nq + i, h)),
        out_shape=jax.ShapeDtypeStruct((bsz * seq, HB * DB), BF16),
        compiler_params=_cparams(("parallel", "parallel", "arbitrary")),
        name="sb_prompt",
    )(bias, proj, proj, proj)


GLA_SUB = 16


def _gla_prompt_kernel(q_ref, k_ref, v_ref, r_ref, sm_ref, w2_ref, b2_ref, ng_ref,
                       o_ref, s_out_ref, s_scr):
    _run_sequences(_gla_chunk, (q_ref, k_ref, v_ref, r_ref, sm_ref, o_ref), (w2_ref, b2_ref, ng_ref),
                   s_out_ref, (s_scr,))


def _gla_chunk(seqs, shared, scr):
    w2_ref, b2_ref, ng_ref = shared
    c = CHUNK
    ri = _iota2((c, c), 0)
    cj = _iota2((c, c), 1)
    incl = cj <= ri
    tril = incl.astype(BF16)
    jrow = _iota2((c, DKC), 0)
    q, k, v, r, bc, b_last_col, outs, states = [], [], [], [], [], [], [], []
    for (q_ref, k_ref, v_ref, r_ref, sm_ref, o_ref), (s_scr,) in zip(seqs, scr):
        sm = sm_ref[:, 0:128]
        pre = _dot3(sm, w2_ref[...]) + b2_ref[...]
        loga = -_softplus(-pre) * (1.0 / GLA_TAU)
        bcum = _tri_dot_left(tril, loga)
        bcum_t = bcum.T
        qa, ka, va, ra = q_ref[...], k_ref[...], v_ref[...], r_ref[...]
        for h in range(HC):
            q.append(qa[:, h * DKC:(h + 1) * DKC] * (DKC ** -0.5))
            k.append(ka[:, h * DKC:(h + 1) * DKC])
            v.append(va[:, h * DVC:(h + 1) * DVC])
            r.append(ra[:, h * DVC:(h + 1) * DVC])
            bc.append(bcum[:, h * DKC:(h + 1) * DKC])
            b_last_col.append(bcum_t[h * DKC:(h + 1) * DKC, c - 1:c])
            outs.append((o_ref, h))
            states.append((s_scr, h))
    n = range(len(q))
    rows = [[] for _ in n]
    for sb in range(c // GLA_SUB):
        i0 = sb * GLA_SUB
        ref = [bc[i][i0:i0 + 1, :] for i in n]
        qe = [(q[i][i0:i0 + GLA_SUB] * jnp.exp(bc[i][i0:i0 + GLA_SUB] - ref[i])).astype(BF16) for i in n]
        ke = [(k[i] * jnp.exp(jnp.where(jrow < i0 + GLA_SUB, ref[i] - bc[i], 0.0))).astype(BF16) for i in n]
        for i in n:
            rows[i].append(_dot_nt(qe[i], ke[i]))
    attn = [jnp.where(incl, jnp.concatenate(rows[i], axis=0), 0.0).astype(BF16) for i in n]
    s = [ref_[h] for ref_, h in states]
    vb = [v[i].astype(BF16) for i in n]
    o = [_dot((q[i] * jnp.exp(bc[i])).astype(BF16), s[i].astype(BF16)) + _dot(attn[i], vb[i]) for i in n]
    kd = [(k[i] * jnp.exp(bc[i][c - 1:c, :] - bc[i])).T.astype(BF16) for i in n]
    s_new = [jnp.exp(b_last_col[i]) * s[i] + _dot(kd[i], vb[i]) for i in n]
    for i in n:
        ref_, h = states[i]
        ref_[h] = s_new[i]
    for i in n:
        on = o[i] * lax.rsqrt(jnp.mean(o[i] * o[i], axis=-1, keepdims=True) + EPS) * ng_ref[...]
        ref_, h = outs[i]
        ref_[:, h * DVC:(h + 1) * DVC] = (on * _silu(r[i])).astype(BF16)


def gla_prompt(proj, bsz, seq, w2pad, b2, norm_g):
    nc = seq // CHUNK
    ns = _seqs_per_step(bsz)
    proj4 = proj.reshape(bsz // ns, ns, seq, N_PROJ)
    o, s = pl.pallas_call(
        _gla_prompt_kernel,
        grid=(bsz // ns, nc),
        in_specs=[_seq_spec(ns, HC * DKC, OFF_QC // (HC * DKC)),
                  _seq_spec(ns, HC * DKC, OFF_KC // (HC * DKC)),
                  _seq_spec(ns, HC * DVC, OFF_VC // (HC * DVC)),
                  _seq_spec(ns, HC * DVC, OFF_RC // (HC * DVC)),
                  _seq_spec(ns, 512, OFF_SMALL // 512),
                  _fixed_spec((128, HC * DKC)), _fixed_spec((1, HC * DKC)), _fixed_spec((1, DVC))],
        out_specs=[_seq_spec(ns, HC * DVC, 0),
                   pl.BlockSpec((ns, HC, DKC, DVC), lambda b, i: (b, 0, 0, 0))],
        out_shape=[jax.ShapeDtypeStruct((bsz // ns, ns, seq, HC * DVC), BF16),
                   jax.ShapeDtypeStruct((bsz, HC, DKC, DVC), F32)],
        scratch_shapes=[pltpu.VMEM((ns, HC, DKC, DVC), F32)],
        compiler_params=_cparams(("parallel", "arbitrary")),
        name="gla_prompt",
    )(proj4, proj4, proj4, proj4, proj4, w2pad, b2.reshape(1, HC * DKC), norm_g.reshape(1, DVC))
    return o.reshape(bsz * seq, HC * DVC), s


def _ssd_prompt_kernel(dvec_ref, xbc_ref, z_ref, sm_ref, cw_ref, cb_ref, arow_ref, dtb_ref, ng_ref,
                       o_ref, s_out_ref, s_scr, prev_scr):
    _run_sequences(_ssd_chunk, (xbc_ref, z_ref, sm_ref, o_ref),
                   (dvec_ref, cw_ref, cb_ref, arow_ref, dtb_ref, ng_ref), s_out_ref, (s_scr, prev_scr))


def _ssd_chunk(seqs, shared, scr):
    dvec_ref, cw_ref, cb_ref, arow_ref, dtb_ref, ng_ref = shared
    c = CHUNK
    ri = _iota2((c, c), 0)
    cj = _iota2((c, c), 1)
    incl = cj <= ri
    tril = incl.astype(BF16)
    hpg = HD // NG
    xh, zh, bgb, cg, cbg, li, lj, dti, dtj, l_last, dcoef, states = ([] for _ in range(12))
    for (xbc_ref, z_ref, sm_ref, o_ref), (s_scr, prev_scr) in zip(seqs, scr):
        xbc = _silu(_causal_conv_chunk(xbc_ref[...], prev_scr, cw_ref) + cb_ref[...])
        sm = sm_ref[:, 0:128]
        dt = _softplus(sm + dtb_ref[...])
        lcum = _tri_dot_left(tril, dt * arow_ref[...])
        lcum_t = lcum.T
        dt_t = dt.T
        z = z_ref[...]
        grp = []
        for g in range(NG):
            b_g = xbc[:, HD * PD + g * NSTATE:HD * PD + (g + 1) * NSTATE]
            c_g = xbc[:, HD * PD + NG * NSTATE + g * NSTATE:HD * PD + NG * NSTATE + (g + 1) * NSTATE]
            b_gb = b_g.astype(BF16)
            grp.append((b_gb, c_g, _dot_nt(c_g.astype(BF16), b_gb)))
        for h in range(HD):
            b_gb, c_g, cb_g = grp[h // hpg]
            lane = SM_DT + h
            xh.append(xbc[:, h * PD:(h + 1) * PD])
            zh.append(z[:, h * PD:(h + 1) * PD])
            bgb.append(b_gb)
            cg.append(c_g)
            cbg.append(cb_g)
            li.append(lcum[:, lane:lane + 1])
            lj.append(lcum_t[lane:lane + 1, :])
            dti.append(dt[:, lane:lane + 1])
            dtj.append(dt_t[lane:lane + 1, :])
            l_last.append(lcum[c - 1:c, lane:lane + 1])
            dcoef.append(dvec_ref[h])
            states.append((s_scr, h))
    n = range(len(xh))
    scores = [(cbg[i] * jnp.exp(jnp.where(incl, li[i] - lj[i], -jnp.inf)) * dtj[i]).astype(BF16) for i in n]
    s = [ref[h] for ref, h in states]
    xb = [xh[i].astype(BF16) for i in n]
    y = [_dot(scores[i], xb[i]) + _dot_nt((cg[i] * jnp.exp(li[i])).astype(BF16), s[i].astype(BF16))
         for i in n]
    xs = [(xh[i] * (dti[i] * jnp.exp(l_last[i] - li[i]))).T.astype(BF16) for i in n]
    s_new = [jnp.exp(l_last[i]) * s[i] + _dot(xs[i], bgb[i]) for i in n]
    for i in n:
        ref, h = states[i]
        ref[h] = s_new[i]
    ys = [(y[i] + dcoef[i] * xh[i]) * _silu(zh[i]) for i in n]
    gw = HD * PD // NG
    for sq, (_, _, _, o_ref) in enumerate(seqs):
        for g in range(NG):
            first = sq * HD + g * hpg
            yg = jnp.concatenate(ys[first:first + hpg], axis=-1)
            yn = yg * lax.rsqrt(jnp.mean(yg * yg, axis=-1, keepdims=True) + EPS)
            o_ref[:, g * gw:(g + 1) * gw] = (yn * ng_ref[:, g * gw:(g + 1) * gw]).astype(BF16)


def ssd_prompt(proj, bsz, seq, conv_w, conv_b, arow, dtb, dvec, norm_g):
    nc = seq // CHUNK
    c = CHUNK
    ns = _seqs_per_step(bsz)
    proj4 = proj.reshape(bsz // ns, ns, seq, N_PROJ)
    o, s = pl.pallas_call(
        _ssd_prompt_kernel,
        grid=(bsz // ns, nc),
        in_specs=[pl.BlockSpec(memory_space=pltpu.SMEM),
                  _seq_spec(ns, SSD_CONV_DIM, OFF_XBC // SSD_CONV_DIM),
                  _seq_spec(ns, 512, OFF_ZD // 512),
                  _seq_spec(ns, 512, OFF_SMALL // 512),
                  _fixed_spec((CONV_W, SSD_CONV_DIM)), _fixed_spec((1, SSD_CONV_DIM)),
                  _fixed_spec((1, 128)), _fixed_spec((1, 128)), _fixed_spec((1, HD * PD))],
        out_specs=[_seq_spec(ns, HD * PD, 0),
                   pl.BlockSpec((ns, HD, PD, NSTATE), lambda b, i: (b, 0, 0, 0))],
        out_shape=[jax.ShapeDtypeStruct((bsz // ns, ns, seq, HD * PD), BF16),
                   jax.ShapeDtypeStruct((bsz, HD, PD, NSTATE), F32)],
        scratch_shapes=[pltpu.VMEM((ns, HD, PD, NSTATE), F32), pltpu.VMEM((ns, c, SSD_CONV_DIM), F32)],
        compiler_params=_cparams(("parallel", "arbitrary")),
        name="ssd_prompt",
    )(dvec, proj4, proj4, proj4, conv_w, conv_b.reshape(1, SSD_CONV_DIM), arow, dtb,
      norm_g.reshape(1, HD * PD))
    return o.reshape(bsz * seq, HD * PD), s


def _row8(r):
    return jnp.concatenate([r, jnp.zeros((7, r.shape[1]), F32)], axis=0)


def _sample_mixers_kernel(gneg_ref, gdtb_ref, aneg_ref, sdtb_ref, dvec_ref,
                          p_ref, gs_ref, gbuf_ref, ls_ref, ss_ref, sbuf_ref,
                          gcw_ref, gng_ref, w2_ref, b2_ref, lng_ref, scw_ref, scb_ref, sng_ref,
                          oa_ref, oc_ref, od_ref, gs_out, gbuf_out, ls_out, ss_out, sbuf_out):
    l = 0
    sm = p_ref[0, :, OFF_SMALL:OFF_SMALL + 128]

    u = p_ref[0, :, OFF_QKVA:OFF_QKVA + GDN_CONV_DIM]
    buf = gbuf_ref[0, 0]
    conv = (buf[0:1] * gcw_ref[0:1, :] + buf[1:2] * gcw_ref[1:2, :]
            + buf[2:3] * gcw_ref[2:3, :] + u * gcw_ref[3:4, :])
    gbuf_out[0, 0:2, :] = buf[1:3]
    gbuf_out[0, 2:3, :] = u
    x = _silu(conv)
    za = p_ref[0, :, OFF_ZA:OFF_ZA + HA * DVA]
    for h in range(HA):
        q = x[:, h * DKA:(h + 1) * DKA]
        k = x[:, HA * DKA + h * DKA:HA * DKA + (h + 1) * DKA]
        v = x[:, 2 * HA * DKA + h * DVA:2 * HA * DKA + (h + 1) * DVA]
        q = q * lax.rsqrt(jnp.sum(q * q, axis=-1, keepdims=True) + EPS) * (DKA ** -0.5)
        k = k * lax.rsqrt(jnp.sum(k * k, axis=-1, keepdims=True) + EPS)
        g = gneg_ref[h] * _softplus(sm[:, SM_A + h:SM_A + h + 1] + gdtb_ref[h])
        b = jax.nn.sigmoid(sm[:, SM_B + h:SM_B + h + 1])
        eg = jnp.exp(g)
        s = gs_ref[0, 0, h]
        lhs = jnp.concatenate([k, q * eg, jnp.zeros((6, DKA), F32)], axis=0)
        kq = _dot(lhs, s, HIGHEST)
        ks, qs = kq[0:1], kq[1:2]
        uu = b * v - (b * eg) * ks
        o = qs + jnp.sum(q * k, axis=-1, keepdims=True) * uu
        gs_out[0, h] = eg * s + _col_from_row(k) * uu
        on = o * lax.rsqrt(jnp.mean(o * o, axis=-1, keepdims=True) + EPS) * gng_ref[...]
        oa_ref[0, :, h * DVA:(h + 1) * DVA] = (on * _silu(za[:, h * DVA:(h + 1) * DVA])).astype(BF16)

    pre = _dot(jnp.broadcast_to(sm, (8, 128)), w2_ref[...], HIGHEST)[0:1] + b2_ref[...]
    loga = -_softplus(-pre) * (1.0 / GLA_TAU)
    qc = p_ref[0, :, OFF_QC:OFF_QC + HC * DKC]
    kc = p_ref[0, :, OFF_KC:OFF_KC + HC * DKC]
    vc = p_ref[0, :, OFF_VC:OFF_VC + HC * DVC]
    rc = p_ref[0, :, OFF_RC:OFF_RC + HC * DVC]
    for h in range(HC):
        q = qc[:, h * DKC:(h + 1) * DKC] * (DKC ** -0.5)
        k = kc[:, h * DKC:(h + 1) * DKC]
        v = vc[:, h * DVC:(h + 1) * DVC]
        ea = jnp.exp(loga[:, h * DKC:(h + 1) * DKC])
        s = ls_ref[0, 0, h]
        o = _dot(_row8(q * ea), s, HIGHEST)[0:1] + jnp.sum(q * k, axis=-1, keepdims=True) * v
        ls_out[0, h] = _col_from_row(ea) * s + _col_from_row(k) * v
        on = o * lax.rsqrt(jnp.mean(o * o, axis=-1, keepdims=True) + EPS) * lng_ref[...]
        oc_ref[0, :, h * DVC:(h + 1) * DVC] = (on * _silu(rc[:, h * DVC:(h + 1) * DVC])).astype(BF16)

    us = p_ref[0, :, OFF_XBC:OFF_XBC + SSD_CONV_DIM]
    sbuf = sbuf_ref[0, 0]
    sconv = (sbuf[0:1] * scw_ref[0:1, :] + sbuf[1:2] * scw_ref[1:2, :]
             + sbuf[2:3] * scw_ref[2:3, :] + us * scw_ref[3:4, :])
    sbuf_out[0, 0:2, :] = sbuf[1:3]
    sbuf_out[0, 2:3, :] = us
    xbc = _silu(sconv + scb_ref[...])
    zd = p_ref[0, :, OFF_ZD:OFF_ZD + HD * PD]
    ys = []
    for h in range(HD):
        g = h // (HD // NG)
        bg = xbc[:, HD * PD + g * NSTATE:HD * PD + (g + 1) * NSTATE]
        cg = xbc[:, HD * PD + NG * NSTATE + g * NSTATE:HD * PD + NG * NSTATE + (g + 1) * NSTATE]
        xh = xbc[:, h * PD:(h + 1) * PD]
        dt = _softplus(sm[:, SM_DT + h:SM_DT + h + 1] + sdtb_ref[h])
        lc = dt * aneg_ref[h]
        el = jnp.exp(lc)
        s = ss_ref[0, 0, h]
        score = jnp.sum(cg * bg, axis=-1, keepdims=True) * dt
        y = score * xh + _dot_nt(_row8(cg * el), s, HIGHEST)[0:1]
        ss_out[0, h] = el * s + _col_from_row(xh * dt) * bg
        ys.append((y + dvec_ref[h] * xh) * _silu(zd[:, h * PD:(h + 1) * PD]))
    gw = HD * PD // NG
    for g in range(NG):
        yg = jnp.concatenate(ys[g * (HD // NG):(g + 1) * (HD // NG)], axis=-1)
        yn = yg * lax.rsqrt(jnp.mean(yg * yg, axis=-1, keepdims=True) + EPS)
        od_ref[0, :, g * gw:(g + 1) * gw] = (yn * sng_ref[:, g * gw:(g + 1) * gw]).astype(BF16)
    del l


def sample_mixers(proj3, layer, state_gdn, state_gdn_conv, state_gla, state_ssd, state_ssd_conv, prm):
    nb = proj3.shape[0]
    smem = pl.BlockSpec(memory_space=pltpu.SMEM)
    full = lambda shape: pl.BlockSpec(shape, lambda b: (0,) * len(shape))
    l = layer
    outs = pl.pallas_call(
        _sample_mixers_kernel,
        grid=(nb,),
        in_specs=[smem, smem, smem, smem, smem,
                  pl.BlockSpec((1, 1, N_PROJ), lambda b: (b, 0, 0)),
                  pl.BlockSpec((1, 1, HA, DKA, DVA), lambda b: (l, b, 0, 0, 0)),
                  pl.BlockSpec((1, 1, CONV_W - 1, GDN_CONV_DIM), lambda b: (l, b, 0, 0)),
                  pl.BlockSpec((1, 1, HC, DKC, DVC), lambda b: (l, b, 0, 0, 0)),
                  pl.BlockSpec((1, 1, HD, PD, NSTATE), lambda b: (l, b, 0, 0, 0)),
                  pl.BlockSpec((1, 1, CONV_W - 1, SSD_CONV_DIM), lambda b: (l, b, 0, 0)),
                  full((CONV_W, GDN_CONV_DIM)), full((1, DVA)),
                  full((128, HC * DKC)), full((1, HC * DKC)), full((1, DVC)),
                  full((CONV_W, SSD_CONV_DIM)), full((1, SSD_CONV_DIM)), full((1, HD * PD))],
        out_specs=[pl.BlockSpec((1, 1, HA * DVA), lambda b: (b, 0, 0)),
                   pl.BlockSpec((1, 1, HC * DVC), lambda b: (b, 0, 0)),
                   pl.BlockSpec((1, 1, HD * PD), lambda b: (b, 0, 0)),
                   pl.BlockSpec((1, HA, DKA, DVA), lambda b: (b, 0, 0, 0)),
                   pl.BlockSpec((1, CONV_W - 1, GDN_CONV_DIM), lambda b: (b, 0, 0)),
                   pl.BlockSpec((1, HC, DKC, DVC), lambda b: (b, 0, 0, 0)),
                   pl.BlockSpec((1, HD, PD, NSTATE), lambda b: (b, 0, 0, 0)),
                   pl.BlockSpec((1, CONV_W - 1, SSD_CONV_DIM), lambda b: (b, 0, 0))],
        out_shape=[jax.ShapeDtypeStruct((nb, 1, HA * DVA), BF16),
                   jax.ShapeDtypeStruct((nb, 1, HC * DVC), BF16),
                   jax.ShapeDtypeStruct((nb, 1, HD * PD), BF16),
                   jax.ShapeDtypeStruct((nb, HA, DKA, DVA), F32),
                   jax.ShapeDtypeStruct((nb, CONV_W - 1, GDN_CONV_DIM), F32),
                   jax.ShapeDtypeStruct((nb, HC, DKC, DVC), F32),
                   jax.ShapeDtypeStruct((nb, HD, PD, NSTATE), F32),
                   jax.ShapeDtypeStruct((nb, CONV_W - 1, SSD_CONV_DIM), F32)],
        compiler_params=_cparams(("parallel",)),
        name="sample_mixers",
    )(prm["gdn_negA"], prm["gdn_dt_bias"], prm["ssd_negA"], prm["ssd_dt_bias"], prm["ssd_D"],
      proj3, state_gdn, state_gdn_conv, state_gla, state_ssd, state_ssd_conv,
      prm["gdn_conv_w"], prm["gdn_norm_g"].reshape(1, DVA),
      prm["w2pad"], prm["gla_b2"].reshape(1, HC * DKC), prm["gla_norm_g"].reshape(1, DVC),
      prm["ssd_conv_w"], prm["ssd_conv_b"].reshape(1, SSD_CONV_DIM),
      prm["ssd_norm_g"].reshape(1, HD * PD))
    return outs


SA_PAGES = 16


def _sample_attn_kernel(pt_ref, q_ref, brow_ref, *refs):
    k_refs = refs[0:SA_PAGES]
    v_refs = refs[SA_PAGES:2 * SA_PAGES]
    o_ref, acc_scr, run_scr = refs[2 * SA_PAGES:]
    j = pl.program_id(1)
    nj = pl.num_programs(1)

    @pl.when(j == 0)
    def _():
        acc_scr[...] = jnp.zeros_like(acc_scr)
        run_scr[...] = jnp.zeros_like(run_scr)

    w = PAGE_SIZE * HB
    q = q_ref[0]
    q8 = jnp.concatenate([q[:, h * DB:(h + 1) * DB] for h in range(HB)]
                         + [jnp.zeros((8 - HB, DB), F32)], axis=0).astype(BF16)
    rows = _iota2((8, w), 0)
    lanes = _iota2((8, w), 1)
    sel = (lanes % HB) == rows
    zs = []
    for p in range(SA_PAGES):
        kp = k_refs[p][0, 0].astype(BF16)
        zz = _dot_nt(q8, kp)
        zs.append(jnp.sum(jnp.where(sel, zz, 0.0), axis=0, keepdims=True))
    z = jnp.concatenate(zs, axis=0) * (DB ** -0.5) + brow_ref[...]
    sp = _softplus_log(z)
    lf = -sp
    plane = _iota2((SA_PAGES, w), 1)
    suf = jnp.where(plane < w - HB, pltpu.roll(lf, w - HB, 1), 0.0)
    tot = lf
    step = HB
    while step < w:
        suf = suf + jnp.where(plane < w - step, pltpu.roll(suf, w - step, 1), 0.0)
        tot = tot + pltpu.roll(tot, step, 1)
        step *= 2
    run = run_scr[0:1, :]
    runs = []
    for p in range(SA_PAGES):
        runs.append(run)
        run = run + tot[p:p + 1, :]
    run_scr[...] = jnp.broadcast_to(run, run_scr.shape)
    a = jnp.exp((z - sp) + (suf + jnp.concatenate(runs, axis=0)))
    acc = acc_scr[...]
    for p in range(SA_PAGES):
        vp = v_refs[p][0, 0].astype(BF16)
        ap = jnp.where(sel, jnp.broadcast_to(a[p:p + 1, :], (8, w)), 0.0).astype(BF16)
        acc = acc + _dot(ap, vp)
    acc_scr[...] = acc

    @pl.when(j == nj - 1)
    def _():
        o_ref[0] = acc[0:HB].astype(BF16)


def sample_attn(proj3, layer, cache_k4, cache_v4, page_table, bias):
    nb, npg = page_table.shape
    l = layer
    nj = npg // SA_PAGES

    def page_spec(p):
        return pl.BlockSpec((1, 1, PAGE_SIZE * HB, DB),
                            lambda b, j, pt: (l, pt[b, npg - 1 - (j * SA_PAGES + p)], 0, 0))

    grid_spec = pltpu.PrefetchScalarGridSpec(
        num_scalar_prefetch=1,
        grid=(nb, nj),
        in_specs=([pl.BlockSpec((1, 1, HB * DB), lambda b, j, pt: (b, 0, OFF_QB // (HB * DB))),
                   pl.BlockSpec((1, PAGE_SIZE * HB), lambda b, j, pt: (0, 0))]
                  + [page_spec(p) for p in range(SA_PAGES)]
                  + [page_spec(p) for p in range(SA_PAGES)]),
        out_specs=pl.BlockSpec((1, HB, DB), lambda b, j, pt: (b, 0, 0)),
        scratch_shapes=[pltpu.VMEM((8, DB), F32), pltpu.VMEM((8, PAGE_SIZE * HB), F32)],
    )
    brow = jnp.tile(bias.astype(F32), PAGE_SIZE).reshape(1, PAGE_SIZE * HB)
    return pl.pallas_call(
        _sample_attn_kernel,
        grid_spec=grid_spec,
        out_shape=jax.ShapeDtypeStruct((nb, HB, DB), BF16),
        compiler_params=_cparams(("parallel", "arbitrary")),
        name="sample_attn",
    )(page_table, proj3, brow, *([cache_k4] * SA_PAGES), *([cache_v4] * SA_PAGES))


def _out_proj_kernel(a_ref, b_ref, c_ref, d_ref, w_ref, x_ref, g1_ref, n2_ref, sc_ref, sh_ref,
                     xo_ref, h_ref):
    gw = GROUP_WIDTH
    acc = _dot(a_ref[...], w_ref[0:gw, :])
    acc = acc + _dot(b_ref[...], w_ref[gw:2 * gw, :])
    acc = acc + _dot(c_ref[...], w_ref[2 * gw:3 * gw, :])
    acc = acc + _dot(d_ref[...], w_ref[3 * gw:4 * gw, :])
    x = x_ref[...] + g1_ref[0] * acc
    xo_ref[...] = x
    y = x * lax.rsqrt(jnp.mean(x * x, axis=-1, keepdims=True) + EPS) * n2_ref[...]
    h_ref[...] = (y * (1.0 + sc_ref[0]) + sh_ref[0]).astype(BF16)


def out_proj(oa, ob, oc, od, w_out, x, g1, n2, sc2, sh2, rows_per_batch):
    t, d = x.shape
    tm = _row_tile(t, 512, g1, rows_per_batch)
    part = pl.BlockSpec((tm, GROUP_WIDTH), lambda i: (i, 0))
    return pl.pallas_call(
        _out_proj_kernel,
        grid=(t // tm,),
        in_specs=[part, part, part, part,
                  pl.BlockSpec((d, d), lambda i: (0, 0)),
                  pl.BlockSpec((tm, d), lambda i: (i, 0)),
                  _mod_spec(g1, tm, rows_per_batch),
                  pl.BlockSpec((1, d), lambda i: (0, 0)),
                  _mod_spec(sc2, tm, rows_per_batch),
                  _mod_spec(sh2, tm, rows_per_batch)],
        out_specs=[pl.BlockSpec((tm, d), lambda i: (i, 0)),
                   pl.BlockSpec((tm, d), lambda i: (i, 0))],
        out_shape=[jax.ShapeDtypeStruct((t, d), F32), jax.ShapeDtypeStruct((t, d), BF16)],
        compiler_params=_cparams(("parallel",), VMEM_LIMIT),
        name="out_proj",
    )(oa, ob, oc, od, w_out, x, g1, n2.reshape(1, d), sc2, sh2)


ROUTE_SUB = 128


def _topk_rows_many(vals, k, ids=None):
    if ids is None:
        ids = _iota2(vals[0].shape, 0).astype(F32)
    n = range(len(vals))
    out_v = [[] for _ in n]
    out_i = [[] for _ in n]
    for _ in range(k):
        m = [jnp.max(vals[i], axis=0, keepdims=True) for i in n]
        idx = [jnp.min(jnp.where(vals[i] == m[i], ids, 1e9), axis=0, keepdims=True) for i in n]
        vals = [jnp.where(ids == idx[i], -jnp.inf, vals[i]) for i in n]
        for i in n:
            out_v[i].append(m[i])
            out_i[i].append(idx[i])
    return [(jnp.concatenate(out_v[i], axis=0), jnp.concatenate(out_i[i], axis=0)) for i in n]


_CAND_ROWS = [(a, PEER_TOPK // (a + 1)) for a in range(PEER_TOPK)]
_N_CAND = sum(nb for _, nb in _CAND_ROWS)
_N_CAND_PAD = -(-_N_CAND // 8) * 8


def _cand_ids(n):
    r = _iota2((_N_CAND_PAD, n), 0)
    ids = jnp.full((_N_CAND_PAD, n), 1e9, F32)
    start = 0
    for a, nb in _CAND_ROWS:
        ids = jnp.where((r >= start) & (r < start + nb), (a * PEER_TOPK + r - start).astype(F32), ids)
        start += nb
    return ids


def _select_rows(table, sel):
    out = jnp.zeros(sel.shape, F32)
    for a in range(table.shape[0]):
        out = jnp.where(sel == float(a), table[a:a + 1, :], out)
    return out


def _peer_route_kernel(h_ref, wq_ref, sk_ref, e1_ref, e2_ref, gt_ref, q_scr, e1_scr, e2_scr, gt_scr):
    tm = h_ref.shape[0]
    q = _dot(h_ref[...], wq_ref[...])
    for cgrp in range(2 * PEER_HEADS):
        q_scr[cgrp] = q[:, cgrp * PK_DIM:(cgrp + 1) * PK_DIM]
    sk0 = sk_ref[0]
    sk1 = sk_ref[1]
    kk = PEER_TOPK

    n = min(tm, ROUTE_SUB)
    cand_ids = _cand_ids(n)
    subs = range(tm // n)

    def head_body(hd, _):
        s = []
        for sub in subs:
            r0 = sub * n
            s.append(_dot_nt(sk0, q_scr[2 * hd, r0:r0 + n, :]))
            s.append(_dot_nt(sk1, q_scr[2 * hd + 1, r0:r0 + n, :]))
        top = _topk_rows_many(s, kk)
        cand = []
        for sub in subs:
            sv0, sv1 = top[2 * sub][0], top[2 * sub + 1][0]
            cand.append(jnp.concatenate(
                [sv0[a:a + 1, :] + sv1[0:nb, :] for a, nb in _CAND_ROWS]
                + [jnp.full((_N_CAND_PAD - _N_CAND, n), -jnp.inf, F32)], axis=0))
        ctop = _topk_rows_many(cand, kk, cand_ids)
        ro = pl.multiple_of(hd * kk, kk)
        for sub in subs:
            r0 = sub * n
            cv, cidx = ctop[sub]
            ia = jnp.floor(cidx * (1.0 / kk))
            ib = cidx - ia * kk
            e1 = _select_rows(top[2 * sub][1], ia)
            e2 = _select_rows(top[2 * sub + 1][1], ib)
            ex = jnp.exp(cv - jnp.max(cv, axis=0, keepdims=True))
            gates = ex / jnp.sum(ex, axis=0, keepdims=True)
            e1_scr[pl.ds(ro, kk), r0:r0 + n] = e1
            e2_scr[pl.ds(ro, kk), r0:r0 + n] = e2
            gt_scr[pl.ds(ro, kk), r0:r0 + n] = gates
        return 0

    lax.fori_loop(0, PEER_HEADS, head_body, 0)

    e1_ref[...] = e1_scr[...].T
    e2_ref[...] = e2_scr[...].T
    gt_ref[...] = gt_scr[...].T


def peer_route(h2, wq, sub_keys):
    t, d = h2.shape
    tm = min(t, 256)
    nq = wq.shape[1]
    nj = PEER_HEADS * PEER_TOPK
    out = jax.ShapeDtypeStruct((t, nj), F32)
    ospec = pl.BlockSpec((tm, nj), lambda i: (i, 0))
    return pl.pallas_call(
        _peer_route_kernel,
        grid=(t // tm,),
        in_specs=[pl.BlockSpec((tm, d), lambda i: (i, 0)),
                  pl.BlockSpec((d, nq), lambda i: (0, 0)),
                  pl.BlockSpec((2, N_KEYS, PK_DIM), lambda i: (0, 0, 0))],
        out_specs=[ospec, ospec, ospec],
        out_shape=[out, out, out],
        scratch_shapes=[pltpu.VMEM((2 * PEER_HEADS, tm, PK_DIM), F32),
                        pltpu.VMEM((nj, tm), F32), pltpu.VMEM((nj, tm), F32),
                        pltpu.VMEM((nj, tm), F32)],
        compiler_params=_cparams(("parallel",), VMEM_LIMIT),
        name="peer_route",
    )(h2, wq, sub_keys)


GATE_GRP = 16


def _transpose8(vs):
    sub = _iota2(vs[0].shape, 0)
    vs = list(vs)
    for d in (4, 2, 1):
        keep = (sub & d) == 0
        out = list(vs)
        for i in range(8):
            if i & d:
                continue
            a, b = vs[i], vs[i + d]
            out[i] = jnp.where(keep, a, pltpu.roll(b, d, 0))
            out[i + d] = jnp.where(keep, pltpu.roll(a, 8 - d, 0), b)
        vs = out
    return vs


def _build_gates(e1_ref, e2_ref, gt_ref, g_scr, stage_scr):
    tm = e1_ref.shape[0]
    nj = PEER_HEADS * PEER_TOPK
    riota = _iota2((N_KEYS, nj), 0).astype(F32)

    half = GATE_GRP // 2

    def group(gi, _):
        t0 = pl.multiple_of(gi * GATE_GRP, GATE_GRP)
        for part in range(2):
            toks = [part * half + i for i in range(half)]
            e1 = [e1_ref[pl.ds(t0 + tt, 1), :] for tt in toks]
            e2 = [e2_ref[pl.ds(t0 + tt, 1), :] for tt in toks]
            g = [gt_ref[pl.ds(t0 + tt, 1), :] for tt in toks]
            pt = [jnp.where(riota == e1[i], g[i], 0.0).astype(BF16) for i in range(half)]
            qt = [jnp.where(riota == e2[i], 1.0, 0.0).astype(BF16) for i in range(half)]
            gm = [_dot_nt(pt[i], qt[i]) for i in range(half)]
            for i, tt in enumerate(toks):
                stage_scr[tt * N_KEYS:(tt + 1) * N_KEYS, :] = gm[i]

        def flush(eb, _):
            r0 = pl.multiple_of(eb * 8, 8)
            tiles = [stage_scr[pl.ds(tt * N_KEYS + r0, 8), :] for tt in range(GATE_GRP)]
            lo = _transpose8(tiles[0:8])
            hi = _transpose8(tiles[8:16])
            for r in range(8):
                rows = jnp.concatenate([lo[r], hi[r]], axis=0)
                g_scr[r0 + r, pl.ds(t0, GATE_GRP), :] = rows.astype(BF16)
            return 0

        lax.fori_loop(0, N_KEYS // 8, flush, 0)
        return 0

    lax.fori_loop(0, tm // GATE_GRP, group, 0)


PEER_TE = 1024
PEER_TE_SUB = 512


def _peer_dense_kernel(h_ref, u_ref, v_ref, e1_ref, e2_ref, gt_ref, x_ref, g2_ref, o_ref,
                       g_scr, stage_scr):
    e = pl.program_id(1)
    ne = pl.num_programs(1)

    @pl.when(e == 0)
    def _():
        o_ref[...] = jnp.zeros_like(o_ref)
        _build_gates(e1_ref, e2_ref, gt_ref, g_scr, stage_scr)

    h = h_ref[...]
    part = None
    per_sub = PEER_TE_SUB // N_KEYS
    for c in range(PEER_TE // PEER_TE_SUB):
        rows = slice(c * PEER_TE_SUB, (c + 1) * PEER_TE_SUB)
        a = _dot_nt(h, u_ref[0, rows, :])
        act = a * (lax.erf(a * (0.5 ** 0.5)) + 1.0) * 0.5
        g0 = e * (PEER_TE // N_KEYS) + c * per_sub
        g = jnp.concatenate([g_scr[g0 + k] for k in range(per_sub)], axis=1)
        hh = (g.astype(F32) * act).astype(BF16)
        p = _dot(hh, v_ref[0, rows, :])
        part = p if part is None else part + p
    o_ref[...] += part

    @pl.when(e == ne - 1)
    def _():
        o_ref[...] = x_ref[...] + g2_ref[0] * o_ref[...]


def peer_dense(h2, u_all, v_all, layer, e1, e2, gt, x, g2, rows_per_batch):
    t, d = x.shape
    tm = _row_tile(t, 512, g2, rows_per_batch)
    te = PEER_TE
    nj = PEER_HEADS * PEER_TOPK
    l = layer
    once = pl.Buffered(1)
    rspec = pl.BlockSpec((tm, nj), lambda i, e: (i, 0), pipeline_mode=once)
    return pl.pallas_call(
        _peer_dense_kernel,
        grid=(t // tm, N_EXPERTS // te),
        in_specs=[pl.BlockSpec((tm, d), lambda i, e: (i, 0), pipeline_mode=once),
                  pl.BlockSpec((1, te, d), lambda i, e: (l, e, 0)),
                  pl.BlockSpec((1, te, d), lambda i, e: (l, e, 0)),
                  rspec, rspec, rspec,
                  pl.BlockSpec((tm, d), lambda i, e: (i, 0), pipeline_mode=once),
                  _mod_spec(g2, tm, rows_per_batch)],
        out_specs=pl.BlockSpec((tm, d), lambda i, e: (i, 0)),
        out_shape=jax.ShapeDtypeStruct((t, d), F32),
        scratch_shapes=[pltpu.VMEM((N_KEYS, tm, N_KEYS), BF16),
                        pltpu.VMEM((GATE_GRP * N_KEYS, N_KEYS), F32)],
        compiler_params=_cparams(("parallel", "arbitrary"), VMEM_LIMIT),
        name="peer_dense",
    )(h2, u_all, v_all, e1, e2, gt, x, g2)


def _cast_kernel(x_ref, o_ref):
    o_ref[...] = x_ref[...].astype(BF16)


def cast_bf16(x):
    depth, n, d = x.shape
    tn = 1024
    return pl.pallas_call(
        _cast_kernel,
        grid=(depth, n // tn),
        in_specs=[pl.BlockSpec((1, tn, d), lambda l, i: (l, i, 0))],
        out_specs=pl.BlockSpec((1, tn, d), lambda l, i: (l, i, 0)),
        out_shape=jax.ShapeDtypeStruct(x.shape, BF16),
        compiler_params=_cparams(("parallel", "parallel"), VMEM_LIMIT),
        name="cast_bf16",
    )(x)


def _final_norm_kernel(x_ref, g_ref, o_ref):
    x = x_ref[...]
    o_ref[...] = x * lax.rsqrt(jnp.mean(x * x, axis=-1, keepdims=True) + EPS) * g_ref[...]


def final_norm(x, g):
    t, d = x.shape
    tm = min(t, 512)
    return pl.pallas_call(
        _final_norm_kernel,
        grid=(t // tm,),
        in_specs=[pl.BlockSpec((tm, d), lambda i: (i, 0)), pl.BlockSpec((1, d), lambda i: (0, 0))],
        out_specs=pl.BlockSpec((tm, d), lambda i: (i, 0)),
        out_shape=jax.ShapeDtypeStruct((t, d), F32),
        compiler_params=_cparams(("parallel",)),
        name="final_norm",
    )(x, g.reshape(1, d))


def _permute_w_in(w):
    d = w.shape[0]
    small = jnp.concatenate([w[:, 2048:2056], w[:, 5128:5144], w[:, 6680:6688]], axis=1)
    pad = jnp.zeros((d, N_PROJ - OFF_SMALL - small.shape[1]), w.dtype)
    return jnp.concatenate([w[:, 0:2048], w[:, 2056:5128], w[:, 5144:6680], small, pad],
                           axis=1).astype(BF16)


def _small_row(vals, off):
    return jnp.zeros((1, 128), F32).at[0, off:off + vals.shape[0]].set(vals.astype(F32))


def _layer_params(l, ada_w, ada_b, norm1_g, norm2_g, w_in, w_out, gdn_conv_w, gdn_A_log, gdn_dt_bias,
                  gdn_norm_g, sb_bias, gla_w2, gla_b2, gla_norm_g, ssd_conv_w, ssd_conv_b, ssd_A_log,
                  ssd_dt_bias, ssd_D, ssd_norm_g, peer_w_query, peer_sub_keys, peer_u, peer_v):
    del ada_w, ada_b
    w2pad = jnp.zeros((128, HC * DKC), F32).at[SM_G:SM_G + GLA_RANK, :].set(gla_w2[l])
    return {
        "norm1_g": norm1_g[l], "norm2_g": norm2_g[l],
        "w_in": _permute_w_in(w_in[l]), "w_out": w_out[l].astype(BF16),
        "gdn_conv_w": gdn_conv_w[l], "gdn_negA": -jnp.exp(gdn_A_log[l]), "gdn_dt_bias": gdn_dt_bias[l],
        "gdn_gcoef": _small_row(-jnp.exp(gdn_A_log[l]), SM_A),
        "gdn_dtb": _small_row(gdn_dt_bias[l], SM_A),
        "gdn_norm_g": gdn_norm_g[l], "sb_bias": sb_bias[l],
        "w2pad": w2pad, "gla_b2": gla_b2[l], "gla_norm_g": gla_norm_g[l],
        "ssd_conv_w": ssd_conv_w[l], "ssd_conv_b": ssd_conv_b[l],
        "ssd_negA": -jnp.exp(ssd_A_log[l]), "ssd_dt_bias": ssd_dt_bias[l],
        "ssd_arow": _small_row(-jnp.exp(ssd_A_log[l]), SM_DT),
        "ssd_dtb": _small_row(ssd_dt_bias[l], SM_DT),
        "ssd_D": ssd_D[l], "ssd_norm_g": ssd_norm_g[l],
        "wq": peer_w_query[l].astype(BF16), "sub_keys": peer_sub_keys[l],
        "peer_u": peer_u, "peer_v": peer_v, "layer": l,
    }


def _split_mod(mod):
    return [mod[:, i * D_MODEL:(i + 1) * D_MODEL] for i in range(6)]


def _peer_block(h2, x, g2, prm, rows_per_batch):
    e1, e2, gt = peer_route(h2, prm["wq"], prm["sub_keys"])
    return peer_dense(h2, prm["peer_u"], prm["peer_v"], prm["layer"], e1, e2, gt, x, g2, rows_per_batch)


def _peer_block_padded(h2, x, g2, prm):
    t = x.shape[0]
    tpad = -(-t // ROUTE_SUB) * ROUTE_SUB
    pad = lambda a: jnp.concatenate([a, jnp.zeros((tpad - t,) + a.shape[1:], a.dtype)], axis=0)
    g2p = pad(g2.reshape(t, -1)).reshape(1, tpad, -1)
    return _peer_block(pad(h2), pad(x), g2p, prm, 1)[:t]


def kernel(x_prompt, x_sample, cache_k, cache_v, state_gdn, state_gdn_conv, state_gla, state_ssd, state_ssd_conv, page_table, c_prompt, c_sample, ada_w, ada_b, norm1_g, norm2_g, w_in, w_out, gdn_conv_w, gdn_A_log, gdn_dt_bias, gdn_norm_g, sb_bias, gla_w2, gla_b2, gla_norm_g, ssd_conv_w, ssd_conv_b, ssd_A_log, ssd_dt_bias, ssd_D, ssd_norm_g, peer_w_query, peer_sub_keys, peer_u, peer_v, final_norm_g):
    bsz, seq, d = x_prompt.shape
    nb = x_sample.shape[0]
    tp = bsz * seq
    n_pool = cache_k.shape[1]
    cache_k4 = cache_k.reshape(DEPTH, n_pool, PAGE_SIZE * HB, DB)
    cache_v4 = cache_v.reshape(DEPTH, n_pool, PAGE_SIZE * HB, DB)

    n_c = bsz + nb
    r_pad = -(-n_c // 8) * 8
    c_all = jnp.concatenate([c_prompt, c_sample, jnp.zeros((r_pad - n_c, d), F32)], axis=0)
    mod = ada_mod(c_all, ada_w, ada_b)

    peer_u = cast_bf16(peer_u)
    peer_v = cast_bf16(peer_v)
    xp = x_prompt.reshape(tp, d)
    xs = x_sample.reshape(nb, d)
    outs_p, outs_s = [], []
    for l in range(DEPTH):
        prm = _layer_params(l, ada_w, ada_b, norm1_g, norm2_g, w_in, w_out, gdn_conv_w, gdn_A_log,
                            gdn_dt_bias, gdn_norm_g, sb_bias, gla_w2, gla_b2, gla_norm_g, ssd_conv_w,
                            ssd_conv_b, ssd_A_log, ssd_dt_bias, ssd_D, ssd_norm_g, peer_w_query,
                            peer_sub_keys, peer_u, peer_v)
        mp = [m.reshape(bsz, 1, d) for m in _split_mod(mod[l, 0:bsz])]
        ms = [m.reshape(1, nb, d) for m in _split_mod(mod[l, bsz:bsz + nb])]

        proj = in_proj(xp, prm["norm1_g"], mp[1], mp[0], prm["w_in"], seq)
        oa, gdn_s = gdn_prompt(proj, bsz, seq, prm["gdn_conv_w"], prm["gdn_gcoef"], prm["gdn_dtb"],
                               prm["gdn_norm_g"])
        ob = sb_prompt(proj, bsz, seq, prm["sb_bias"])
        oc, gla_s = gla_prompt(proj, bsz, seq, prm["w2pad"], prm["gla_b2"], prm["gla_norm_g"])
        od, ssd_s = ssd_prompt(proj, bsz, seq, prm["ssd_conv_w"], prm["ssd_conv_b"], prm["ssd_arow"],
                               prm["ssd_dtb"], prm["ssd_D"], prm["ssd_norm_g"])
        p3 = proj.reshape(bsz, seq, N_PROJ)
        outs_p.append((p3[:, :, OFF_KB:OFF_KB + HB * DB].reshape(bsz, seq, HB, DB),
                       p3[:, :, OFF_VB:OFF_VB + HB * DB].reshape(bsz, seq, HB, DB),
                       gdn_s, p3[:, seq - (CONV_W - 1):, OFF_QKVA:OFF_QKVA + GDN_CONV_DIM],
                       gla_s, ssd_s, p3[:, seq - (CONV_W - 1):, OFF_XBC:OFF_XBC + SSD_CONV_DIM]))
        xp, h2 = out_proj(oa, ob, oc, od, prm["w_out"], xp, mp[2], prm["norm2_g"], mp[4], mp[3], seq)
        xp = _peer_block(h2, xp, mp[5], prm, seq)

        proj_s = in_proj(xs, prm["norm1_g"], ms[1], ms[0], prm["w_in"], 1)
        ps3 = proj_s.reshape(nb, 1, N_PROJ)
        (oa_s, oc_s, od_s, gdn_n, gbuf_n, gla_n, ssd_n, sbuf_n) = sample_mixers(
            ps3, l, state_gdn, state_gdn_conv, state_gla, state_ssd, state_ssd_conv, prm)
        ob_s = sample_attn(ps3, l, cache_k4, cache_v4, page_table, prm["sb_bias"])
        outs_s.append((proj_s[:, OFF_KB:OFF_KB + HB * DB].reshape(nb, 1, HB, DB),
                       proj_s[:, OFF_VB:OFF_VB + HB * DB].reshape(nb, 1, HB, DB),
                       gdn_n, gbuf_n, gla_n, ssd_n, sbuf_n))
        xs, h2s = out_proj(oa_s.reshape(nb, -1), ob_s.reshape(nb, -1), oc_s.reshape(nb, -1),
                           od_s.reshape(nb, -1), prm["w_out"], xs, ms[2], prm["norm2_g"], ms[4], ms[3], 1)
        xs = _peer_block_padded(h2s, xs, ms[5], prm)

    y_prompt = final_norm(xp, final_norm_g).reshape(bsz, seq, d)
    y_sample = final_norm(xs, final_norm_g).reshape(nb, 1, d)
    stk = lambda lst, i: jnp.stack([s[i] for s in lst], axis=0)
    return (y_prompt, y_sample, stk(outs_p, 0), stk(outs_p, 1), stk(outs_s, 0), stk(outs_s, 1),
            stk(outs_p, 2), stk(outs_s, 2), stk(outs_p, 3), stk(outs_s, 3), stk(outs_p, 4), stk(outs_s, 4),
            stk(outs_p, 5), stk(outs_s, 5), stk(outs_p, 6), stk(outs_s, 6))
```

```python
import functools
import math

import jax
import jax.numpy as jnp
import numpy as np
from jax import lax
from jax.experimental import pallas as pl
from jax.experimental.pallas import tpu as pltpu

F32 = jnp.float32
BF16 = jnp.bfloat16
HIGHEST = lax.Precision.HIGHEST

D_MODEL = 2048
DEPTH = 2
PAGE_SIZE = 128
GROUP_WIDTH = D_MODEL // 4
HA, DKA, DVA = 4, 128, 128
HB, DB = 4, 128
HC, DKC, DVC = 4, 64, 128
GLA_RANK = 16
GLA_TAU = 16.0
HD, PD, NG, NSTATE = 8, 64, 2, 128
CONV_W = 4
CHUNK = 64
SB_BLOCK = 128
PEER_HEADS = 8
N_KEYS = 128
N_EXPERTS = N_KEYS * N_KEYS
PK_DIM = 128
PEER_TOPK = 16
EPS = 1e-6
GDN_CONV_DIM = 2 * HA * DKA + HA * DVA
SSD_CONV_DIM = HD * PD + 2 * NG * NSTATE

OFF_QKVA = 0
OFF_ZA = 1536
OFF_QB = 2048
OFF_KB = 2560
OFF_VB = 3072
OFF_QC = 3584
OFF_KC = 3840
OFF_VC = 4096
OFF_RC = 4608
OFF_XBC = 5120
OFF_ZD = 6144
OFF_SMALL = 6656
SM_A, SM_B, SM_G, SM_DT = 0, 4, 8, 24
N_PROJ = 7168

VMEM_LIMIT = 56 * 1024 * 1024


def _cparams(sem, vmem=None):
    return pltpu.CompilerParams(dimension_semantics=sem, vmem_limit_bytes=vmem)


def _softplus(x):
    return jnp.maximum(x, 0.0) + jnp.log1p(jnp.exp(-jnp.abs(x)))


def _softplus_log(x):
    return jnp.maximum(x, 0.0) + jnp.log(1.0 + jnp.exp(-jnp.abs(x)))


def _silu(x):
    return x * jax.nn.sigmoid(x)


def _dotb(a, b):
    return _dot(a.astype(BF16), b.astype(BF16))


def _dotb_nt(a, b):
    return _dot_nt(a.astype(BF16), b.astype(BF16))


def _split2(x):
    hi = x.astype(BF16)
    return hi, (x - hi.astype(F32)).astype(BF16)


def _dot3(a, b):
    ah, al = _split2(a)
    bh, bl = _split2(b)
    return _dot(ah, bh) + (_dot(ah, bl) + _dot(al, bh))


def _dot(a, b, precision=None):
    return jnp.dot(a, b, precision=precision, preferred_element_type=F32)


def _dot_nt(a, b, precision=None):
    return lax.dot_general(a, b, (((1,), (1,)), ((), ())), precision=precision,
                           preferred_element_type=F32)


def _split3(x):
    hi = x.astype(BF16)
    r1 = x - hi.astype(F32)
    mid = r1.astype(BF16)
    lo = (r1 - mid.astype(F32)).astype(BF16)
    return hi, mid, lo


def _tri_dot_left(tri_bf16, x):
    hi, mid, lo = _split3(x)
    return (_dot(tri_bf16, hi) + _dot(tri_bf16, mid)) + _dot(tri_bf16, lo)


def _tri_dot_right(x, tri_bf16):
    hi, mid, lo = _split3(x)
    return (_dot(hi, tri_bf16) + _dot(mid, tri_bf16)) + _dot(lo, tri_bf16)


def _iota2(shape, dim):
    return lax.broadcasted_iota(jnp.int32, shape, dim)


def _col_from_row(r):
    n = r.shape[1]
    eye = _iota2((n, n), 0) == _iota2((n, n), 1)
    return jnp.sum(jnp.where(eye, jnp.broadcast_to(r, (n, n)), 0.0), axis=1, keepdims=True)


def _ada_kernel(c_ref, w_ref, b_ref, o_ref):
    c = c_ref[...]
    o_ref[0] = _dot(_silu(c), w_ref[0]) + b_ref[0]


def ada_mod(c_all, ada_w, ada_b):
    r = c_all.shape[0]
    n = ada_w.shape[2]
    tn = 1024
    return pl.pallas_call(
        _ada_kernel,
        grid=(DEPTH, n // tn),
        in_specs=[pl.BlockSpec((r, D_MODEL), lambda l, j: (0, 0)),
                  pl.BlockSpec((1, D_MODEL, tn), lambda l, j: (l, 0, j)),
                  pl.BlockSpec((1, 1, tn), lambda l, j: (l, 0, j))],
        out_specs=pl.BlockSpec((1, r, tn), lambda l, j: (l, 0, j)),
        out_shape=jax.ShapeDtypeStruct((DEPTH, r, n), F32),
        compiler_params=_cparams(("parallel", "parallel"), VMEM_LIMIT),
        name="ada_mod",
    )(c_all, ada_w, ada_b.reshape(DEPTH, 1, n))


def _in_proj_kernel(x_ref, g_ref, sc_ref, sh_ref, w_ref, o_ref, h_scr):
    @pl.when(pl.program_id(1) == 0)
    def _():
        x = x_ref[...]
        y = x * lax.rsqrt(jnp.mean(x * x, axis=-1, keepdims=True) + EPS) * g_ref[...]
        h_scr[...] = (y * (1.0 + sc_ref[0]) + sh_ref[0]).astype(BF16)

    o_ref[...] = _dot(h_scr[...], w_ref[...])


def _row_tile(t, cap, mod, rows_per_batch):
    return min(t, cap, rows_per_batch) if mod.shape[1] == 1 else min(t, cap)


def _mod_spec(mod, tm, rows_per_batch):
    nb, r, d = mod.shape
    if r == 1:
        return pl.BlockSpec((1, 1, d), lambda i, *_: ((i * tm) // rows_per_batch, 0, 0))
    return pl.BlockSpec((1, r, d), lambda i, *_: (0, 0, 0))


def in_proj(x, g, sc, sh, w, rows_per_batch):
    t, d = x.shape
    n = w.shape[1]
    tm = _row_tile(t, 1024, sc, rows_per_batch)
    tn = 1024
    return pl.pallas_call(
        _in_proj_kernel,
        grid=(t // tm, n // tn),
        in_specs=[pl.BlockSpec((tm, d), lambda i, j: (i, 0)),
                  pl.BlockSpec((1, d), lambda i, j: (0, 0)),
                  _mod_spec(sc, tm, rows_per_batch),
                  _mod_spec(sh, tm, rows_per_batch),
                  pl.BlockSpec((d, tn), lambda i, j: (0, j))],
        out_specs=pl.BlockSpec((tm, tn), lambda i, j: (i, j)),
        out_shape=jax.ShapeDtypeStruct((t, n), F32),
        scratch_shapes=[pltpu.VMEM((tm, d), BF16)],
        compiler_params=_cparams(("parallel", "arbitrary"), VMEM_LIMIT),
        name="in_proj",
    )(x, g.reshape(1, d), sc, sh, w)


def _causal_conv_chunk(u, prev_ref, w_ref):
    c = u.shape[0]
    rows = _iota2(u.shape, 0)
    prev = prev_ref[...]
    out = u * w_ref[CONV_W - 1:CONV_W, :]
    for k in range(1, CONV_W):
        shifted = jnp.where(rows >= k, pltpu.roll(u, k, 0), pltpu.roll(prev, k, 0))
        out = out + shifted * w_ref[CONV_W - 1 - k:CONV_W - k, :]
    prev_ref[...] = u
    del c
    return out


def _inv_unit_lower_many(ms):
    c = ms[0].shape[0]
    n = range(len(ms))
    eye = (_iota2((c, c), 0) == _iota2((c, c), 1)).astype(F32)
    x = [-m for m in ms]
    p = [eye + x[i] for i in n]
    steps = int(math.ceil(math.log2(c))) - 1
    xs = [_split2(x[i]) for i in n]
    for _ in range(steps):
        x = [_dot(xs[i][0], xs[i][0]) + (_dot(xs[i][0], xs[i][1]) + _dot(xs[i][1], xs[i][0])) for i in n]
        xs = [_split2(x[i]) for i in n]
        ps = [_split2(p[i]) for i in n]
        p = [p[i] + (_dot(ps[i][0], xs[i][0]) + (_dot(ps[i][0], xs[i][1]) + _dot(ps[i][1], xs[i][0])))
             for i in n]
    return p


def _run_sequences(body, seq_refs, shared_refs, s_out_ref, scratch):
    ci = pl.program_id(1)
    nc = pl.num_programs(1)

    @pl.when(ci == 0)
    def _():
        for s in scratch:
            s[...] = jnp.zeros_like(s)

    nseq = s_out_ref.shape[0]
    body([tuple(r.at[0, sq] for r in seq_refs) for sq in range(nseq)], shared_refs,
         [tuple(s.at[sq] for s in scratch) for sq in range(nseq)])

    @pl.when(ci == nc - 1)
    def _():
        s_out_ref[...] = scratch[0][...]


def _gdn_prompt_kernel(qkv_ref, z_ref, sm_ref, cw_ref, gcoef_ref, dtb_ref, ng_ref,
                       o_ref, s_out_ref, s_scr, prev_scr):
    _run_sequences(_gdn_chunk, (qkv_ref, z_ref, sm_ref, o_ref), (cw_ref, gcoef_ref, dtb_ref, ng_ref),
                   s_out_ref, (s_scr, prev_scr))


def _gdn_chunk(seqs, shared, scr):
    cw_ref, gcoef_ref, dtb_ref, ng_ref = shared
    c = CHUNK
    ri = _iota2((c, c), 0)
    cj = _iota2((c, c), 1)
    incl = cj <= ri
    strict = cj < ri
    tril = incl.astype(BF16)
    q, k, v, gc, gr, bc, g_last, zs, outs, states = [], [], [], [], [], [], [], [], [], []
    for (qkv_ref, z_ref, sm_ref, o_ref), (s_scr, prev_scr) in zip(seqs, scr):
        x = _silu(_causal_conv_chunk(qkv_ref[...], prev_scr, cw_ref))
        sm = sm_ref[:, 0:128]
        g = gcoef_ref[...] * _softplus(sm + dtb_ref[...])
        beta = jax.nn.sigmoid(sm)
        gcum = _tri_dot_left(tril, g)
        gcum_t = gcum.T
        z = z_ref[...]
        for h in range(HA):
            q.append(x[:, h * DKA:(h + 1) * DKA])
            k.append(x[:, HA * DKA + h * DKA:HA * DKA + (h + 1) * DKA])
            v.append(x[:, 2 * HA * DKA + h * DVA:2 * HA * DKA + (h + 1) * DVA])
            gc.append(gcum[:, SM_A + h:SM_A + h + 1])
            gr.append(gcum_t[SM_A + h:SM_A + h + 1, :])
            bc.append(beta[:, SM_B + h:SM_B + h + 1])
            g_last.append(gcum[c - 1:c, SM_A + h:SM_A + h + 1])
            zs.append(z[:, h * DVA:(h + 1) * DVA])
            outs.append((o_ref, h))
            states.append((s_scr, h))
    n = range(len(q))
    q = [q[i] * lax.rsqrt(jnp.sum(q[i] * q[i], axis=-1, keepdims=True) + EPS) * (DKA ** -0.5) for i in n]
    k = [k[i] * lax.rsqrt(jnp.sum(k[i] * k[i], axis=-1, keepdims=True) + EPS) for i in n]
    decay = [jnp.exp(jnp.where(incl, gc[i] - gr[i], -jnp.inf)) for i in n]
    eg = [jnp.exp(gc[i]) for i in n]
    kb = [k[i].astype(BF16) for i in n]
    kk = [_dot_nt(kb[i], kb[i]) for i in n]
    m = [jnp.where(strict, decay[i], 0.0) * kk[i] * bc[i] for i in n]
    tinv = _inv_unit_lower_many(m)
    tb = [tinv[i].astype(BF16) for i in n]
    w = [_dot(tb[i], ((bc[i] * eg[i]) * k[i]).astype(BF16)) for i in n]
    u = [_dot(tb[i], (bc[i] * v[i]).astype(BF16)) for i in n]
    s = [ref[h] for ref, h in states]
    sb = [s[i].astype(BF16) for i in n]
    u = [u[i] - _dot(w[i].astype(BF16), sb[i]) for i in n]
    ub = [u[i].astype(BF16) for i in n]
    attn = [_dot_nt(q[i].astype(BF16), kb[i]) * decay[i] for i in n]
    o = [_dot((q[i] * eg[i]).astype(BF16), sb[i]) + _dot(attn[i].astype(BF16), ub[i]) for i in n]
    kd = [(k[i] * jnp.exp(g_last[i] - gc[i])).T.astype(BF16) for i in n]
    s_new = [jnp.exp(g_last[i]) * s[i] + _dot(kd[i], ub[i]) for i in n]
    for i in n:
        ref, h = states[i]
        ref[h] = s_new[i]
    for i in n:
        on = o[i] * lax.rsqrt(jnp.mean(o[i] * o[i], axis=-1, keepdims=True) + EPS) * ng_ref[...]
        ref, h = outs[i]
        ref[:, h * DVA:(h + 1) * DVA] = (on * _silu(zs[i])).astype(BF16)


def _seqs_per_step(bsz):
    return 4 if bsz % 4 == 0 else (2 if bsz % 2 == 0 else 1)


def _seq_spec(nseq, width, col_block):
    return pl.BlockSpec((1, nseq, CHUNK, width), lambda b, i: (b, 0, i, col_block))


def _fixed_spec(shape):
    return pl.BlockSpec(shape, lambda b, i: (0,) * len(shape))


def gdn_prompt(proj, bsz, seq, conv_w, gcoef, dtb, norm_g):
    nc = seq // CHUNK
    c = CHUNK
    ns = _seqs_per_step(bsz)
    proj4 = proj.reshape(bsz // ns, ns, seq, N_PROJ)
    o, s = pl.pallas_call(
        _gdn_prompt_kernel,
        grid=(bsz // ns, nc),
        in_specs=[_seq_spec(ns, GDN_CONV_DIM, OFF_QKVA // GDN_CONV_DIM),
                  _seq_spec(ns, 512, OFF_ZA // 512),
                  _seq_spec(ns, 512, OFF_SMALL // 512),
                  _fixed_spec((CONV_W, GDN_CONV_DIM)), _fixed_spec((1, 128)), _fixed_spec((1, 128)),
                  _fixed_spec((1, DVA))],
        out_specs=[_seq_spec(ns, HA * DVA, 0),
                   pl.BlockSpec((ns, HA, DKA, DVA), lambda b, i: (b, 0, 0, 0))],
        out_shape=[jax.ShapeDtypeStruct((bsz // ns, ns, seq, HA * DVA), BF16),
                   jax.ShapeDtypeStruct((bsz, HA, DKA, DVA), F32)],
        scratch_shapes=[pltpu.VMEM((ns, HA, DKA, DVA), F32), pltpu.VMEM((ns, c, GDN_CONV_DIM), F32)],
        compiler_params=_cparams(("parallel", "arbitrary")),
        name="gdn_prompt",
    )(proj4, proj4, proj4, conv_w, gcoef, dtb, norm_g.reshape(1, DVA))
    return o.reshape(bsz * seq, HA * DVA), s


SB_TQ = 512
SB_TK = 256


def _sb_prompt_kernel(bias_ref, q_ref, k_ref, v_ref, o_ref):
    h = pl.program_id(1)
    qi = pl.program_id(2)
    tq = q_ref.shape[0]
    tk = min(SB_TK, tq)
    bias = bias_ref[h]
    q = q_ref[...].astype(BF16)
    ri = _iota2((tq, tk), 0)
    cj = _iota2((tq, tk), 1)
    tri = (_iota2((tk, tk), 0) > _iota2((tk, tk), 1)).astype(BF16)
    nkb = (qi + 1) * (tq // tk)

    def make_body(masked):
        def body(jj, carry):
            acc, run = carry
            j = nkb - 1 - jj
            off = pl.multiple_of(j * tk, tk)
            kb = k_ref[pl.ds(off, tk), :].astype(BF16)
            vb = v_ref[pl.ds(off, tk), :].astype(BF16)
            z = _dot_nt(q, kb) * (DB ** -0.5) + bias
            sp = _softplus_log(z)
            lf = -sp
            if masked:
                mask = (j * tk + cj) < (qi * tq + ri)
                lf = jnp.where(mask, lf, 0.0)
            hi, lo = _split2(lf)
            after = (_dot(hi, tri) + _dot(lo, tri)) + run
            a = jnp.exp((z - sp) + after)
            if masked:
                a = jnp.where(mask, a, 0.0)
            acc = acc + _dot(a.astype(BF16), vb)
            run = run + jnp.sum(lf, axis=-1, keepdims=True)
            return acc, run
        return body

    ndiag = tq // tk
    carry = lax.fori_loop(0, ndiag, make_body(True),
                          (jnp.zeros((tq, DB), F32), jnp.zeros((tq, 1), F32)))
    acc, _ = lax.fori_loop(ndiag, nkb, make_body(False), carry)
    o_ref[...] = acc.astype(BF16)


def sb_prompt(proj, bsz, seq, bias):
    tq = min(SB_TQ, seq)
    nq = seq // tq
    return pl.pallas_call(
        _sb_prompt_kernel,
        grid=(bsz, HB, nq),
        in_specs=[pl.BlockSpec(memory_space=pltpu.SMEM),
                  pl.BlockSpec((tq, DB), lambda b, h, i: (b * nq + i, OFF_QB // DB + h)),
                  pl.BlockSpec((seq, DB), lambda b, h, i: (b, OFF_KB // DB + h)),
                  pl.BlockSpec((seq, DB), lambda b, h, i: (b, OFF_VB // DB + h))],
        out_specs=pl.BlockSpec((tq, DB), lambda b, h, i: (b * nq + i, h)),
        out_shape=jax.ShapeDtypeStruct((bsz * seq, HB * DB), BF16),
        compiler_params=_cparams(("parallel", "parallel", "arbitrary")),
        name="sb_prompt",
    )(bias, proj, proj, proj)


GLA_SUB = 16


def _gla_prompt_kernel(q_ref, k_ref, v_ref, r_ref, sm_ref, w2_ref, b2_ref, ng_ref,
                       o_ref, s_out_ref, s_scr):
    _run_sequences(_gla_chunk, (q_ref, k_ref, v_ref, r_ref, sm_ref, o_ref), (w2_ref, b2_ref, ng_ref),
                   s_out_ref, (s_scr,))


def _gla_chunk(seqs, shared, scr):
    w2_ref, b2_ref, ng_ref = shared
    c = CHUNK
    ri = _iota2((c, c), 0)
    cj = _iota2((c, c), 1)
    incl = cj <= ri
    tril = incl.astype(BF16)
    jrow = _iota2((c, DKC), 0)
    q, k, v, r, bc, b_last_col, outs, states = [], [], [], [], [], [], [], []
    for (q_ref, k_ref, v_ref, r_ref, sm_ref, o_ref), (s_scr,) in zip(seqs, scr):
        sm = sm_ref[:, 0:128]
        pre = _dot3(sm, w2_ref[...]) + b2_ref[...]
        loga = -_softplus(-pre) * (1.0 / GLA_TAU)
        bcum = _tri_dot_left(tril, loga)
        bcum_t = bcum.T
        qa, ka, va, ra = q_ref[...], k_ref[...], v_ref[...], r_ref[...]
        for h in range(HC):
            q.append(qa[:, h * DKC:(h + 1) * DKC] * (DKC ** -0.5))
            k.append(ka[:, h * DKC:(h + 1) * DKC])
            v.append(va[:, h * DVC:(h + 1) * DVC])
            r.append(ra[:, h * DVC:(h + 1) * DVC])
            bc.append(bcum[:, h * DKC:(h + 1) * DKC])
            b_last_col.append(bcum_t[h * DKC:(h + 1) * DKC, c - 1:c])
            outs.append((o_ref, h))
            states.append((s_scr, h))
    n = range(len(q))
    rows = [[] for _ in n]
    for sb in range(c // GLA_SUB):
        i0 = sb * GLA_SUB
        ref = [bc[i][i0:i0 + 1, :] for i in n]
        qe = [(q[i][i0:i0 + GLA_SUB] * jnp.exp(bc[i][i0:i0 + GLA_SUB] - ref[i])).astype(BF16) for i in n]
        ke = [(k[i] * jnp.exp(jnp.where(jrow < i0 + GLA_SUB, ref[i] - bc[i], 0.0))).astype(BF16) for i in n]
        for i in n:
            rows[i].append(_dot_nt(qe[i], ke[i]))
    attn = [jnp.where(incl, jnp.concatenate(rows[i], axis=0), 0.0).astype(BF16) for i in n]
    s = [ref_[h] for ref_, h in states]
    vb = [v[i].astype(BF16) for i in n]
    o = [_dot((q[i] * jnp.exp(bc[i])).astype(BF16), s[i].astype(BF16)) + _dot(attn[i], vb[i]) for i in n]
    kd = [(k[i] * jnp.exp(bc[i][c - 1:c, :] - bc[i])).T.astype(BF16) for i in n]
    s_new = [jnp.exp(b_last_col[i]) * s[i] + _dot(kd[i], vb[i]) for i in n]
    for i in n:
        ref_, h = states[i]
        ref_[h] = s_new[i]
    for i in n:
        on = o[i] * lax.rsqrt(jnp.mean(o[i] * o[i], axis=-1, keepdims=True) + EPS) * ng_ref[...]
        ref_, h = outs[i]
        ref_[:, h * DVC:(h + 1) * DVC] = (on * _silu(r[i])).astype(BF16)


def gla_prompt(proj, bsz, seq, w2pad, b2, norm_g):
    nc = seq // CHUNK
    ns = _seqs_per_step(bsz)
    proj4 = proj.reshape(bsz // ns, ns, seq, N_PROJ)
    o, s = pl.pallas_call(
        _gla_prompt_kernel,
        grid=(bsz // ns, nc),
        in_specs=[_seq_spec(ns, HC * DKC, OFF_QC // (HC * DKC)),
                  _seq_spec(ns, HC * DKC, OFF_KC // (HC * DKC)),
                  _seq_spec(ns, HC * DVC, OFF_VC // (HC * DVC)),
                  _seq_spec(ns, HC * DVC, OFF_RC // (HC * DVC)),
                  _seq_spec(ns, 512, OFF_SMALL // 512),
                  _fixed_spec((128, HC * DKC)), _fixed_spec((1, HC * DKC)), _fixed_spec((1, DVC))],
        out_specs=[_seq_spec(ns, HC * DVC, 0),
                   pl.BlockSpec((ns, HC, DKC, DVC), lambda b, i: (b, 0, 0, 0))],
        out_shape=[jax.ShapeDtypeStruct((bsz // ns, ns, seq, HC * DVC), BF16),
                   jax.ShapeDtypeStruct((bsz, HC, DKC, DVC), F32)],
        scratch_shapes=[pltpu.VMEM((ns, HC, DKC, DVC), F32)],
        compiler_params=_cparams(("parallel", "arbitrary")),
        name="gla_prompt",
    )(proj4, proj4, proj4, proj4, proj4, w2pad, b2.reshape(1, HC * DKC), norm_g.reshape(1, DVC))
    return o.reshape(bsz * seq, HC * DVC), s


def _ssd_prompt_kernel(dvec_ref, xbc_ref, z_ref, sm_ref, cw_ref, cb_ref, arow_ref, dtb_ref, ng_ref,
                       o_ref, s_out_ref, s_scr, prev_scr):
    _run_sequences(_ssd_chunk, (xbc_ref, z_ref, sm_ref, o_ref),
                   (dvec_ref, cw_ref, cb_ref, arow_ref, dtb_ref, ng_ref), s_out_ref, (s_scr, prev_scr))


def _ssd_chunk(seqs, shared, scr):
    dvec_ref, cw_ref, cb_ref, arow_ref, dtb_ref, ng_ref = shared
    c = CHUNK
    ri = _iota2((c, c), 0)
    cj = _iota2((c, c), 1)
    incl = cj <= ri
    tril = incl.astype(BF16)
    hpg = HD // NG
    xh, zh, bgb, cg, cbg, li, lj, dti, dtj, l_last, dcoef, states = ([] for _ in range(12))
    for (xbc_ref, z_ref, sm_ref, o_ref), (s_scr, prev_scr) in zip(seqs, scr):
        xbc = _silu(_causal_conv_chunk(xbc_ref[...], prev_scr, cw_ref) + cb_ref[...])
        sm = sm_ref[:, 0:128]
        dt = _softplus(sm + dtb_ref[...])
        lcum = _tri_dot_left(tril, dt * arow_ref[...])
        lcum_t = lcum.T
        dt_t = dt.T
        z = z_ref[...]
        grp = []
        for g in range(NG):
            b_g = xbc[:, HD * PD + g * NSTATE:HD * PD + (g + 1) * NSTATE]
            c_g = xbc[:, HD * PD + NG * NSTATE + g * NSTATE:HD * PD + NG * NSTATE + (g + 1) * NSTATE]
            b_gb = b_g.astype(BF16)
            grp.append((b_gb, c_g, _dot_nt(c_g.astype(BF16), b_gb)))
        for h in range(HD):
            b_gb, c_g, cb_g = grp[h // hpg]
            lane = SM_DT + h
            xh.append(xbc[:, h * PD:(h + 1) * PD])
            zh.append(z[:, h * PD:(h + 1) * PD])
            bgb.append(b_gb)
            cg.append(c_g)
            cbg.append(cb_g)
            li.append(lcum[:, lane:lane + 1])
            lj.append(lcum_t[lane:lane + 1, :])
            dti.append(dt[:, lane:lane + 1])
            dtj.append(dt_t[lane:lane + 1, :])
            l_last.append(lcum[c - 1:c, lane:lane + 1])
            dcoef.append(dvec_ref[h])
            states.append((s_scr, h))
    n = range(len(xh))
    scores = [(cbg[i] * jnp.exp(jnp.where(incl, li[i] - lj[i], -jnp.inf)) * dtj[i]).astype(BF16) for i in n]
    s = [ref[h] for ref, h in states]
    xb = [xh[i].astype(BF16) for i in n]
    y = [_dot(scores[i], xb[i]) + _dot_nt((cg[i] * jnp.exp(li[i])).astype(BF16), s[i].astype(BF16))
         for i in n]
    xs = [(xh[i] * (dti[i] * jnp.exp(l_last[i] - li[i]))).T.astype(BF16) for i in n]
    s_new = [jnp.exp(l_last[i]) * s[i] + _dot(xs[i], bgb[i]) for i in n]
    for i in n:
        ref, h = states[i]
        ref[h] = s_new[i]
    ys = [(y[i] + dcoef[i] * xh[i]) * _silu(zh[i]) for i in n]
    gw = HD * PD // NG
    for sq, (_, _, _, o_ref) in enumerate(seqs):
        for g in range(NG):
            first = sq * HD + g * hpg
            yg = jnp.concatenate(ys[first:first + hpg], axis=-1)
            yn = yg * lax.rsqrt(jnp.mean(yg * yg, axis=-1, keepdims=True) + EPS)
            o_ref[:, g * gw:(g + 1) * gw] = (yn * ng_ref[:, g * gw:(g + 1) * gw]).astype(BF16)


def ssd_prompt(proj, bsz, seq, conv_w, conv_b, arow, dtb, dvec, norm_g):
    nc = seq // CHUNK
    c = CHUNK
    ns = _seqs_per_step(bsz)
    proj4 = proj.reshape(bsz // ns, ns, seq, N_PROJ)
    o, s = pl.pallas_call(
        _ssd_prompt_kernel,
        grid=(bsz // ns, nc),
        in_specs=[pl.BlockSpec(memory_space=pltpu.SMEM),
                  _seq_spec(ns, SSD_CONV_DIM, OFF_XBC // SSD_CONV_DIM),
                  _seq_spec(ns, 512, OFF_ZD // 512),
                  _seq_spec(ns, 512, OFF_SMALL // 512),
                  _fixed_spec((CONV_W, SSD_CONV_DIM)), _fixed_spec((1, SSD_CONV_DIM)),
                  _fixed_spec((1, 128)), _fixed_spec((1, 128)), _fixed_spec((1, HD * PD))],
        out_specs=[_seq_spec(ns, HD * PD, 0),
                   pl.BlockSpec((ns, HD, PD, NSTATE), lambda b, i: (b, 0, 0, 0))],
        out_shape=[jax.ShapeDtypeStruct((bsz // ns, ns, seq, HD * PD), BF16),
                   jax.ShapeDtypeStruct((bsz, HD, PD, NSTATE), F32)],
        scratch_shapes=[pltpu.VMEM((ns, HD, PD, NSTATE), F32), pltpu.VMEM((ns, c, SSD_CONV_DIM), F32)],
        compiler_params=_cparams(("parallel", "arbitrary")),
        name="ssd_prompt",
    )(dvec, proj4, proj4, proj4, conv_w, conv_b.reshape(1, SSD_CONV_DIM), arow, dtb,
      norm_g.reshape(1, HD * PD))
    return o.reshape(bsz * seq, HD * PD), s


def _row8(r):
    return jnp.concatenate([r, jnp.zeros((7, r.shape[1]), F32)], axis=0)


def _sample_mixers_kernel(gneg_ref, gdtb_ref, aneg_ref, sdtb_ref, dvec_ref,
                          p_ref, gs_ref, gbuf_ref, ls_ref, ss_ref, sbuf_ref,
                          gcw_ref, gng_ref, w2_ref, b2_ref, lng_ref, scw_ref, scb_ref, sng_ref,
                          oa_ref, oc_ref, od_ref, gs_out, gbuf_out, ls_out, ss_out, sbuf_out):
    l = 0
    sm = p_ref[0, :, OFF_SMALL:OFF_SMALL + 128]

    u = p_ref[0, :, OFF_QKVA:OFF_QKVA + GDN_CONV_DIM]
    buf = gbuf_ref[0, 0]
    conv = (buf[0:1] * gcw_ref[0:1, :] + buf[1:2] * gcw_ref[1:2, :]
            + buf[2:3] * gcw_ref[2:3, :] + u * gcw_ref[3:4, :])
    gbuf_out[0, 0:2, :] = buf[1:3]
    gbuf_out[0, 2:3, :] = u
    x = _silu(conv)
    za = p_ref[0, :, OFF_ZA:OFF_ZA + HA * DVA]
    ha = range(HA)
    q = [x[:, h * DKA:(h + 1) * DKA] for h in ha]
    k = [x[:, HA * DKA + h * DKA:HA * DKA + (h + 1) * DKA] for h in ha]
    v = [x[:, 2 * HA * DKA + h * DVA:2 * HA * DKA + (h + 1) * DVA] for h in ha]
    q = [q[h] * lax.rsqrt(jnp.sum(q[h] * q[h], axis=-1, keepdims=True) + EPS) * (DKA ** -0.5) for h in ha]
    k = [k[h] * lax.rsqrt(jnp.sum(k[h] * k[h], axis=-1, keepdims=True) + EPS) for h in ha]
    g = [gneg_ref[h] * _softplus(sm[:, SM_A + h:SM_A + h + 1] + gdtb_ref[h]) for h in ha]
    b = [jax.nn.sigmoid(sm[:, SM_B + h:SM_B + h + 1]) for h in ha]
    eg = [jnp.exp(g[h]) for h in ha]
    s = [gs_ref[0, 0, h] for h in ha]
    lhs = [jnp.concatenate([k[h], q[h] * eg[h], jnp.zeros((6, DKA), F32)], axis=0) for h in ha]
    kq = [_dot(lhs[h], s[h], HIGHEST) for h in ha]
    uu = [b[h] * v[h] - (b[h] * eg[h]) * kq[h][0:1] for h in ha]
    o = [kq[h][1:2] + jnp.sum(q[h] * k[h], axis=-1, keepdims=True) * uu[h] for h in ha]
    kcol = [_col_from_row(k[h]) for h in ha]
    for h in ha:
        gs_out[0, h] = eg[h] * s[h] + kcol[h] * uu[h]
    for h in ha:
        on = o[h] * lax.rsqrt(jnp.mean(o[h] * o[h], axis=-1, keepdims=True) + EPS) * gng_ref[...]
        oa_ref[0, :, h * DVA:(h + 1) * DVA] = (on * _silu(za[:, h * DVA:(h + 1) * DVA])).astype(BF16)

    pre = _dot(jnp.broadcast_to(sm, (8, 128)), w2_ref[...], HIGHEST)[0:1] + b2_ref[...]
    loga = -_softplus(-pre) * (1.0 / GLA_TAU)
    qc = p_ref[0, :, OFF_QC:OFF_QC + HC * DKC]
    kc = p_ref[0, :, OFF_KC:OFF_KC + HC * DKC]
    vc = p_ref[0, :, OFF_VC:OFF_VC + HC * DVC]
    rc = p_ref[0, :, OFF_RC:OFF_RC + HC * DVC]
    hc = range(HC)
    q = [qc[:, h * DKC:(h + 1) * DKC] * (DKC ** -0.5) for h in hc]
    k = [kc[:, h * DKC:(h + 1) * DKC] for h in hc]
    v = [vc[:, h * DVC:(h + 1) * DVC] for h in hc]
    ea = [jnp.exp(loga[:, h * DKC:(h + 1) * DKC]) for h in hc]
    s = [ls_ref[0, 0, h] for h in hc]
    qs = [_dot(_row8(q[h] * ea[h]), s[h], HIGHEST)[0:1] for h in hc]
    o = [qs[h] + jnp.sum(q[h] * k[h], axis=-1, keepdims=True) * v[h] for h in hc]
    eacol = [_col_from_row(ea[h]) for h in hc]
    kcol = [_col_from_row(k[h]) for h in hc]
    for h in hc:
        ls_out[0, h] = eacol[h] * s[h] + kcol[h] * v[h]
    for h in hc:
        on = o[h] * lax.rsqrt(jnp.mean(o[h] * o[h], axis=-1, keepdims=True) + EPS) * lng_ref[...]
        oc_ref[0, :, h * DVC:(h + 1) * DVC] = (on * _silu(rc[:, h * DVC:(h + 1) * DVC])).astype(BF16)

    us = p_ref[0, :, OFF_XBC:OFF_XBC + SSD_CONV_DIM]
    sbuf = sbuf_ref[0, 0]
    sconv = (sbuf[0:1] * scw_ref[0:1, :] + sbuf[1:2] * scw_ref[1:2, :]
             + sbuf[2:3] * scw_ref[2:3, :] + us * scw_ref[3:4, :])
    sbuf_out[0, 0:2, :] = sbuf[1:3]
    sbuf_out[0, 2:3, :] = us
    xbc = _silu(sconv + scb_ref[...])
    zd = p_ref[0, :, OFF_ZD:OFF_ZD + HD * PD]
    hd = range(HD)
    grp = [h // (HD // NG) for h in hd]
    bg = [xbc[:, HD * PD + grp[h] * NSTATE:HD * PD + (grp[h] + 1) * NSTATE] for h in hd]
    cg = [xbc[:, HD * PD + NG * NSTATE + grp[h] * NSTATE:HD * PD + NG * NSTATE + (grp[h] + 1) * NSTATE]
          for h in hd]
    xh = [xbc[:, h * PD:(h + 1) * PD] for h in hd]
    dt = [_softplus(sm[:, SM_DT + h:SM_DT + h + 1] + sdtb_ref[h]) for h in hd]
    el = [jnp.exp(dt[h] * aneg_ref[h]) for h in hd]
    s = [ss_ref[0, 0, h] for h in hd]
    score = [jnp.sum(cg[h] * bg[h], axis=-1, keepdims=True) * dt[h] for h in hd]
    cs = [_dot_nt(_row8(cg[h] * el[h]), s[h], HIGHEST)[0:1] for h in hd]
    xcol = [_col_from_row(xh[h] * dt[h]) for h in hd]
    for h in hd:
        ss_out[0, h] = el[h] * s[h] + xcol[h] * bg[h]
    ys = [(score[h] * xh[h] + cs[h] + dvec_ref[h] * xh[h]) * _silu(zd[:, h * PD:(h + 1) * PD]) for h in hd]
    gw = HD * PD // NG
    for g in range(NG):
        yg = jnp.concatenate(ys[g * (HD // NG):(g + 1) * (HD // NG)], axis=-1)
        yn = yg * lax.rsqrt(jnp.mean(yg * yg, axis=-1, keepdims=True) + EPS)
        od_ref[0, :, g * gw:(g + 1) * gw] = (yn * sng_ref[:, g * gw:(g + 1) * gw]).astype(BF16)
    del l


def sample_mixers(proj3, layer, state_gdn, state_gdn_conv, state_gla, state_ssd, state_ssd_conv, prm):
    nb = proj3.shape[0]
    smem = pl.BlockSpec(memory_space=pltpu.SMEM)
    full = lambda shape: pl.BlockSpec(shape, lambda b: (0,) * len(shape))
    l = layer
    outs = pl.pallas_call(
        _sample_mixers_kernel,
        grid=(nb,),
        in_specs=[smem, smem, smem, smem, smem,
                  pl.BlockSpec((1, 1, N_PROJ), lambda b: (b, 0, 0)),
                  pl.BlockSpec((1, 1, HA, DKA, DVA), lambda b: (l, b, 0, 0, 0)),
                  pl.BlockSpec((1, 1, CONV_W - 1, GDN_CONV_DIM), lambda b: (l, b, 0, 0)),
                  pl.BlockSpec((1, 1, HC, DKC, DVC), lambda b: (l, b, 0, 0, 0)),
                  pl.BlockSpec((1, 1, HD, PD, NSTATE), lambda b: (l, b, 0, 0, 0)),
                  pl.BlockSpec((1, 1, CONV_W - 1, SSD_CONV_DIM), lambda b: (l, b, 0, 0)),
                  full((CONV_W, GDN_CONV_DIM)), full((1, DVA)),
                  full((128, HC * DKC)), full((1, HC * DKC)), full((1, DVC)),
                  full((CONV_W, SSD_CONV_DIM)), full((1, SSD_CONV_DIM)), full((1, HD * PD))],
        out_specs=[pl.BlockSpec((1, 1, HA * DVA), lambda b: (b, 0, 0)),
                   pl.BlockSpec((1, 1, HC * DVC), lambda b: (b, 0, 0)),
                   pl.BlockSpec((1, 1, HD * PD), lambda b: (b, 0, 0)),
                   pl.BlockSpec((1, HA, DKA, DVA), lambda b: (b, 0, 0, 0)),
                   pl.BlockSpec((1, CONV_W - 1, GDN_CONV_DIM), lambda b: (b, 0, 0)),
                   pl.BlockSpec((1, HC, DKC, DVC), lambda b: (b, 0, 0, 0)),
                   pl.BlockSpec((1, HD, PD, NSTATE), lambda b: (b, 0, 0, 0)),
                   pl.BlockSpec((1, CONV_W - 1, SSD_CONV_DIM), lambda b: (b, 0, 0))],
        out_shape=[jax.ShapeDtypeStruct((nb, 1, HA * DVA), BF16),
                   jax.ShapeDtypeStruct((nb, 1, HC * DVC), BF16),
                   jax.ShapeDtypeStruct((nb, 1, HD * PD), BF16),
                   jax.ShapeDtypeStruct((nb, HA, DKA, DVA), F32),
                   jax.ShapeDtypeStruct((nb, CONV_W - 1, GDN_CONV_DIM), F32),
                   jax.ShapeDtypeStruct((nb, HC, DKC, DVC), F32),
                   jax.ShapeDtypeStruct((nb, HD, PD, NSTATE), F32),
                   jax.ShapeDtypeStruct((nb, CONV_W - 1, SSD_CONV_DIM), F32)],
        compiler_params=_cparams(("parallel",)),
        name="sample_mixers",
    )(prm["gdn_negA"], prm["gdn_dt_bias"], prm["ssd_negA"], prm["ssd_dt_bias"], prm["ssd_D"],
      proj3, state_gdn, state_gdn_conv, state_gla, state_ssd, state_ssd_conv,
      prm["gdn_conv_w"], prm["gdn_norm_g"].reshape(1, DVA),
      prm["w2pad"], prm["gla_b2"].reshape(1, HC * DKC), prm["gla_norm_g"].reshape(1, DVC),
      prm["ssd_conv_w"], prm["ssd_conv_b"].reshape(1, SSD_CONV_DIM),
      prm["ssd_norm_g"].reshape(1, HD * PD))
    return outs


SA_PAGES = 16


def _sample_attn_kernel(pt_ref, q_ref, brow_ref, *refs):
    k_refs = refs[0:SA_PAGES]
    v_refs = refs[SA_PAGES:2 * SA_PAGES]
    o_ref, acc_scr, run_scr = refs[2 * SA_PAGES:]
    j = pl.program_id(1)
    nj = pl.num_programs(1)

    @pl.when(j == 0)
    def _():
        acc_scr[...] = jnp.zeros_like(acc_scr)
        run_scr[...] = jnp.zeros_like(run_scr)

    w = PAGE_SIZE * HB
    q = q_ref[0]
    q8 = jnp.concatenate([q[:, h * DB:(h + 1) * DB] for h in range(HB)]
                         + [jnp.zeros((8 - HB, DB), F32)], axis=0).astype(BF16)
    rows = _iota2((8, w), 0)
    lanes = _iota2((8, w), 1)
    sel = (lanes % HB) == rows
    zs = []
    for p in range(SA_PAGES):
        kp = k_refs[p][0, 0].astype(BF16)
        zz = _dot_nt(q8, kp)
        zs.append(jnp.sum(jnp.where(sel, zz, 0.0), axis=0, keepdims=True))
    z = jnp.concatenate(zs, axis=0) * (DB ** -0.5) + brow_ref[...]
    sp = _softplus_log(z)
    lf = -sp
    plane = _iota2((SA_PAGES, w), 1)
    suf = jnp.where(plane < w - HB, pltpu.roll(lf, w - HB, 1), 0.0)
    tot = lf
    step = HB
    while step < w:
        suf = suf + jnp.where(plane < w - step, pltpu.roll(suf, w - step, 1), 0.0)
        tot = tot + pltpu.roll(tot, step, 1)
        step *= 2
    run = run_scr[0:1, :]
    runs = []
    for p in range(SA_PAGES):
        runs.append(run)
        run = run + tot[p:p + 1, :]
    run_scr[...] = jnp.broadcast_to(run, run_scr.shape)
    a = jnp.exp((z - sp) + (suf + jnp.concatenate(runs, axis=0)))
    acc = acc_scr[...]
    for p in range(SA_PAGES):
        vp = v_refs[p][0, 0].astype(BF16)
        ap = jnp.where(sel, jnp.broadcast_to(a[p:p + 1, :], (8, w)), 0.0).astype(BF16)
        acc = acc + _dot(ap, vp)
    acc_scr[...] = acc

    @pl.when(j == nj - 1)
    def _():
        o_ref[0] = acc[0:HB].astype(BF16)


def sample_attn(proj3, layer, cache_k4, cache_v4, page_table, bias):
    nb, npg = page_table.shape
    l = layer
    nj = npg // SA_PAGES

    def page_spec(p):
        return pl.BlockSpec((1, 1, PAGE_SIZE * HB, DB),
                            lambda b, j, pt: (l, pt[b, npg - 1 - (j * SA_PAGES + p)], 0, 0))

    grid_spec = pltpu.PrefetchScalarGridSpec(
        num_scalar_prefetch=1,
        grid=(nb, nj),
        in_specs=([pl.BlockSpec((1, 1, HB * DB), lambda b, j, pt: (b, 0, OFF_QB // (HB * DB))),
                   pl.BlockSpec((1, PAGE_SIZE * HB), lambda b, j, pt: (0, 0))]
                  + [page_spec(p) for p in range(SA_PAGES)]
                  + [page_spec(p) for p in range(SA_PAGES)]),
        out_specs=pl.BlockSpec((1, HB, DB), lambda b, j, pt: (b, 0, 0)),
        scratch_shapes=[pltpu.VMEM((8, DB), F32), pltpu.VMEM((8, PAGE_SIZE * HB), F32)],
    )
    brow = jnp.tile(bias.astype(F32), PAGE_SIZE).reshape(1, PAGE_SIZE * HB)
    return pl.pallas_call(
        _sample_attn_kernel,
        grid_spec=grid_spec,
        out_shape=jax.ShapeDtypeStruct((nb, HB, DB), BF16),
        compiler_params=_cparams(("parallel", "arbitrary")),
        name="sample_attn",
    )(page_table, proj3, brow, *([cache_k4] * SA_PAGES), *([cache_v4] * SA_PAGES))


def _out_proj_kernel(a_ref, b_ref, c_ref, d_ref, w_ref, x_ref, g1_ref, n2_ref, sc_ref, sh_ref,
                     xo_ref, h_ref):
    gw = GROUP_WIDTH
    acc = _dot(a_ref[...], w_ref[0:gw, :])
    acc = acc + _dot(b_ref[...], w_ref[gw:2 * gw, :])
    acc = acc + _dot(c_ref[...], w_ref[2 * gw:3 * gw, :])
    acc = acc + _dot(d_ref[...], w_ref[3 * gw:4 * gw, :])
    x = x_ref[...] + g1_ref[0] * acc
    xo_ref[...] = x
    y = x * lax.rsqrt(jnp.mean(x * x, axis=-1, keepdims=True) + EPS) * n2_ref[...]
    h_ref[...] = (y * (1.0 + sc_ref[0]) + sh_ref[0]).astype(BF16)


def out_proj(oa, ob, oc, od, w_out, x, g1, n2, sc2, sh2, rows_per_batch):
    t, d = x.shape
    tm = _row_tile(t, 512, g1, rows_per_batch)
    part = pl.BlockSpec((tm, GROUP_WIDTH), lambda i: (i, 0))
    return pl.pallas_call(
        _out_proj_kernel,
        grid=(t // tm,),
        in_specs=[part, part, part, part,
                  pl.BlockSpec((d, d), lambda i: (0, 0)),
                  pl.BlockSpec((tm, d), lambda i: (i, 0)),
                  _mod_spec(g1, tm, rows_per_batch),
                  pl.BlockSpec((1, d), lambda i: (0, 0)),
                  _mod_spec(sc2, tm, rows_per_batch),
                  _mod_spec(sh2, tm, rows_per_batch)],
        out_specs=[pl.BlockSpec((tm, d), lambda i: (i, 0)),
                   pl.BlockSpec((tm, d), lambda i: (i, 0))],
        out_shape=[jax.ShapeDtypeStruct((t, d), F32), jax.ShapeDtypeStruct((t, d), BF16)],
        compiler_params=_cparams(("parallel",), VMEM_LIMIT),
        name="out_proj",
    )(oa, ob, oc, od, w_out, x, g1, n2.reshape(1, d), sc2, sh2)


ROUTE_SUB = 128


def _topk_rows_many(vals, k, ids=None):
    if ids is None:
        ids = _iota2(vals[0].shape, 0).astype(F32)
    n = range(len(vals))
    out_v = [[] for _ in n]
    out_i = [[] for _ in n]
    for _ in range(k):
        m = [jnp.max(vals[i], axis=0, keepdims=True) for i in n]
        idx = [jnp.min(jnp.where(vals[i] == m[i], ids, 1e9), axis=0, keepdims=True) for i in n]
        vals = [jnp.where(ids == idx[i], -jnp.inf, vals[i]) for i in n]
        for i in n:
            out_v[i].append(m[i])
            out_i[i].append(idx[i])
    return [(jnp.concatenate(out_v[i], axis=0), jnp.concatenate(out_i[i], axis=0)) for i in n]


_CAND_ROWS = [(a, PEER_TOPK // (a + 1)) for a in range(PEER_TOPK)]
_N_CAND = sum(nb for _, nb in _CAND_ROWS)
_N_CAND_PAD = -(-_N_CAND // 8) * 8


def _cand_ids(n):
    r = _iota2((_N_CAND_PAD, n), 0)
    ids = jnp.full((_N_CAND_PAD, n), 1e9, F32)
    start = 0
    for a, nb in _CAND_ROWS:
        ids = jnp.where((r >= start) & (r < start + nb), (a * PEER_TOPK + r - start).astype(F32), ids)
        start += nb
    return ids


def _select_rows(table, sel):
    out = jnp.zeros(sel.shape, F32)
    for a in range(table.shape[0]):
        out = jnp.where(sel == float(a), table[a:a + 1, :], out)
    return out


def _peer_route_kernel(h_ref, wq_ref, sk_ref, e1_ref, e2_ref, gt_ref, q_scr, e1_scr, e2_scr, gt_scr):
    tm = h_ref.shape[0]
    q = _dot(h_ref[...], wq_ref[...])
    for cgrp in range(2 * PEER_HEADS):
        q_scr[cgrp] = q[:, cgrp * PK_DIM:(cgrp + 1) * PK_DIM]
    sk0 = sk_ref[0]
    sk1 = sk_ref[1]
    kk = PEER_TOPK

    n = min(tm, ROUTE_SUB)
    cand_ids = _cand_ids(n)
    subs = range(tm // n)

    def head_body(hd, _):
        s = []
        for sub in subs:
            r0 = sub * n
            s.append(_dot_nt(sk0, q_scr[2 * hd, r0:r0 + n, :]))
            s.append(_dot_nt(sk1, q_scr[2 * hd + 1, r0:r0 + n, :]))
        top = _topk_rows_many(s, kk)
        cand = []
        for sub in subs:
            sv0, sv1 = top[2 * sub][0], top[2 * sub + 1][0]
            cand.append(jnp.concatenate(
                [sv0[a:a + 1, :] + sv1[0:nb, :] for a, nb in _CAND_ROWS]
                + [jnp.full((_N_CAND_PAD - _N_CAND, n), -jnp.inf, F32)], axis=0))
        ctop = _topk_rows_many(cand, kk, cand_ids)
        ro = pl.multiple_of(hd * kk, kk)
        for sub in subs:
            r0 = sub * n
            cv, cidx = ctop[sub]
            ia = jnp.floor(cidx * (1.0 / kk))
            ib = cidx - ia * kk
            e1 = _select_rows(top[2 * sub][1], ia)
            e2 = _select_rows(top[2 * sub + 1][1], ib)
            ex = jnp.exp(cv - jnp.max(cv, axis=0, keepdims=True))
            gates = ex / jnp.sum(ex, axis=0, keepdims=True)
            e1_scr[pl.ds(ro, kk), r0:r0 + n] = e1
            e2_scr[pl.ds(ro, kk), r0:r0 + n] = e2
            gt_scr[pl.ds(ro, kk), r0:r0 + n] = gates
        return 0

    lax.fori_loop(0, PEER_HEADS, head_body, 0)

    e1_ref[...] = e1_scr[...].T
    e2_ref[...] = e2_scr[...].T
    gt_ref[...] = gt_scr[...].T


def peer_route(h2, wq, sub_keys):
    t, d = h2.shape
    tm = min(t, 256)
    nq = wq.shape[1]
    nj = PEER_HEADS * PEER_TOPK
    out = jax.ShapeDtypeStruct((t, nj), F32)
    ospec = pl.BlockSpec((tm, nj), lambda i: (i, 0))
    return pl.pallas_call(
        _peer_route_kernel,
        grid=(t // tm,),
        in_specs=[pl.BlockSpec((tm, d), lambda i: (i, 0)),
                  pl.BlockSpec((d, nq), lambda i: (0, 0)),
                  pl.BlockSpec((2, N_KEYS, PK_DIM), lambda i: (0, 0, 0))],
        out_specs=[ospec, ospec, ospec],
        out_shape=[out, out, out],
        scratch_shapes=[pltpu.VMEM((2 * PEER_HEADS, tm, PK_DIM), F32),
                        pltpu.VMEM((nj, tm), F32), pltpu.VMEM((nj, tm), F32),
                        pltpu.VMEM((nj, tm), F32)],
        compiler_params=_cparams(("parallel",), VMEM_LIMIT),
        name="peer_route",
    )(h2, wq, sub_keys)


GATE_GRP = 16


def _transpose8(vs):
    sub = _iota2(vs[0].shape, 0)
    vs = list(vs)
    for d in (4, 2, 1):
        keep = (sub & d) == 0
        out = list(vs)
        for i in range(8):
            if i & d:
                continue
            a, b = vs[i], vs[i + d]
            out[i] = jnp.where(keep, a, pltpu.roll(b, d, 0))
            out[i + d] = jnp.where(keep, pltpu.roll(a, 8 - d, 0), b)
        vs = out
    return vs


def _build_gates(e1_ref, e2_ref, gt_ref, g_scr, stage_scr):
    tm = e1_ref.shape[0]
    nj = PEER_HEADS * PEER_TOPK
    riota = _iota2((N_KEYS, nj), 0).astype(F32)

    half = GATE_GRP // 2

    def group(gi, _):
        t0 = pl.multiple_of(gi * GATE_GRP, GATE_GRP)
        for part in range(2):
            toks = [part * half + i for i in range(half)]
            e1 = [e1_ref[pl.ds(t0 + tt, 1), :] for tt in toks]
            e2 = [e2_ref[pl.ds(t0 + tt, 1), :] for tt in toks]
            g = [gt_ref[pl.ds(t0 + tt, 1), :] for tt in toks]
            pt = [jnp.where(riota == e1[i], g[i], 0.0).astype(BF16) for i in range(half)]
            qt = [jnp.where(riota == e2[i], 1.0, 0.0).astype(BF16) for i in range(half)]
            gm = [_dot_nt(pt[i], qt[i]) for i in range(half)]
            for i, tt in enumerate(toks):
                stage_scr[tt * N_KEYS:(tt + 1) * N_KEYS, :] = gm[i]

        def flush(eb, _):
            r0 = pl.multiple_of(eb * 8, 8)
            tiles = [stage_scr[pl.ds(tt * N_KEYS + r0, 8), :] for tt in range(GATE_GRP)]
            lo = _transpose8(tiles[0:8])
            hi = _transpose8(tiles[8:16])
            for r in range(8):
                rows = jnp.concatenate([lo[r], hi[r]], axis=0)
                g_scr[r0 + r, pl.ds(t0, GATE_GRP), :] = rows.astype(BF16)
            return 0

        lax.fori_loop(0, N_KEYS // 8, flush, 0)
        return 0

    lax.fori_loop(0, tm // GATE_GRP, group, 0)


PEER_TE = 1024
PEER_TE_SUB = 512


def _peer_dense_kernel(h_ref, u_ref, v_ref, e1_ref, e2_ref, gt_ref, x_ref, g2_ref, fg_ref, o_ref,
                       g_scr, stage_scr, *, final_norm):
    e = pl.program_id(1)
    ne = pl.num_programs(1)

    @pl.when(e == 0)
    def _():
        o_ref[...] = jnp.zeros_like(o_ref)
        _build_gates(e1_ref, e2_ref, gt_ref, g_scr, stage_scr)

    h = h_ref[...]
    part = None
    per_sub = PEER_TE_SUB // N_KEYS
    for c in range(PEER_TE // PEER_TE_SUB):
        rows = slice(c * PEER_TE_SUB, (c + 1) * PEER_TE_SUB)
        a = _dot_nt(h, u_ref[0, rows, :])
        act = a * (lax.erf(a * (0.5 ** 0.5)) + 1.0) * 0.5
        g0 = e * (PEER_TE // N_KEYS) + c * per_sub
        g = jnp.concatenate([g_scr[g0 + k] for k in range(per_sub)], axis=1)
        hh = (g.astype(F32) * act).astype(BF16)
        p = _dot(hh, v_ref[0, rows, :])
        part = p if part is None else part + p
    o_ref[...] += part

    @pl.when(e == ne - 1)
    def _():
        x = x_ref[...] + g2_ref[0] * o_ref[...]
        if final_norm:
            x = x * lax.rsqrt(jnp.mean(x * x, axis=-1, keepdims=True) + EPS) * fg_ref[...]
        o_ref[...] = x


def peer_dense(h2, u_all, v_all, layer, e1, e2, gt, x, g2, rows_per_batch, final_g, final_norm):
    t, d = x.shape
    tm = _row_tile(t, 512, g2, rows_per_batch)
    te = PEER_TE
    nj = PEER_HEADS * PEER_TOPK
    l = layer
    once = pl.Buffered(1)
    rspec = pl.BlockSpec((tm, nj), lambda i, e: (i, 0), pipeline_mode=once)
    return pl.pallas_call(
        functools.partial(_peer_dense_kernel, final_norm=final_norm),
        grid=(t // tm, N_EXPERTS // te),
        in_specs=[pl.BlockSpec((tm, d), lambda i, e: (i, 0), pipeline_mode=once),
                  pl.BlockSpec((1, te, d), lambda i, e: (l, e, 0)),
                  pl.BlockSpec((1, te, d), lambda i, e: (l, e, 0)),
                  rspec, rspec, rspec,
                  pl.BlockSpec((tm, d), lambda i, e: (i, 0), pipeline_mode=once),
                  _mod_spec(g2, tm, rows_per_batch),
                  pl.BlockSpec((1, d), lambda i, e: (0, 0))],
        out_specs=pl.BlockSpec((tm, d), lambda i, e: (i, 0)),
        out_shape=jax.ShapeDtypeStruct((t, d), F32),
        scratch_shapes=[pltpu.VMEM((N_KEYS, tm, N_KEYS), BF16),
                        pltpu.VMEM((GATE_GRP * N_KEYS, N_KEYS), F32)],
        compiler_params=_cparams(("parallel", "arbitrary"), VMEM_LIMIT),
        name="peer_dense",
    )(h2, u_all, v_all, e1, e2, gt, x, g2, final_g.reshape(1, d))


def _cast_kernel(x_ref, o_ref):
    o_ref[...] = x_ref[...].astype(BF16)


def cast_bf16(x):
    depth, n, d = x.shape
    tn = 1024
    return pl.pallas_call(
        _cast_kernel,
        grid=(depth, n // tn),
        in_specs=[pl.BlockSpec((1, tn, d), lambda l, i: (l, i, 0))],
        out_specs=pl.BlockSpec((1, tn, d), lambda l, i: (l, i, 0)),
        out_shape=jax.ShapeDtypeStruct(x.shape, BF16),
        compiler_params=_cparams(("parallel", "parallel"), VMEM_LIMIT),
        name="cast_bf16",
    )(x)


def _permute_w_in(w):
    d = w.shape[0]
    small = jnp.concatenate([w[:, 2048:2056], w[:, 5128:5144], w[:, 6680:6688]], axis=1)
    pad = jnp.zeros((d, N_PROJ - OFF_SMALL - small.shape[1]), w.dtype)
    return jnp.concatenate([w[:, 0:2048], w[:, 2056:5128], w[:, 5144:6680], small, pad],
                           axis=1).astype(BF16)


def _small_row(vals, off):
    return jnp.zeros((1, 128), F32).at[0, off:off + vals.shape[0]].set(vals.astype(F32))


def _layer_params(l, ada_w, ada_b, norm1_g, norm2_g, w_in, w_out, gdn_conv_w, gdn_A_log, gdn_dt_bias,
                  gdn_norm_g, sb_bias, gla_w2, gla_b2, gla_norm_g, ssd_conv_w, ssd_conv_b, ssd_A_log,
                  ssd_dt_bias, ssd_D, ssd_norm_g, peer_w_query, peer_sub_keys, peer_u, peer_v):
    del ada_w, ada_b
    w2pad = jnp.zeros((128, HC * DKC), F32).at[SM_G:SM_G + GLA_RANK, :].set(gla_w2[l])
    return {
        "norm1_g": norm1_g[l], "norm2_g": norm2_g[l],
        "w_in": _permute_w_in(w_in[l]), "w_out": w_out[l].astype(BF16),
        "gdn_conv_w": gdn_conv_w[l], "gdn_negA": -jnp.exp(gdn_A_log[l]), "gdn_dt_bias": gdn_dt_bias[l],
        "gdn_gcoef": _small_row(-jnp.exp(gdn_A_log[l]), SM_A),
        "gdn_dtb": _small_row(gdn_dt_bias[l], SM_A),
        "gdn_norm_g": gdn_norm_g[l], "sb_bias": sb_bias[l],
        "w2pad": w2pad, "gla_b2": gla_b2[l], "gla_norm_g": gla_norm_g[l],
        "ssd_conv_w": ssd_conv_w[l], "ssd_conv_b": ssd_conv_b[l],
        "ssd_negA": -jnp.exp(ssd_A_log[l]), "ssd_dt_bias": ssd_dt_bias[l],
        "ssd_arow": _small_row(-jnp.exp(ssd_A_log[l]), SM_DT),
        "ssd_dtb": _small_row(ssd_dt_bias[l], SM_DT),
        "ssd_D": ssd_D[l], "ssd_norm_g": ssd_norm_g[l],
        "wq": peer_w_query[l].astype(BF16), "sub_keys": peer_sub_keys[l],
        "peer_u": peer_u, "peer_v": peer_v, "layer": l,
    }


def _split_mod(mod):
    return [mod[:, i * D_MODEL:(i + 1) * D_MODEL] for i in range(6)]


def _peer_block(h2, x, g2, prm, rows_per_batch):
    e1, e2, gt = peer_route(h2, prm["wq"], prm["sub_keys"])
    return peer_dense(h2, prm["peer_u"], prm["peer_v"], prm["layer"], e1, e2, gt, x, g2, rows_per_batch,
                      prm["final_g"], prm["layer"] == DEPTH - 1)


def _peer_block_padded(h2, x, g2, prm):
    t = x.shape[0]
    tpad = -(-t // ROUTE_SUB) * ROUTE_SUB
    pad = lambda a: jnp.concatenate([a, jnp.zeros((tpad - t,) + a.shape[1:], a.dtype)], axis=0)
    g2p = pad(g2.reshape(t, -1)).reshape(1, tpad, -1)
    return _peer_block(pad(h2), pad(x), g2p, prm, 1)[:t]


def kernel(x_prompt, x_sample, cache_k, cache_v, state_gdn, state_gdn_conv, state_gla, state_ssd, state_ssd_conv, page_table, c_prompt, c_sample, ada_w, ada_b, norm1_g, norm2_g, w_in, w_out, gdn_conv_w, gdn_A_log, gdn_dt_bias, gdn_norm_g, sb_bias, gla_w2, gla_b2, gla_norm_g, ssd_conv_w, ssd_conv_b, ssd_A_log, ssd_dt_bias, ssd_D, ssd_norm_g, peer_w_query, peer_sub_keys, peer_u, peer_v, final_norm_g):
    bsz, seq, d = x_prompt.shape
    nb = x_sample.shape[0]
    tp = bsz * seq
    n_pool = cache_k.shape[1]
    cache_k4 = cache_k.reshape(DEPTH, n_pool, PAGE_SIZE * HB, DB)
    cache_v4 = cache_v.reshape(DEPTH, n_pool, PAGE_SIZE * HB, DB)

    n_c = bsz + nb
    r_pad = -(-n_c // 8) * 8
    c_all = jnp.concatenate([c_prompt, c_sample, jnp.zeros((r_pad - n_c, d), F32)], axis=0)
    mod = ada_mod(c_all, ada_w, ada_b)

    peer_u = cast_bf16(peer_u)
    peer_v = cast_bf16(peer_v)
    xp = x_prompt.reshape(tp, d)
    xs = x_sample.reshape(nb, d)
    outs_p, outs_s = [], []
    for l in range(DEPTH):
        prm = _layer_params(l, ada_w, ada_b, norm1_g, norm2_g, w_in, w_out, gdn_conv_w, gdn_A_log,
                            gdn_dt_bias, gdn_norm_g, sb_bias, gla_w2, gla_b2, gla_norm_g, ssd_conv_w,
                            ssd_conv_b, ssd_A_log, ssd_dt_bias, ssd_D, ssd_norm_g, peer_w_query,
                            peer_sub_keys, peer_u, peer_v)
        prm["final_g"] = final_norm_g
        mp = [m.reshape(bsz, 1, d) for m in _split_mod(mod[l, 0:bsz])]
        ms = [m.reshape(1, nb, d) for m in _split_mod(mod[l, bsz:bsz + nb])]

        proj = in_proj(xp, prm["norm1_g"], mp[1], mp[0], prm["w_in"], seq)
        oa, gdn_s = gdn_prompt(proj, bsz, seq, prm["gdn_conv_w"], prm["gdn_gcoef"], prm["gdn_dtb"],
                               prm["gdn_norm_g"])
        ob = sb_prompt(proj, bsz, seq, prm["sb_bias"])
        oc, gla_s = gla_prompt(proj, bsz, seq, prm["w2pad"], prm["gla_b2"], prm["gla_norm_g"])
        od, ssd_s = ssd_prompt(proj, bsz, seq, prm["ssd_conv_w"], prm["ssd_conv_b"], prm["ssd_arow"],
                               prm["ssd_dtb"], prm["ssd_D"], prm["ssd_norm_g"])
        p3 = proj.reshape(bsz, seq, N_PROJ)
        outs_p.append((p3[:, :, OFF_KB:OFF_KB + HB * DB].reshape(bsz, seq, HB, DB),
                       p3[:, :, OFF_VB:OFF_VB + HB * DB].reshape(bsz, seq, HB, DB),
                       gdn_s, p3[:, seq - (CONV_W - 1):, OFF_QKVA:OFF_QKVA + GDN_CONV_DIM],
                       gla_s, ssd_s, p3[:, seq - (CONV_W - 1):, OFF_XBC:OFF_XBC + SSD_CONV_DIM]))
        xp, h2 = out_proj(oa, ob, oc, od, prm["w_out"], xp, mp[2], prm["norm2_g"], mp[4], mp[3], seq)
        xp = _peer_block(h2, xp, mp[5], prm, seq)

        proj_s = in_proj(xs, prm["norm1_g"], ms[1], ms[0], prm["w_in"], 1)
        ps3 = proj_s.reshape(nb, 1, N_PROJ)
        (oa_s, oc_s, od_s, gdn_n, gbuf_n, gla_n, ssd_n, sbuf_n) = sample_mixers(
            ps3, l, state_gdn, state_gdn_conv, state_gla, state_ssd, state_ssd_conv, prm)
        ob_s = sample_attn(ps3, l, cache_k4, cache_v4, page_table, prm["sb_bias"])
        outs_s.append((proj_s[:, OFF_KB:OFF_KB + HB * DB].reshape(nb, 1, HB, DB),
                       proj_s[:, OFF_VB:OFF_VB + HB * DB].reshape(nb, 1, HB, DB),
                       gdn_n, gbuf_n, gla_n, ssd_n, sbuf_n))
        xs, h2s = out_proj(oa_s.reshape(nb, -1), ob_s.reshape(nb, -1), oc_s.reshape(nb, -1),
                           od_s.reshape(nb, -1), prm["w_out"], xs, ms[2], prm["norm2_g"], ms[4], ms[3], 1)
        xs = _peer_block_padded(h2s, xs, ms[5], prm)

    y_prompt = xp.reshape(bsz, seq, d)
    y_sample = xs.reshape(nb, 1, d)
    stk = lambda lst, i: jnp.stack([s[i] for s in lst], axis=0)
    return (y_prompt, y_sample, stk(outs_p, 0), stk(outs_p, 1), stk(outs_s, 0), stk(outs_s, 1),
            stk(outs_p, 2), stk(outs_s, 2), stk(outs_p, 3), stk(outs_s, 3), stk(outs_p, 4), stk(outs_s, 4),
            stk(outs_p, 5), stk(outs_s, 5), stk(outs_p, 6), stk(outs_s, 6))
```

```python
import functools
import math

import jax
import jax.numpy as jnp
import numpy as np
from jax import lax
from jax.experimental import pallas as pl
from jax.experimental.pallas import tpu as pltpu

F32 = jnp.float32
BF16 = jnp.bfloat16
HIGHEST = lax.Precision.HIGHEST

D_MODEL = 2048
DEPTH = 2
PAGE_SIZE = 128
GROUP_WIDTH = D_MODEL // 4
HA, DKA, DVA = 4, 128, 128
HB, DB = 4, 128
HC, DKC, DVC = 4, 64, 128
GLA_RANK = 16
GLA_TAU = 16.0
HD, PD, NG, NSTATE = 8, 64, 2, 128
CONV_W = 4
CHUNK = 64
SB_BLOCK = 128
PEER_HEADS = 8
N_KEYS = 128
N_EXPERTS = N_KEYS * N_KEYS
PK_DIM = 128
PEER_TOPK = 16
EPS = 1e-6
GDN_CONV_DIM = 2 * HA * DKA + HA * DVA
SSD_CONV_DIM = HD * PD + 2 * NG * NSTATE

OFF_QKVA = 0
OFF_ZA = 1536
OFF_QB = 2048
OFF_KB = 2560
OFF_VB = 3072
OFF_QC = 3584
OFF_KC = 3840
OFF_VC = 4096
OFF_RC = 4608
OFF_XBC = 5120
OFF_ZD = 6144
OFF_SMALL = 6656
SM_A, SM_B, SM_G, SM_DT = 0, 4, 8, 24
N_PROJ = 7168

VMEM_LIMIT = 56 * 1024 * 1024


def _cparams(sem, vmem=None):
    return pltpu.CompilerParams(dimension_semantics=sem, vmem_limit_bytes=vmem)


def _softplus(x):
    return jnp.maximum(x, 0.0) + jnp.log1p(jnp.exp(-jnp.abs(x)))


def _softplus_log(x):
    return jnp.maximum(x, 0.0) + jnp.log(1.0 + jnp.exp(-jnp.abs(x)))


def _silu(x):
    return x * jax.nn.sigmoid(x)


def _dotb(a, b):
    return _dot(a.astype(BF16), b.astype(BF16))


def _dotb_nt(a, b):
    return _dot_nt(a.astype(BF16), b.astype(BF16))


def _split2(x):
    hi = x.astype(BF16)
    return hi, (x - hi.astype(F32)).astype(BF16)


def _dot3(a, b):
    ah, al = _split2(a)
    bh, bl = _split2(b)
    return _dot(ah, bh) + (_dot(ah, bl) + _dot(al, bh))


def _dot(a, b, precision=None):
    return jnp.dot(a, b, precision=precision, preferred_element_type=F32)


def _dot_nt(a, b, precision=None):
    return lax.dot_general(a, b, (((1,), (1,)), ((), ())), precision=precision,
                           preferred_element_type=F32)


def _split3(x):
    hi = x.astype(BF16)
    r1 = x - hi.astype(F32)
    mid = r1.astype(BF16)
    lo = (r1 - mid.astype(F32)).astype(BF16)
    return hi, mid, lo


def _tri_dot_left(tri_bf16, x):
    hi, mid, lo = _split3(x)
    return (_dot(tri_bf16, hi) + _dot(tri_bf16, mid)) + _dot(tri_bf16, lo)


def _tri_dot_right(x, tri_bf16):
    hi, mid, lo = _split3(x)
    return (_dot(hi, tri_bf16) + _dot(mid, tri_bf16)) + _dot(lo, tri_bf16)


def _iota2(shape, dim):
    return lax.broadcasted_iota(jnp.int32, shape, dim)


def _col_from_row(r):
    n = r.shape[1]
    eye = _iota2((n, n), 0) == _iota2((n, n), 1)
    return jnp.sum(jnp.where(eye, jnp.broadcast_to(r, (n, n)), 0.0), axis=1, keepdims=True)


def _ada_kernel(c_ref, w_ref, b_ref, o_ref):
    c = c_ref[...]
    o_ref[0] = _dot(_silu(c), w_ref[0]) + b_ref[0]


def ada_mod(c_all, ada_w, ada_b):
    r = c_all.shape[0]
    n = ada_w.shape[2]
    tn = 1024
    return pl.pallas_call(
        _ada_kernel,
        grid=(DEPTH, n // tn),
        in_specs=[pl.BlockSpec((r, D_MODEL), lambda l, j: (0, 0)),
                  pl.BlockSpec((1, D_MODEL, tn), lambda l, j: (l, 0, j)),
                  pl.BlockSpec((1, 1, tn), lambda l, j: (l, 0, j))],
        out_specs=pl.BlockSpec((1, r, tn), lambda l, j: (l, 0, j)),
        out_shape=jax.ShapeDtypeStruct((DEPTH, r, n), F32),
        compiler_params=_cparams(("parallel", "parallel"), VMEM_LIMIT),
        name="ada_mod",
    )(c_all, ada_w, ada_b.reshape(DEPTH, 1, n))


def _in_proj_kernel(x_ref, g_ref, sc_ref, sh_ref, w_ref, o_ref, h_scr):
    @pl.when(pl.program_id(1) == 0)
    def _():
        x = x_ref[...]
        y = x * lax.rsqrt(jnp.mean(x * x, axis=-1, keepdims=True) + EPS) * g_ref[...]
        h_scr[...] = (y * (1.0 + sc_ref[0]) + sh_ref[0]).astype(BF16)

    o_ref[...] = _dot(h_scr[...], w_ref[...])


def _row_tile(t, cap, mod, rows_per_batch):
    return min(t, cap, rows_per_batch) if mod.shape[1] == 1 else min(t, cap)


def _mod_spec(mod, tm, rows_per_batch):
    nb, r, d = mod.shape
    if r == 1:
        return pl.BlockSpec((1, 1, d), lambda i, *_: ((i * tm) // rows_per_batch, 0, 0))
    return pl.BlockSpec((1, r, d), lambda i, *_: (0, 0, 0))


def in_proj(x, g, sc, sh, w, rows_per_batch):
    t, d = x.shape
    n = w.shape[1]
    tm = _row_tile(t, 1024, sc, rows_per_batch)
    tn = 1024
    return pl.pallas_call(
        _in_proj_kernel,
        grid=(t // tm, n // tn),
        in_specs=[pl.BlockSpec((tm, d), lambda i, j: (i, 0)),
                  pl.BlockSpec((1, d), lambda i, j: (0, 0)),
                  _mod_spec(sc, tm, rows_per_batch),
                  _mod_spec(sh, tm, rows_per_batch),
                  pl.BlockSpec((d, tn), lambda i, j: (0, j))],
        out_specs=pl.BlockSpec((tm, tn), lambda i, j: (i, j)),
        out_shape=jax.ShapeDtypeStruct((t, n), F32),
        scratch_shapes=[pltpu.VMEM((tm, d), BF16)],
        compiler_params=_cparams(("parallel", "arbitrary"), VMEM_LIMIT),
        name="in_proj",
    )(x, g.reshape(1, d), sc, sh, w)


def _causal_conv_chunk(u, prev_ref, w_ref):
    c = u.shape[0]
    rows = _iota2(u.shape, 0)
    prev = prev_ref[...]
    out = u * w_ref[CONV_W - 1:CONV_W, :]
    for k in range(1, CONV_W):
        shifted = jnp.where(rows >= k, pltpu.roll(u, k, 0), pltpu.roll(prev, k, 0))
        out = out + shifted * w_ref[CONV_W - 1 - k:CONV_W - k, :]
    prev_ref[...] = u
    del c
    return out


def _inv_unit_lower_many(ms):
    c = ms[0].shape[0]
    n = range(len(ms))
    eye = (_iota2((c, c), 0) == _iota2((c, c), 1)).astype(F32)
    x = [-m for m in ms]
    p = [eye + x[i] for i in n]
    steps = int(math.ceil(math.log2(c))) - 1
    xs = [_split2(x[i]) for i in n]
    for _ in range(steps):
        x = [_dot(xs[i][0], xs[i][0]) + (_dot(xs[i][0], xs[i][1]) + _dot(xs[i][1], xs[i][0])) for i in n]
        xs = [_split2(x[i]) for i in n]
        ps = [_split2(p[i]) for i in n]
        p = [p[i] + (_dot(ps[i][0], xs[i][0]) + (_dot(ps[i][0], xs[i][1]) + _dot(ps[i][1], xs[i][0])))
             for i in n]
    return p


def _run_sequences(body, seq_refs, shared_refs, s_out_ref, scratch):
    ci = pl.program_id(1)
    nc = pl.num_programs(1)

    @pl.when(ci == 0)
    def _():
        for s in scratch:
            s[...] = jnp.zeros_like(s)

    nseq = s_out_ref.shape[0]
    body([tuple(r.at[0, sq] for r in seq_refs) for sq in range(nseq)], shared_refs,
         [tuple(s.at[sq] for s in scratch) for sq in range(nseq)])

    @pl.when(ci == nc - 1)
    def _():
        s_out_ref[...] = scratch[0][...]


def _gdn_prompt_kernel(qkv_ref, z_ref, sm_ref, cw_ref, gcoef_ref, dtb_ref, ng_ref,
                       o_ref, s_out_ref, s_scr, prev_scr):
    _run_sequences(_gdn_chunk, (qkv_ref, z_ref, sm_ref, o_ref), (cw_ref, gcoef_ref, dtb_ref, ng_ref),
                   s_out_ref, (s_scr, prev_scr))


def _gdn_chunk(seqs, shared, scr):
    cw_ref, gcoef_ref, dtb_ref, ng_ref = shared
    c = CHUNK
    ri = _iota2((c, c), 0)
    cj = _iota2((c, c), 1)
    incl = cj <= ri
    strict = cj < ri
    tril = incl.astype(BF16)
    q, k, v, gc, gr, bc, g_last, zs, outs, states = [], [], [], [], [], [], [], [], [], []
    for (qkv_ref, z_ref, sm_ref, o_ref), (s_scr, prev_scr) in zip(seqs, scr):
        x = _silu(_causal_conv_chunk(qkv_ref[...], prev_scr, cw_ref))
        sm = sm_ref[:, 0:128]
        g = gcoef_ref[...] * _softplus(sm + dtb_ref[...])
        beta = jax.nn.sigmoid(sm)
        gcum = _tri_dot_left(tril, g)
        gcum_t = gcum.T
        z = z_ref[...]
        for h in range(HA):
            q.append(x[:, h * DKA:(h + 1) * DKA])
            k.append(x[:, HA * DKA + h * DKA:HA * DKA + (h + 1) * DKA])
            v.append(x[:, 2 * HA * DKA + h * DVA:2 * HA * DKA + (h + 1) * DVA])
            gc.append(gcum[:, SM_A + h:SM_A + h + 1])
            gr.append(gcum_t[SM_A + h:SM_A + h + 1, :])
            bc.append(beta[:, SM_B + h:SM_B + h + 1])
            g_last.append(gcum[c - 1:c, SM_A + h:SM_A + h + 1])
            zs.append(z[:, h * DVA:(h + 1) * DVA])
            outs.append((o_ref, h))
            states.append((s_scr, h))
    n = range(len(q))
    q = [q[i] * lax.rsqrt(jnp.sum(q[i] * q[i], axis=-1, keepdims=True) + EPS) * (DKA ** -0.5) for i in n]
    k = [k[i] * lax.rsqrt(jnp.sum(k[i] * k[i], axis=-1, keepdims=True) + EPS) for i in n]
    decay = [jnp.exp(jnp.where(incl, gc[i] - gr[i], -jnp.inf)) for i in n]
    eg = [jnp.exp(gc[i]) for i in n]
    kb = [k[i].astype(BF16) for i in n]
    kk = [_dot_nt(kb[i], kb[i]) for i in n]
    m = [jnp.where(strict, decay[i], 0.0) * kk[i] * bc[i] for i in n]
    tinv = _inv_unit_lower_many(m)
    tb = [tinv[i].astype(BF16) for i in n]
    w = [_dot(tb[i], ((bc[i] * eg[i]) * k[i]).astype(BF16)) for i in n]
    u = [_dot(tb[i], (bc[i] * v[i]).astype(BF16)) for i in n]
    s = [ref[h] for ref, h in states]
    sb = [s[i].astype(BF16) for i in n]
    u = [u[i] - _dot(w[i].astype(BF16), sb[i]) for i in n]
    ub = [u[i].astype(BF16) for i in n]
    attn = [_dot_nt(q[i].astype(BF16), kb[i]) * decay[i] for i in n]
    o = [_dot((q[i] * eg[i]).astype(BF16), sb[i]) + _dot(attn[i].astype(BF16), ub[i]) for i in n]
    kd = [(k[i] * jnp.exp(g_last[i] - gc[i])).T.astype(BF16) for i in n]
    s_new = [jnp.exp(g_last[i]) * s[i] + _dot(kd[i], ub[i]) for i in n]
    for i in n:
        ref, h = states[i]
        ref[h] = s_new[i]
    for i in n:
        on = o[i] * lax.rsqrt(jnp.mean(o[i] * o[i], axis=-1, keepdims=True) + EPS) * ng_ref[...]
        ref, h = outs[i]
        ref[:, h * DVA:(h + 1) * DVA] = (on * _silu(zs[i])).astype(BF16)


def _seqs_per_step(bsz):
    return 4 if bsz % 4 == 0 else (2 if bsz % 2 == 0 else 1)


def _seq_spec(nseq, width, col_block):
    return pl.BlockSpec((1, nseq, CHUNK, width), lambda b, i: (b, 0, i, col_block))


def _fixed_spec(shape):
    return pl.BlockSpec(shape, lambda b, i: (0,) * len(shape))


def gdn_prompt(proj, bsz, seq, conv_w, gcoef, dtb, norm_g):
    nc = seq // CHUNK
    c = CHUNK
    ns = _seqs_per_step(bsz)
    proj4 = proj.reshape(bsz // ns, ns, seq, N_PROJ)
    o, s = pl.pallas_call(
        _gdn_prompt_kernel,
        grid=(bsz // ns, nc),
        in_specs=[_seq_spec(ns, GDN_CONV_DIM, OFF_QKVA // GDN_CONV_DIM),
                  _seq_spec(ns, 512, OFF_ZA // 512),
                  _seq_spec(ns, 512, OFF_SMALL // 512),
                  _fixed_spec((CONV_W, GDN_CONV_DIM)), _fixed_spec((1, 128)), _fixed_spec((1, 128)),
                  _fixed_spec((1, DVA))],
        out_specs=[_seq_spec(ns, HA * DVA, 0),
                   pl.BlockSpec((ns, HA, DKA, DVA), lambda b, i: (b, 0, 0, 0))],
        out_shape=[jax.ShapeDtypeStruct((bsz // ns, ns, seq, HA * DVA), BF16),
                   jax.ShapeDtypeStruct((bsz, HA, DKA, DVA), F32)],
        scratch_shapes=[pltpu.VMEM((ns, HA, DKA, DVA), F32), pltpu.VMEM((ns, c, GDN_CONV_DIM), F32)],
        compiler_params=_cparams(("parallel", "arbitrary")),
        name="gdn_prompt",
    )(proj4, proj4, proj4, conv_w, gcoef, dtb, norm_g.reshape(1, DVA))
    return o.reshape(bsz * seq, HA * DVA), s


SB_TQ = 512
SB_TK = 256


def _sb_prompt_kernel(bias_ref, q_ref, k_ref, v_ref, o_ref):
    h = pl.program_id(1)
    qi = pl.program_id(2)
    tq = q_ref.shape[0]
    tk = min(SB_TK, tq)
    bias = bias_ref[h]
    q = q_ref[...].astype(BF16)
    ri = _iota2((tq, tk), 0)
    cj = _iota2((tq, tk), 1)
    tri = (_iota2((tk, tk), 0) > _iota2((tk, tk), 1)).astype(BF16)
    nkb = (qi + 1) * (tq // tk)

    def make_body(masked):
        def body(jj, carry):
            acc, run = carry
            j = nkb - 1 - jj
            off = pl.multiple_of(j * tk, tk)
            kb = k_ref[pl.ds(off, tk), :].astype(BF16)
            vb = v_ref[pl.ds(off, tk), :].astype(BF16)
            z = _dot_nt(q, kb) * (DB ** -0.5) + bias
            sp = _softplus_log(z)
            lf = -sp
            if masked:
                mask = (j * tk + cj) < (qi * tq + ri)
                lf = jnp.where(mask, lf, 0.0)
            hi, lo = _split2(lf)
            after = (_dot(hi, tri) + _dot(lo, tri)) + run
            a = jnp.exp((z - sp) + after)
            if masked:
                a = jnp.where(mask, a, 0.0)
            acc = acc + _dot(a.astype(BF16), vb)
            run = run + jnp.sum(lf, axis=-1, keepdims=True)
            return acc, run
        return body

    ndiag = tq // tk
    carry = lax.fori_loop(0, ndiag, make_body(True),
                          (jnp.zeros((tq, DB), F32), jnp.zeros((tq, 1), F32)))
    acc, _ = lax.fori_loop(ndiag, nkb, make_body(False), carry)
    o_ref[...] = acc.astype(BF16)


def sb_prompt(proj, bsz, seq, bias):
    tq = min(SB_TQ, seq)
    nq = seq // tq
    return pl.pallas_call(
        _sb_prompt_kernel,
        grid=(bsz, HB, nq),
        in_specs=[pl.BlockSpec(memory_space=pltpu.SMEM),
                  pl.BlockSpec((tq, DB), lambda b, h, i: (b * nq + i, OFF_QB // DB + h)),
                  pl.BlockSpec((seq, DB), lambda b, h, i: (b, OFF_KB // DB + h)),
                  pl.BlockSpec((seq, DB), lambda b, h, i: (b, OFF_VB // DB + h))],
        out_specs=pl.BlockSpec((tq, DB), lambda b, h, i: (b * nq + i, h)),
        out_shape=jax.ShapeDtypeStruct((bsz * seq, HB * DB), BF16),
        compiler_params=_cparams(("parallel", "parallel", "arbitrary")),
        name="sb_prompt",
    )(bias, proj, proj, proj)


GLA_SUB = 16


def _gla_prompt_kernel(q_ref, k_ref, v_ref, r_ref, sm_ref, w2_ref, b2_ref, ng_ref,
                       o_ref, s_out_ref, s_scr):
    _run_sequences(_gla_chunk, (q_ref, k_ref, v_ref, r_ref, sm_ref, o_ref), (w2_ref, b2_ref, ng_ref),
                   s_out_ref, (s_scr,))


def _gla_chunk(seqs, shared, scr):
    w2_ref, b2_ref, ng_ref = shared
    c = CHUNK
    ri = _iota2((c, c), 0)
    cj = _iota2((c, c), 1)
    incl = cj <= ri
    tril = incl.astype(BF16)
    jrow = _iota2((c, DKC), 0)
    q, k, v, r, bc, b_last_col, outs, states = [], [], [], [], [], [], [], []
    for (q_ref, k_ref, v_ref, r_ref, sm_ref, o_ref), (s_scr,) in zip(seqs, scr):
        sm = sm_ref[:, 0:128]
        pre = _dot3(sm, w2_ref[...]) + b2_ref[...]
        loga = -_softplus(-pre) * (1.0 / GLA_TAU)
        bcum = _tri_dot_left(tril, loga)
        bcum_t = bcum.T
        qa, ka, va, ra = q_ref[...], k_ref[...], v_ref[...], r_ref[...]
        for h in range(HC):
            q.append(qa[:, h * DKC:(h + 1) * DKC] * (DKC ** -0.5))
            k.append(ka[:, h * DKC:(h + 1) * DKC])
            v.append(va[:, h * DVC:(h + 1) * DVC])
            r.append(ra[:, h * DVC:(h + 1) * DVC])
            bc.append(bcum[:, h * DKC:(h + 1) * DKC])
            b_last_col.append(bcum_t[h * DKC:(h + 1) * DKC, c - 1:c])
            outs.append((o_ref, h))
            states.append((s_scr, h))
    n = range(len(q))
    rows = [[] for _ in n]
    for sb in range(c // GLA_SUB):
        i0 = sb * GLA_SUB
        ref = [bc[i][i0:i0 + 1, :] for i in n]
        qe = [(q[i][i0:i0 + GLA_SUB] * jnp.exp(bc[i][i0:i0 + GLA_SUB] - ref[i])).astype(BF16) for i in n]
        ke = [(k[i] * jnp.exp(jnp.where(jrow < i0 + GLA_SUB, ref[i] - bc[i], 0.0))).astype(BF16) for i in n]
        for i in n:
            rows[i].append(_dot_nt(qe[i], ke[i]))
    attn = [jnp.where(incl, jnp.concatenate(rows[i], axis=0), 0.0).astype(BF16) for i in n]
    s = [ref_[h] for ref_, h in states]
    vb = [v[i].astype(BF16) for i in n]
    o = [_dot((q[i] * jnp.exp(bc[i])).astype(BF16), s[i].astype(BF16)) + _dot(attn[i], vb[i]) for i in n]
    kd = [(k[i] * jnp.exp(bc[i][c - 1:c, :] - bc[i])).T.astype(BF16) for i in n]
    s_new = [jnp.exp(b_last_col[i]) * s[i] + _dot(kd[i], vb[i]) for i in n]
    for i in n:
        ref_, h = states[i]
        ref_[h] = s_new[i]
    for i in n:
        on = o[i] * lax.rsqrt(jnp.mean(o[i] * o[i], axis=-1, keepdims=True) + EPS) * ng_ref[...]
        ref_, h = outs[i]
        ref_[:, h * DVC:(h + 1) * DVC] = (on * _silu(r[i])).astype(BF16)


def gla_prompt(proj, bsz, seq, w2pad, b2, norm_g):
    nc = seq // CHUNK
    ns = _seqs_per_step(bsz)
    proj4 = proj.reshape(bsz // ns, ns, seq, N_PROJ)
    o, s = pl.pallas_call(
        _gla_prompt_kernel,
        grid=(bsz // ns, nc),
        in_specs=[_seq_spec(ns, HC * DKC, OFF_QC // (HC * DKC)),
                  _seq_spec(ns, HC * DKC, OFF_KC // (HC * DKC)),
                  _seq_spec(ns, HC * DVC, OFF_VC // (HC * DVC)),
                  _seq_spec(ns, HC * DVC, OFF_RC // (HC * DVC)),
                  _seq_spec(ns, 512, OFF_SMALL // 512),
                  _fixed_spec((128, HC * DKC)), _fixed_spec((1, HC * DKC)), _fixed_spec((1, DVC))],
        out_specs=[_seq_spec(ns, HC * DVC, 0),
                   pl.BlockSpec((ns, HC, DKC, DVC), lambda b, i: (b, 0, 0, 0))],
        out_shape=[jax.ShapeDtypeStruct((bsz // ns, ns, seq, HC * DVC), BF16),
                   jax.ShapeDtypeStruct((bsz, HC, DKC, DVC), F32)],
        scratch_shapes=[pltpu.VMEM((ns, HC, DKC, DVC), F32)],
        compiler_params=_cparams(("parallel", "arbitrary")),
        name="gla_prompt",
    )(proj4, proj4, proj4, proj4, proj4, w2pad, b2.reshape(1, HC * DKC), norm_g.reshape(1, DVC))
    return o.reshape(bsz * seq, HC * DVC), s


def _ssd_prompt_kernel(dvec_ref, xbc_ref, z_ref, sm_ref, cw_ref, cb_ref, arow_ref, dtb_ref, ng_ref,
                       o_ref, s_out_ref, s_scr, prev_scr):
    _run_sequences(_ssd_chunk, (xbc_ref, z_ref, sm_ref, o_ref),
                   (dvec_ref, cw_ref, cb_ref, arow_ref, dtb_ref, ng_ref), s_out_ref, (s_scr, prev_scr))


def _ssd_chunk(seqs, shared, scr):
    dvec_ref, cw_ref, cb_ref, arow_ref, dtb_ref, ng_ref = shared
    c = CHUNK
    ri = _iota2((c, c), 0)
    cj = _iota2((c, c), 1)
    incl = cj <= ri
    tril = incl.astype(BF16)
    hpg = HD // NG
    xh, zh, bgb, cg, cbg, li, lj, dti, dtj, l_last, dcoef, states = ([] for _ in range(12))
    for (xbc_ref, z_ref, sm_ref, o_ref), (s_scr, prev_scr) in zip(seqs, scr):
        xbc = _silu(_causal_conv_chunk(xbc_ref[...], prev_scr, cw_ref) + cb_ref[...])
        sm = sm_ref[:, 0:128]
        dt = _softplus(sm + dtb_ref[...])
        lcum = _tri_dot_left(tril, dt * arow_ref[...])
        lcum_t = lcum.T
        dt_t = dt.T
        z = z_ref[...]
        grp = []
        for g in range(NG):
            b_g = xbc[:, HD * PD + g * NSTATE:HD * PD + (g + 1) * NSTATE]
            c_g = xbc[:, HD * PD + NG * NSTATE + g * NSTATE:HD * PD + NG * NSTATE + (g + 1) * NSTATE]
            b_gb = b_g.astype(BF16)
            grp.append((b_gb, c_g, _dot_nt(c_g.astype(BF16), b_gb)))
        for h in range(HD):
            b_gb, c_g, cb_g = grp[h // hpg]
            lane = SM_DT + h
            xh.append(xbc[:, h * PD:(h + 1) * PD])
            zh.append(z[:, h * PD:(h + 1) * PD])
            bgb.append(b_gb)
            cg.append(c_g)
            cbg.append(cb_g)
            li.append(lcum[:, lane:lane + 1])
            lj.append(lcum_t[lane:lane + 1, :])
            dti.append(dt[:, lane:lane + 1])
            dtj.append(dt_t[lane:lane + 1, :])
            l_last.append(lcum[c - 1:c, lane:lane + 1])
            dcoef.append(dvec_ref[h])
            states.append((s_scr, h))
    n = range(len(xh))
    scores = [(cbg[i] * jnp.exp(jnp.where(incl, li[i] - lj[i], -jnp.inf)) * dtj[i]).astype(BF16) for i in n]
    s = [ref[h] for ref, h in states]
    xb = [xh[i].astype(BF16) for i in n]
    y = [_dot(scores[i], xb[i]) + _dot_nt((cg[i] * jnp.exp(li[i])).astype(BF16), s[i].astype(BF16))
         for i in n]
    xs = [(xh[i] * (dti[i] * jnp.exp(l_last[i] - li[i]))).T.astype(BF16) for i in n]
    s_new = [jnp.exp(l_last[i]) * s[i] + _dot(xs[i], bgb[i]) for i in n]
    for i in n:
        ref, h = states[i]
        ref[h] = s_new[i]
    ys = [(y[i] + dcoef[i] * xh[i]) * _silu(zh[i]) for i in n]
    gw = HD * PD // NG
    for sq, (_, _, _, o_ref) in enumerate(seqs):
        for g in range(NG):
            first = sq * HD + g * hpg
            yg = jnp.concatenate(ys[first:first + hpg], axis=-1)
            yn = yg * lax.rsqrt(jnp.mean(yg * yg, axis=-1, keepdims=True) + EPS)
            o_ref[:, g * gw:(g + 1) * gw] = (yn * ng_ref[:, g * gw:(g + 1) * gw]).astype(BF16)


def ssd_prompt(proj, bsz, seq, conv_w, conv_b, arow, dtb, dvec, norm_g):
    nc = seq // CHUNK
    c = CHUNK
    ns = _seqs_per_step(bsz)
    proj4 = proj.reshape(bsz // ns, ns, seq, N_PROJ)
    o, s = pl.pallas_call(
        _ssd_prompt_kernel,
        grid=(bsz // ns, nc),
        in_specs=[pl.BlockSpec(memory_space=pltpu.SMEM),
                  _seq_spec(ns, SSD_CONV_DIM, OFF_XBC // SSD_CONV_DIM),
                  _seq_spec(ns, 512, OFF_ZD // 512),
                  _seq_spec(ns, 512, OFF_SMALL // 512),
                  _fixed_spec((CONV_W, SSD_CONV_DIM)), _fixed_spec((1, SSD_CONV_DIM)),
                  _fixed_spec((1, 128)), _fixed_spec((1, 128)), _fixed_spec((1, HD * PD))],
        out_specs=[_seq_spec(ns, HD * PD, 0),
                   pl.BlockSpec((ns, HD, PD, NSTATE), lambda b, i: (b, 0, 0, 0))],
        out_shape=[jax.ShapeDtypeStruct((bsz // ns, ns, seq, HD * PD), BF16),
                   jax.ShapeDtypeStruct((bsz, HD, PD, NSTATE), F32)],
        scratch_shapes=[pltpu.VMEM((ns, HD, PD, NSTATE), F32), pltpu.VMEM((ns, c, SSD_CONV_DIM), F32)],
        compiler_params=_cparams(("parallel", "arbitrary")),
        name="ssd_prompt",
    )(dvec, proj4, proj4, proj4, conv_w, conv_b.reshape(1, SSD_CONV_DIM), arow, dtb,
      norm_g.reshape(1, HD * PD))
    return o.reshape(bsz * seq, HD * PD), s


def _row8(r):
    return jnp.concatenate([r, jnp.zeros((7, r.shape[1]), F32)], axis=0)


def _sample_mixers_kernel(gneg_ref, gdtb_ref, aneg_ref, sdtb_ref, dvec_ref,
                          p_ref, gs_ref, gbuf_ref, ls_ref, ss_ref, sbuf_ref,
                          gcw_ref, gng_ref, w2_ref, b2_ref, lng_ref, scw_ref, scb_ref, sng_ref,
                          oa_ref, oc_ref, od_ref, gs_out, gbuf_out, ls_out, ss_out, sbuf_out):
    l = 0
    sm = p_ref[0, :, OFF_SMALL:OFF_SMALL + 128]

    u = p_ref[0, :, OFF_QKVA:OFF_QKVA + GDN_CONV_DIM]
    buf = gbuf_ref[0, 0]
    conv = (buf[0:1] * gcw_ref[0:1, :] + buf[1:2] * gcw_ref[1:2, :]
            + buf[2:3] * gcw_ref[2:3, :] + u * gcw_ref[3:4, :])
    gbuf_out[0, 0:2, :] = buf[1:3]
    gbuf_out[0, 2:3, :] = u
    x = _silu(conv)
    za = p_ref[0, :, OFF_ZA:OFF_ZA + HA * DVA]
    ha = range(HA)
    q = [x[:, h * DKA:(h + 1) * DKA] for h in ha]
    k = [x[:, HA * DKA + h * DKA:HA * DKA + (h + 1) * DKA] for h in ha]
    v = [x[:, 2 * HA * DKA + h * DVA:2 * HA * DKA + (h + 1) * DVA] for h in ha]
    q = [q[h] * lax.rsqrt(jnp.sum(q[h] * q[h], axis=-1, keepdims=True) + EPS) * (DKA ** -0.5) for h in ha]
    k = [k[h] * lax.rsqrt(jnp.sum(k[h] * k[h], axis=-1, keepdims=True) + EPS) for h in ha]
    g = [gneg_ref[h] * _softplus(sm[:, SM_A + h:SM_A + h + 1] + gdtb_ref[h]) for h in ha]
    b = [jax.nn.sigmoid(sm[:, SM_B + h:SM_B + h + 1]) for h in ha]
    eg = [jnp.exp(g[h]) for h in ha]
    s = [gs_ref[0, 0, h] for h in ha]
    lhs = [jnp.concatenate([k[h], q[h] * eg[h], jnp.zeros((6, DKA), F32)], axis=0) for h in ha]
    kq = [_dot(lhs[h], s[h], HIGHEST) for h in ha]
    uu = [b[h] * v[h] - (b[h] * eg[h]) * kq[h][0:1] for h in ha]
    o = [kq[h][1:2] + jnp.sum(q[h] * k[h], axis=-1, keepdims=True) * uu[h] for h in ha]
    kcol = [_col_from_row(k[h]) for h in ha]
    for h in ha:
        gs_out[0, h] = eg[h] * s[h] + kcol[h] * uu[h]
    for h in ha:
        on = o[h] * lax.rsqrt(jnp.mean(o[h] * o[h], axis=-1, keepdims=True) + EPS) * gng_ref[...]
        oa_ref[0, :, h * DVA:(h + 1) * DVA] = (on * _silu(za[:, h * DVA:(h + 1) * DVA])).astype(BF16)

    pre = _dot(jnp.broadcast_to(sm, (8, 128)), w2_ref[...], HIGHEST)[0:1] + b2_ref[...]
    loga = -_softplus(-pre) * (1.0 / GLA_TAU)
    qc = p_ref[0, :, OFF_QC:OFF_QC + HC * DKC]
    kc = p_ref[0, :, OFF_KC:OFF_KC + HC * DKC]
    vc = p_ref[0, :, OFF_VC:OFF_VC + HC * DVC]
    rc = p_ref[0, :, OFF_RC:OFF_RC + HC * DVC]
    hc = range(HC)
    q = [qc[:, h * DKC:(h + 1) * DKC] * (DKC ** -0.5) for h in hc]
    k = [kc[:, h * DKC:(h + 1) * DKC] for h in hc]
    v = [vc[:, h * DVC:(h + 1) * DVC] for h in hc]
    ea = [jnp.exp(loga[:, h * DKC:(h + 1) * DKC]) for h in hc]
    s = [ls_ref[0, 0, h] for h in hc]
    qs = [_dot(_row8(q[h] * ea[h]), s[h], HIGHEST)[0:1] for h in hc]
    o = [qs[h] + jnp.sum(q[h] * k[h], axis=-1, keepdims=True) * v[h] for h in hc]
    eacol = [_col_from_row(ea[h]) for h in hc]
    kcol = [_col_from_row(k[h]) for h in hc]
    for h in hc:
        ls_out[0, h] = eacol[h] * s[h] + kcol[h] * v[h]
    for h in hc:
        on = o[h] * lax.rsqrt(jnp.mean(o[h] * o[h], axis=-1, keepdims=True) + EPS) * lng_ref[...]
        oc_ref[0, :, h * DVC:(h + 1) * DVC] = (on * _silu(rc[:, h * DVC:(h + 1) * DVC])).astype(BF16)

    us = p_ref[0, :, OFF_XBC:OFF_XBC + SSD_CONV_DIM]
    sbuf = sbuf_ref[0, 0]
    sconv = (sbuf[0:1] * scw_ref[0:1, :] + sbuf[1:2] * scw_ref[1:2, :]
             + sbuf[2:3] * scw_ref[2:3, :] + us * scw_ref[3:4, :])
    sbuf_out[0, 0:2, :] = sbuf[1:3]
    sbuf_out[0, 2:3, :] = us
    xbc = _silu(sconv + scb_ref[...])
    zd = p_ref[0, :, OFF_ZD:OFF_ZD + HD * PD]
    hd = range(HD)
    grp = [h // (HD // NG) for h in hd]
    bg = [xbc[:, HD * PD + grp[h] * NSTATE:HD * PD + (grp[h] + 1) * NSTATE] for h in hd]
    cg = [xbc[:, HD * PD + NG * NSTATE + grp[h] * NSTATE:HD * PD + NG * NSTATE + (grp[h] + 1) * NSTATE]
          for h in hd]
    xh = [xbc[:, h * PD:(h + 1) * PD] for h in hd]
    dt = [_softplus(sm[:, SM_DT + h:SM_DT + h + 1] + sdtb_ref[h]) for h in hd]
    el = [jnp.exp(dt[h] * aneg_ref[h]) for h in hd]
    s = [ss_ref[0, 0, h] for h in hd]
    score = [jnp.sum(cg[h] * bg[h], axis=-1, keepdims=True) * dt[h] for h in hd]
    cs = [_dot_nt(_row8(cg[h] * el[h]), s[h], HIGHEST)[0:1] for h in hd]
    xcol = [_col_from_row(xh[h] * dt[h]) for h in hd]
    for h in hd:
        ss_out[0, h] = el[h] * s[h] + xcol[h] * bg[h]
    ys = [(score[h] * xh[h] + cs[h] + dvec_ref[h] * xh[h]) * _silu(zd[:, h * PD:(h + 1) * PD]) for h in hd]
    gw = HD * PD // NG
    for g in range(NG):
        yg = jnp.concatenate(ys[g * (HD // NG):(g + 1) * (HD // NG)], axis=-1)
        yn = yg * lax.rsqrt(jnp.mean(yg * yg, axis=-1, keepdims=True) + EPS)
        od_ref[0, :, g * gw:(g + 1) * gw] = (yn * sng_ref[:, g * gw:(g + 1) * gw]).astype(BF16)
    del l


def sample_mixers(proj3, layer, state_gdn, state_gdn_conv, state_gla, state_ssd, state_ssd_conv, prm):
    nb = proj3.shape[0]
    smem = pl.BlockSpec(memory_space=pltpu.SMEM)
    full = lambda shape: pl.BlockSpec(shape, lambda b: (0,) * len(shape))
    l = layer
    outs = pl.pallas_call(
        _sample_mixers_kernel,
        grid=(nb,),
        in_specs=[smem, smem, smem, smem, smem,
                  pl.BlockSpec((1, 1, N_PROJ), lambda b: (b, 0, 0)),
                  pl.BlockSpec((1, 1, HA, DKA, DVA), lambda b: (l, b, 0, 0, 0)),
                  pl.BlockSpec((1, 1, CONV_W - 1, GDN_CONV_DIM), lambda b: (l, b, 0, 0)),
                  pl.BlockSpec((1, 1, HC, DKC, DVC), lambda b: (l, b, 0, 0, 0)),
                  pl.BlockSpec((1, 1, HD, PD, NSTATE), lambda b: (l, b, 0, 0, 0)),
                  pl.BlockSpec((1, 1, CONV_W - 1, SSD_CONV_DIM), lambda b: (l, b, 0, 0)),
                  full((CONV_W, GDN_CONV_DIM)), full((1, DVA)),
                  full((128, HC * DKC)), full((1, HC * DKC)), full((1, DVC)),
                  full((CONV_W, SSD_CONV_DIM)), full((1, SSD_CONV_DIM)), full((1, HD * PD))],
        out_specs=[pl.BlockSpec((1, 1, HA * DVA), lambda b: (b, 0, 0)),
                   pl.BlockSpec((1, 1, HC * DVC), lambda b: (b, 0, 0)),
                   pl.BlockSpec((1, 1, HD * PD), lambda b: (b, 0, 0)),
                   pl.BlockSpec((1, HA, DKA, DVA), lambda b: (b, 0, 0, 0)),
                   pl.BlockSpec((1, CONV_W - 1, GDN_CONV_DIM), lambda b: (b, 0, 0)),
                   pl.BlockSpec((1, HC, DKC, DVC), lambda b: (b, 0, 0, 0)),
                   pl.BlockSpec((1, HD, PD, NSTATE), lambda b: (b, 0, 0, 0)),
                   pl.BlockSpec((1, CONV_W - 1, SSD_CONV_DIM), lambda b: (b, 0, 0))],
        out_shape=[jax.ShapeDtypeStruct((nb, 1, HA * DVA), BF16),
                   jax.ShapeDtypeStruct((nb, 1, HC * DVC), BF16),
                   jax.ShapeDtypeStruct((nb, 1, HD * PD), BF16),
                   jax.ShapeDtypeStruct((nb, HA, DKA, DVA), F32),
                   jax.ShapeDtypeStruct((nb, CONV_W - 1, GDN_CONV_DIM), F32),
                   jax.ShapeDtypeStruct((nb, HC, DKC, DVC), F32),
                   jax.ShapeDtypeStruct((nb, HD, PD, NSTATE), F32),
                   jax.ShapeDtypeStruct((nb, CONV_W - 1, SSD_CONV_DIM), F32)],
        compiler_params=_cparams(("parallel",)),
        name="sample_mixers",
    )(prm["gdn_negA"], prm["gdn_dt_bias"], prm["ssd_negA"], prm["ssd_dt_bias"], prm["ssd_D"],
      proj3, state_gdn, state_gdn_conv, state_gla, state_ssd, state_ssd_conv,
      prm["gdn_conv_w"], prm["gdn_norm_g"].reshape(1, DVA),
      prm["w2pad"], prm["gla_b2"].reshape(1, HC * DKC), prm["gla_norm_g"].reshape(1, DVC),
      prm["ssd_conv_w"], prm["ssd_conv_b"].reshape(1, SSD_CONV_DIM),
      prm["ssd_norm_g"].reshape(1, HD * PD))
    return outs


SA_PAGES = 32


def _sample_attn_kernel(pt_ref, q_ref, brow_ref, *refs):
    k_refs = refs[0:SA_PAGES]
    v_refs = refs[SA_PAGES:2 * SA_PAGES]
    o_ref, acc_scr, run_scr = refs[2 * SA_PAGES:]
    j = pl.program_id(1)
    nj = pl.num_programs(1)

    @pl.when(j == 0)
    def _():
        acc_scr[...] = jnp.zeros_like(acc_scr)
        run_scr[...] = jnp.zeros_like(run_scr)

    w = PAGE_SIZE * HB
    q = q_ref[0]
    q8 = jnp.concatenate([q[:, h * DB:(h + 1) * DB] for h in range(HB)]
                         + [jnp.zeros((8 - HB, DB), F32)], axis=0).astype(BF16)
    rows = _iota2((8, w), 0)
    lanes = _iota2((8, w), 1)
    sel = (lanes % HB) == rows
    zs = []
    for p in range(SA_PAGES):
        kp = k_refs[p][0, 0].astype(BF16)
        zz = _dot_nt(q8, kp)
        zs.append(jnp.sum(jnp.where(sel, zz, 0.0), axis=0, keepdims=True))
    z = jnp.concatenate(zs, axis=0) * (DB ** -0.5) + brow_ref[...]
    sp = _softplus_log(z)
    lf = -sp
    plane = _iota2((SA_PAGES, w), 1)
    suf = jnp.where(plane < w - HB, pltpu.roll(lf, w - HB, 1), 0.0)
    tot = lf
    step = HB
    while step < w:
        suf = suf + jnp.where(plane < w - step, pltpu.roll(suf, w - step, 1), 0.0)
        tot = tot + pltpu.roll(tot, step, 1)
        step *= 2
    run = run_scr[0:1, :]
    runs = []
    for p in range(SA_PAGES):
        runs.append(run)
        run = run + tot[p:p + 1, :]
    run_scr[...] = jnp.broadcast_to(run, run_scr.shape)
    a = jnp.exp((z - sp) + (suf + jnp.concatenate(runs, axis=0)))
    acc = acc_scr[...]
    for p in range(SA_PAGES):
        vp = v_refs[p][0, 0].astype(BF16)
        ap = jnp.where(sel, jnp.broadcast_to(a[p:p + 1, :], (8, w)), 0.0).astype(BF16)
        acc = acc + _dot(ap, vp)
    acc_scr[...] = acc

    @pl.when(j == nj - 1)
    def _():
        o_ref[0] = acc[0:HB].astype(BF16)


def sample_attn(proj3, layer, cache_k4, cache_v4, page_table, bias):
    nb, npg = page_table.shape
    l = layer
    nj = npg // SA_PAGES

    def page_spec(p):
        return pl.BlockSpec((1, 1, PAGE_SIZE * HB, DB),
                            lambda b, j, pt: (l, pt[b, npg - 1 - (j * SA_PAGES + p)], 0, 0))

    grid_spec = pltpu.PrefetchScalarGridSpec(
        num_scalar_prefetch=1,
        grid=(nb, nj),
        in_specs=([pl.BlockSpec((1, 1, HB * DB), lambda b, j, pt: (b, 0, OFF_QB // (HB * DB))),
                   pl.BlockSpec((1, PAGE_SIZE * HB), lambda b, j, pt: (0, 0))]
                  + [page_spec(p) for p in range(SA_PAGES)]
                  + [page_spec(p) for p in range(SA_PAGES)]),
        out_specs=pl.BlockSpec((1, HB, DB), lambda b, j, pt: (b, 0, 0)),
        scratch_shapes=[pltpu.VMEM((8, DB), F32), pltpu.VMEM((8, PAGE_SIZE * HB), F32)],
    )
    brow = jnp.tile(bias.astype(F32), PAGE_SIZE).reshape(1, PAGE_SIZE * HB)
    return pl.pallas_call(
        _sample_attn_kernel,
        grid_spec=grid_spec,
        out_shape=jax.ShapeDtypeStruct((nb, HB, DB), BF16),
        compiler_params=_cparams(("parallel", "arbitrary")),
        name="sample_attn",
    )(page_table, proj3, brow, *([cache_k4] * SA_PAGES), *([cache_v4] * SA_PAGES))


def _out_proj_kernel(a_ref, b_ref, c_ref, d_ref, w_ref, x_ref, g1_ref, n2_ref, sc_ref, sh_ref,
                     xo_ref, h_ref):
    gw = GROUP_WIDTH
    acc = _dot(a_ref[...], w_ref[0:gw, :])
    acc = acc + _dot(b_ref[...], w_ref[gw:2 * gw, :])
    acc = acc + _dot(c_ref[...], w_ref[2 * gw:3 * gw, :])
    acc = acc + _dot(d_ref[...], w_ref[3 * gw:4 * gw, :])
    x = x_ref[...] + g1_ref[0] * acc
    xo_ref[...] = x
    y = x * lax.rsqrt(jnp.mean(x * x, axis=-1, keepdims=True) + EPS) * n2_ref[...]
    h_ref[...] = (y * (1.0 + sc_ref[0]) + sh_ref[0]).astype(BF16)


def out_proj(oa, ob, oc, od, w_out, x, g1, n2, sc2, sh2, rows_per_batch):
    t, d = x.shape
    tm = _row_tile(t, 512, g1, rows_per_batch)
    part = pl.BlockSpec((tm, GROUP_WIDTH), lambda i: (i, 0))
    return pl.pallas_call(
        _out_proj_kernel,
        grid=(t // tm,),
        in_specs=[part, part, part, part,
                  pl.BlockSpec((d, d), lambda i: (0, 0)),
                  pl.BlockSpec((tm, d), lambda i: (i, 0)),
                  _mod_spec(g1, tm, rows_per_batch),
                  pl.BlockSpec((1, d), lambda i: (0, 0)),
                  _mod_spec(sc2, tm, rows_per_batch),
                  _mod_spec(sh2, tm, rows_per_batch)],
        out_specs=[pl.BlockSpec((tm, d), lambda i: (i, 0)),
                   pl.BlockSpec((tm, d), lambda i: (i, 0))],
        out_shape=[jax.ShapeDtypeStruct((t, d), F32), jax.ShapeDtypeStruct((t, d), BF16)],
        compiler_params=_cparams(("parallel",), VMEM_LIMIT),
        name="out_proj",
    )(oa, ob, oc, od, w_out, x, g1, n2.reshape(1, d), sc2, sh2)


ROUTE_SUB = 128


def _topk_rows_many(vals, k, ids=None):
    if ids is None:
        ids = _iota2(vals[0].shape, 0).astype(F32)
    n = range(len(vals))
    out_v = [[] for _ in n]
    out_i = [[] for _ in n]
    for _ in range(k):
        m = [jnp.max(vals[i], axis=0, keepdims=True) for i in n]
        idx = [jnp.min(jnp.where(vals[i] == m[i], ids, 1e9), axis=0, keepdims=True) for i in n]
        vals = [jnp.where(ids == idx[i], -jnp.inf, vals[i]) for i in n]
        for i in n:
            out_v[i].append(m[i])
            out_i[i].append(idx[i])
    return [(jnp.concatenate(out_v[i], axis=0), jnp.concatenate(out_i[i], axis=0)) for i in n]


_CAND_ROWS = [(a, PEER_TOPK // (a + 1)) for a in range(PEER_TOPK)]
_N_CAND = sum(nb for _, nb in _CAND_ROWS)
_N_CAND_PAD = -(-_N_CAND // 8) * 8


def _cand_ids(n):
    r = _iota2((_N_CAND_PAD, n), 0)
    ids = jnp.full((_N_CAND_PAD, n), 1e9, F32)
    start = 0
    for a, nb in _CAND_ROWS:
        ids = jnp.where((r >= start) & (r < start + nb), (a * PEER_TOPK + r - start).astype(F32), ids)
        start += nb
    return ids


def _select_rows(table, sel):
    out = jnp.zeros(sel.shape, F32)
    for a in range(table.shape[0]):
        out = jnp.where(sel == float(a), table[a:a + 1, :], out)
    return out


def _peer_route_kernel(h_ref, wq_ref, sk_ref, e1_ref, e2_ref, gt_ref, q_scr, e1_scr, e2_scr, gt_scr):
    tm = h_ref.shape[0]
    q = _dot(h_ref[...], wq_ref[...])
    for cgrp in range(2 * PEER_HEADS):
        q_scr[cgrp] = q[:, cgrp * PK_DIM:(cgrp + 1) * PK_DIM]
    sk0 = sk_ref[0]
    sk1 = sk_ref[1]
    kk = PEER_TOPK

    n = min(tm, ROUTE_SUB)
    cand_ids = _cand_ids(n)
    subs = range(tm // n)

    def head_body(hd, _):
        s = []
        for sub in subs:
            r0 = sub * n
            s.append(_dot_nt(sk0, q_scr[2 * hd, r0:r0 + n, :]))
            s.append(_dot_nt(sk1, q_scr[2 * hd + 1, r0:r0 + n, :]))
        top = _topk_rows_many(s, kk)
        cand = []
        for sub in subs:
            sv0, sv1 = top[2 * sub][0], top[2 * sub + 1][0]
            cand.append(jnp.concatenate(
                [sv0[a:a + 1, :] + sv1[0:nb, :] for a, nb in _CAND_ROWS]
                + [jnp.full((_N_CAND_PAD - _N_CAND, n), -jnp.inf, F32)], axis=0))
        ctop = _topk_rows_many(cand, kk, cand_ids)
        ro = pl.multiple_of(hd * kk, kk)
        for sub in subs:
            r0 = sub * n
            cv, cidx = ctop[sub]
            ia = jnp.floor(cidx * (1.0 / kk))
            ib = cidx - ia * kk
            e1 = _select_rows(top[2 * sub][1], ia)
            e2 = _select_rows(top[2 * sub + 1][1], ib)
            ex = jnp.exp(cv - jnp.max(cv, axis=0, keepdims=True))
            gates = ex / jnp.sum(ex, axis=0, keepdims=True)
            e1_scr[pl.ds(ro, kk), r0:r0 + n] = e1
            e2_scr[pl.ds(ro, kk), r0:r0 + n] = e2
            gt_scr[pl.ds(ro, kk), r0:r0 + n] = gates
        return 0

    lax.fori_loop(0, PEER_HEADS, head_body, 0)

    e1_ref[...] = e1_scr[...].T
    e2_ref[...] = e2_scr[...].T
    gt_ref[...] = gt_scr[...].T


def peer_route(h2, wq, sub_keys):
    t, d = h2.shape
    tm = min(t, 256)
    nq = wq.shape[1]
    nj = PEER_HEADS * PEER_TOPK
    out = jax.ShapeDtypeStruct((t, nj), F32)
    ospec = pl.BlockSpec((tm, nj), lambda i: (i, 0))
    return pl.pallas_call(
        _peer_route_kernel,
        grid=(t // tm,),
        in_specs=[pl.BlockSpec((tm, d), lambda i: (i, 0)),
                  pl.BlockSpec((d, nq), lambda i: (0, 0)),
                  pl.BlockSpec((2, N_KEYS, PK_DIM), lambda i: (0, 0, 0))],
        out_specs=[ospec, ospec, ospec],
        out_shape=[out, out, out],
        scratch_shapes=[pltpu.VMEM((2 * PEER_HEADS, tm, PK_DIM), F32),
                        pltpu.VMEM((nj, tm), F32), pltpu.VMEM((nj, tm), F32),
                        pltpu.VMEM((nj, tm), F32)],
        compiler_params=_cparams(("parallel",), VMEM_LIMIT),
        name="peer_route",
    )(h2, wq, sub_keys)


GATE_GRP = 16


def _transpose8(vs):
    sub = _iota2(vs[0].shape, 0)
    vs = list(vs)
    for d in (4, 2, 1):
        keep = (sub & d) == 0
        out = list(vs)
        for i in range(8):
            if i & d:
                continue
            a, b = vs[i], vs[i + d]
            out[i] = jnp.where(keep, a, pltpu.roll(b, d, 0))
            out[i + d] = jnp.where(keep, pltpu.roll(a, 8 - d, 0), b)
        vs = out
    return vs


def _build_gates(e1_ref, e2_ref, gt_ref, g_scr, stage_scr):
    tm = e1_ref.shape[0]
    nj = PEER_HEADS * PEER_TOPK
    riota = _iota2((N_KEYS, nj), 0).astype(F32)

    half = GATE_GRP // 2

    def group(gi, _):
        t0 = pl.multiple_of(gi * GATE_GRP, GATE_GRP)
        for part in range(2):
            toks = [part * half + i for i in range(half)]
            e1 = [e1_ref[pl.ds(t0 + tt, 1), :] for tt in toks]
            e2 = [e2_ref[pl.ds(t0 + tt, 1), :] for tt in toks]
            g = [gt_ref[pl.ds(t0 + tt, 1), :] for tt in toks]
            pt = [jnp.where(riota == e1[i], g[i], 0.0).astype(BF16) for i in range(half)]
            qt = [jnp.where(riota == e2[i], 1.0, 0.0).astype(BF16) for i in range(half)]
            gm = [_dot_nt(pt[i], qt[i]) for i in range(half)]
            for i, tt in enumerate(toks):
                stage_scr[tt * N_KEYS:(tt + 1) * N_KEYS, :] = gm[i]

        def flush(eb, _):
            r0 = pl.multiple_of(eb * 8, 8)
            tiles = [stage_scr[pl.ds(tt * N_KEYS + r0, 8), :] for tt in range(GATE_GRP)]
            lo = _transpose8(tiles[0:8])
            hi = _transpose8(tiles[8:16])
            for r in range(8):
                rows = jnp.concatenate([lo[r], hi[r]], axis=0)
                g_scr[r0 + r, pl.ds(t0, GATE_GRP), :] = rows.astype(BF16)
            return 0

        lax.fori_loop(0, N_KEYS // 8, flush, 0)
        return 0

    lax.fori_loop(0, tm // GATE_GRP, group, 0)


PEER_TE = 1024
PEER_TE_SUB = 512


def _peer_dense_kernel(h_ref, u_ref, v_ref, e1_ref, e2_ref, gt_ref, x_ref, g2_ref, fg_ref, o_ref,
                       g_scr, stage_scr, *, final_norm):
    e = pl.program_id(1)
    ne = pl.num_programs(1)

    @pl.when(e == 0)
    def _():
        o_ref[...] = jnp.zeros_like(o_ref)
        _build_gates(e1_ref, e2_ref, gt_ref, g_scr, stage_scr)

    h = h_ref[...]
    part = None
    per_sub = PEER_TE_SUB // N_KEYS
    for c in range(PEER_TE // PEER_TE_SUB):
        rows = slice(c * PEER_TE_SUB, (c + 1) * PEER_TE_SUB)
        a = _dot_nt(h, u_ref[0, rows, :])
        act = a * (lax.erf(a * (0.5 ** 0.5)) + 1.0) * 0.5
        g0 = e * (PEER_TE // N_KEYS) + c * per_sub
        g = jnp.concatenate([g_scr[g0 + k] for k in range(per_sub)], axis=1)
        hh = (g.astype(F32) * act).astype(BF16)
        p = _dot(hh, v_ref[0, rows, :])
        part = p if part is None else part + p
    o_ref[...] += part

    @pl.when(e == ne - 1)
    def _():
        x = x_ref[...] + g2_ref[0] * o_ref[...]
        if final_norm:
            x = x * lax.rsqrt(jnp.mean(x * x, axis=-1, keepdims=True) + EPS) * fg_ref[...]
        o_ref[...] = x


def peer_dense(h2, u_all, v_all, layer, e1, e2, gt, x, g2, rows_per_batch, final_g, final_norm):
    t, d = x.shape
    tm = _row_tile(t, 512, g2, rows_per_batch)
    te = PEER_TE
    nj = PEER_HEADS * PEER_TOPK
    l = layer
    once = pl.Buffered(1)
    rspec = pl.BlockSpec((tm, nj), lambda i, e: (i, 0), pipeline_mode=once)
    return pl.pallas_call(
        functools.partial(_peer_dense_kernel, final_norm=final_norm),
        grid=(t // tm, N_EXPERTS // te),
        in_specs=[pl.BlockSpec((tm, d), lambda i, e: (i, 0), pipeline_mode=once),
                  pl.BlockSpec((1, te, d), lambda i, e: (l, e, 0)),
                  pl.BlockSpec((1, te, d), lambda i, e: (l, e, 0)),
                  rspec, rspec, rspec,
                  pl.BlockSpec((tm, d), lambda i, e: (i, 0), pipeline_mode=once),
                  _mod_spec(g2, tm, rows_per_batch),
                  pl.BlockSpec((1, d), lambda i, e: (0, 0))],
        out_specs=pl.BlockSpec((tm, d), lambda i, e: (i, 0)),
        out_shape=jax.ShapeDtypeStruct((t, d), F32),
        scratch_shapes=[pltpu.VMEM((N_KEYS, tm, N_KEYS), BF16),
                        pltpu.VMEM((GATE_GRP * N_KEYS, N_KEYS), F32)],
        compiler_params=_cparams(("parallel", "arbitrary"), VMEM_LIMIT),
        name="peer_dense",
    )(h2, u_all, v_all, e1, e2, gt, x, g2, final_g.reshape(1, d))


def _cast_kernel(x_ref, o_ref):
    o_ref[...] = x_ref[...].astype(BF16)


def cast_bf16(x):
    depth, n, d = x.shape
    tn = 1024
    return pl.pallas_call(
        _cast_kernel,
        grid=(depth, n // tn),
        in_specs=[pl.BlockSpec((1, tn, d), lambda l, i: (l, i, 0))],
        out_specs=pl.BlockSpec((1, tn, d), lambda l, i: (l, i, 0)),
        out_shape=jax.ShapeDtypeStruct(x.shape, BF16),
        compiler_params=_cparams(("parallel", "parallel"), VMEM_LIMIT),
        name="cast_bf16",
    )(x)


def _permute_w_in(w):
    d = w.shape[0]
    small = jnp.concatenate([w[:, 2048:2056], w[:, 5128:5144], w[:, 6680:6688]], axis=1)
    pad = jnp.zeros((d, N_PROJ - OFF_SMALL - small.shape[1]), w.dtype)
    return jnp.concatenate([w[:, 0:2048], w[:, 2056:5128], w[:, 5144:6680], small, pad],
                           axis=1).astype(BF16)


def _small_row(vals, off):
    return jnp.zeros((1, 128), F32).at[0, off:off + vals.shape[0]].set(vals.astype(F32))


def _layer_params(l, ada_w, ada_b, norm1_g, norm2_g, w_in, w_out, gdn_conv_w, gdn_A_log, gdn_dt_bias,
                  gdn_norm_g, sb_bias, gla_w2, gla_b2, gla_norm_g, ssd_conv_w, ssd_conv_b, ssd_A_log,
                  ssd_dt_bias, ssd_D, ssd_norm_g, peer_w_query, peer_sub_keys, peer_u, peer_v):
    del ada_w, ada_b
    w2pad = jnp.zeros((128, HC * DKC), F32).at[SM_G:SM_G + GLA_RANK, :].set(gla_w2[l])
    return {
        "norm1_g": norm1_g[l], "norm2_g": norm2_g[l],
        "w_in": _permute_w_in(w_in[l]), "w_out": w_out[l].astype(BF16),
        "gdn_conv_w": gdn_conv_w[l], "gdn_negA": -jnp.exp(gdn_A_log[l]), "gdn_dt_bias": gdn_dt_bias[l],
        "gdn_gcoef": _small_row(-jnp.exp(gdn_A_log[l]), SM_A),
        "gdn_dtb": _small_row(gdn_dt_bias[l], SM_A),
        "gdn_norm_g": gdn_norm_g[l], "sb_bias": sb_bias[l],
        "w2pad": w2pad, "gla_b2": gla_b2[l], "gla_norm_g": gla_norm_g[l],
        "ssd_conv_w": ssd_conv_w[l], "ssd_conv_b": ssd_conv_b[l],
        "ssd_negA": -jnp.exp(ssd_A_log[l]), "ssd_dt_bias": ssd_dt_bias[l],
        "ssd_arow": _small_row(-jnp.exp(ssd_A_log[l]), SM_DT),
        "ssd_dtb": _small_row(ssd_dt_bias[l], SM_DT),
        "ssd_D": ssd_D[l], "ssd_norm_g": ssd_norm_g[l],
        "wq": peer_w_query[l].astype(BF16), "sub_keys": peer_sub_keys[l],
        "peer_u": peer_u, "peer_v": peer_v, "layer": l,
    }


def _split_mod(mod):
    return [mod[:, i * D_MODEL:(i + 1) * D_MODEL] for i in range(6)]


def _peer_block(h2, x, g2, prm, rows_per_batch):
    e1, e2, gt = peer_route(h2, prm["wq"], prm["sub_keys"])
    return peer_dense(h2, prm["peer_u"], prm["peer_v"], prm["layer"], e1, e2, gt, x, g2, rows_per_batch,
                      prm["final_g"], prm["layer"] == DEPTH - 1)


def _peer_block_padded(h2, x, g2, prm):
    t = x.shape[0]
    tpad = -(-t // ROUTE_SUB) * ROUTE_SUB
    pad = lambda a: jnp.concatenate([a, jnp.zeros((tpad - t,) + a.shape[1:], a.dtype)], axis=0)
    g2p = pad(g2.reshape(t, -1)).reshape(1, tpad, -1)
    return _peer_block(pad(h2), pad(x), g2p, prm, 1)[:t]


def kernel(x_prompt, x_sample, cache_k, cache_v, state_gdn, state_gdn_conv, state_gla, state_ssd, state_ssd_conv, page_table, c_prompt, c_sample, ada_w, ada_b, norm1_g, norm2_g, w_in, w_out, gdn_conv_w, gdn_A_log, gdn_dt_bias, gdn_norm_g, sb_bias, gla_w2, gla_b2, gla_norm_g, ssd_conv_w, ssd_conv_b, ssd_A_log, ssd_dt_bias, ssd_D, ssd_norm_g, peer_w_query, peer_sub_keys, peer_u, peer_v, final_norm_g):
    bsz, seq, d = x_prompt.shape
    nb = x_sample.shape[0]
    tp = bsz * seq
    n_pool = cache_k.shape[1]
    cache_k4 = cache_k.reshape(DEPTH, n_pool, PAGE_SIZE * HB, DB)
    cache_v4 = cache_v.reshape(DEPTH, n_pool, PAGE_SIZE * HB, DB)

    n_c = bsz + nb
    r_pad = -(-n_c // 8) * 8
    c_all = jnp.concatenate([c_prompt, c_sample, jnp.zeros((r_pad - n_c, d), F32)], axis=0)
    mod = ada_mod(c_all, ada_w, ada_b)

    peer_u = cast_bf16(peer_u)
    peer_v = cast_bf16(peer_v)
    xp = x_prompt.reshape(tp, d)
    xs = x_sample.reshape(nb, d)
    outs_p, outs_s = [], []
    for l in range(DEPTH):
        prm = _layer_params(l, ada_w, ada_b, norm1_g, norm2_g, w_in, w_out, gdn_conv_w, gdn_A_log,
                            gdn_dt_bias, gdn_norm_g, sb_bias, gla_w2, gla_b2, gla_norm_g, ssd_conv_w,
                            ssd_conv_b, ssd_A_log, ssd_dt_bias, ssd_D, ssd_norm_g, peer_w_query,
                            peer_sub_keys, peer_u, peer_v)
        prm["final_g"] = final_norm_g
        mp = [m.reshape(bsz, 1, d) for m in _split_mod(mod[l, 0:bsz])]
        ms = [m.reshape(1, nb, d) for m in _split_mod(mod[l, bsz:bsz + nb])]

        proj = in_proj(xp, prm["norm1_g"], mp[1], mp[0], prm["w_in"], seq)
        oa, gdn_s = gdn_prompt(proj, bsz, seq, prm["gdn_conv_w"], prm["gdn_gcoef"], prm["gdn_dtb"],
                               prm["gdn_norm_g"])
        ob = sb_prompt(proj, bsz, seq, prm["sb_bias"])
        oc, gla_s = gla_prompt(proj, bsz, seq, prm["w2pad"], prm["gla_b2"], prm["gla_norm_g"])
        od, ssd_s = ssd_prompt(proj, bsz, seq, prm["ssd_conv_w"], prm["ssd_conv_b"], prm["ssd_arow"],
                               prm["ssd_dtb"], prm["ssd_D"], prm["ssd_norm_g"])
        p3 = proj.reshape(bsz, seq, N_PROJ)
        outs_p.append((p3[:, :, OFF_KB:OFF_KB + HB * DB].reshape(bsz, seq, HB, DB),
                       p3[:, :, OFF_VB:OFF_VB + HB * DB].reshape(bsz, seq, HB, DB),
                       gdn_s, p3[:, seq - (CONV_W - 1):, OFF_QKVA:OFF_QKVA + GDN_CONV_DIM],
                       gla_s, ssd_s, p3[:, seq - (CONV_W - 1):, OFF_XBC:OFF_XBC + SSD_CONV_DIM]))
        xp, h2 = out_proj(oa, ob, oc, od, prm["w_out"], xp, mp[2], prm["norm2_g"], mp[4], mp[3], seq)
        xp = _peer_block(h2, xp, mp[5], prm, seq)

        proj_s = in_proj(xs, prm["norm1_g"], ms[1], ms[0], prm["w_in"], 1)
        ps3 = proj_s.reshape(nb, 1, N_PROJ)
        (oa_s, oc_s, od_s, gdn_n, gbuf_n, gla_n, ssd_n, sbuf_n) = sample_mixers(
            ps3, l, state_gdn, state_gdn_conv, state_gla, state_ssd, state_ssd_conv, prm)
        ob_s = sample_attn(ps3, l, cache_k4, cache_v4, page_table, prm["sb_bias"])
        outs_s.append((proj_s[:, OFF_KB:OFF_KB + HB * DB].reshape(nb, 1, HB, DB),
                       proj_s[:, OFF_VB:OFF_VB + HB * DB].reshape(nb, 1, HB, DB),
                       gdn_n, gbuf_n, gla_n, ssd_n, sbuf_n))
        xs, h2s = out_proj(oa_s.reshape(nb, -1), ob_s.reshape(nb, -1), oc_s.reshape(nb, -1),
                           od_s.reshape(nb, -1), prm["w_out"], xs, ms[2], prm["norm2_g"], ms[4], ms[3], 1)
        xs = _peer_block_padded(h2s, xs, ms[5], prm)

    y_prompt = xp.reshape(bsz, seq, d)
    y_sample = xs.reshape(nb, 1, d)
    stk = lambda lst, i: jnp.stack([s[i] for s in lst], axis=0)
    return (y_prompt, y_sample, stk(outs_p, 0), stk(outs_p, 1), stk(outs_s, 0), stk(outs_s, 1),
            stk(outs_p, 2), stk(outs_s, 2), stk(outs_p, 3), stk(outs_s, 3), stk(outs_p, 4), stk(outs_s, 4),
            stk(outs_p, 5), stk(outs_s, 5), stk(outs_p, 6), stk(outs_s, 6))
```

```python
import functools
import math

import jax
import jax.numpy as jnp
import numpy as np
from jax import lax
from jax.experimental import pallas as pl
from jax.experimental.pallas import tpu as pltpu

F32 = jnp.float32
BF16 = jnp.bfloat16
HIGHEST = lax.Precision.HIGHEST

D_MODEL = 2048
DEPTH = 2
PAGE_SIZE = 128
GROUP_WIDTH = D_MODEL // 4
HA, DKA, DVA = 4, 128, 128
HB, DB = 4, 128
HC, DKC, DVC = 4, 64, 128
GLA_RANK = 16
GLA_TAU = 16.0
HD, PD, NG, NSTATE = 8, 64, 2, 128
CONV_W = 4
CHUNK = 64
SB_BLOCK = 128
PEER_HEADS = 8
N_KEYS = 128
N_EXPERTS = N_KEYS * N_KEYS
PK_DIM = 128
PEER_TOPK = 16
EPS = 1e-6
GDN_CONV_DIM = 2 * HA * DKA + HA * DVA
SSD_CONV_DIM = HD * PD + 2 * NG * NSTATE

OFF_QKVA = 0
OFF_ZA = 1536
OFF_QB = 2048
OFF_KB = 2560
OFF_VB = 3072
OFF_QC = 3584
OFF_KC = 3840
OFF_VC = 4096
OFF_RC = 4608
OFF_XBC = 5120
OFF_ZD = 6144
OFF_SMALL = 6656
SM_A, SM_B, SM_G, SM_DT = 0, 4, 8, 24
N_PROJ = 7168

VMEM_LIMIT = 56 * 1024 * 1024


def _cparams(sem, vmem=None):
    return pltpu.CompilerParams(dimension_semantics=sem, vmem_limit_bytes=vmem)


def _softplus(x):
    return jnp.maximum(x, 0.0) + jnp.log1p(jnp.exp(-jnp.abs(x)))


def _softplus_log(x):
    return jnp.maximum(x, 0.0) + jnp.log(1.0 + jnp.exp(-jnp.abs(x)))


def _silu(x):
    return x * jax.nn.sigmoid(x)


def _dotb(a, b):
    return _dot(a.astype(BF16), b.astype(BF16))


def _dotb_nt(a, b):
    return _dot_nt(a.astype(BF16), b.astype(BF16))


def _split2(x):
    hi = x.astype(BF16)
    return hi, (x - hi.astype(F32)).astype(BF16)


def _dot3(a, b):
    ah, al = _split2(a)
    bh, bl = _split2(b)
    return _dot(ah, bh) + (_dot(ah, bl) + _dot(al, bh))


def _dot(a, b, precision=None):
    return jnp.dot(a, b, precision=precision, preferred_element_type=F32)


def _dot_nt(a, b, precision=None):
    return lax.dot_general(a, b, (((1,), (1,)), ((), ())), precision=precision,
                           preferred_element_type=F32)


def _split3(x):
    hi = x.astype(BF16)
    r1 = x - hi.astype(F32)
    mid = r1.astype(BF16)
    lo = (r1 - mid.astype(F32)).astype(BF16)
    return hi, mid, lo


def _tri_dot_left(tri_bf16, x):
    hi, mid, lo = _split3(x)
    return (_dot(tri_bf16, hi) + _dot(tri_bf16, mid)) + _dot(tri_bf16, lo)


def _tri_dot_right(x, tri_bf16):
    hi, mid, lo = _split3(x)
    return (_dot(hi, tri_bf16) + _dot(mid, tri_bf16)) + _dot(lo, tri_bf16)


def _iota2(shape, dim):
    return lax.broadcasted_iota(jnp.int32, shape, dim)


def _col_from_row(r):
    n = r.shape[1]
    eye = _iota2((n, n), 0) == _iota2((n, n), 1)
    return jnp.sum(jnp.where(eye, jnp.broadcast_to(r, (n, n)), 0.0), axis=1, keepdims=True)


def _ada_kernel(c_ref, w_ref, b_ref, o_ref):
    c = c_ref[...]
    o_ref[0] = _dot(_silu(c), w_ref[0]) + b_ref[0]


def ada_mod(c_all, ada_w, ada_b):
    r = c_all.shape[0]
    n = ada_w.shape[2]
    tn = 1024
    return pl.pallas_call(
        _ada_kernel,
        grid=(DEPTH, n // tn),
        in_specs=[pl.BlockSpec((r, D_MODEL), lambda l, j: (0, 0)),
                  pl.BlockSpec((1, D_MODEL, tn), lambda l, j: (l, 0, j)),
                  pl.BlockSpec((1, 1, tn), lambda l, j: (l, 0, j))],
        out_specs=pl.BlockSpec((1, r, tn), lambda l, j: (l, 0, j)),
        out_shape=jax.ShapeDtypeStruct((DEPTH, r, n), F32),
        compiler_params=_cparams(("parallel", "parallel"), VMEM_LIMIT),
        name="ada_mod",
    )(c_all, ada_w, ada_b.reshape(DEPTH, 1, n))


def _in_proj_kernel(x_ref, g_ref, sc_ref, sh_ref, w_ref, o_ref, h_scr):
    @pl.when(pl.program_id(1) == 0)
    def _():
        x = x_ref[...]
        y = x * lax.rsqrt(jnp.mean(x * x, axis=-1, keepdims=True) + EPS) * g_ref[...]
        h_scr[...] = (y * (1.0 + sc_ref[0]) + sh_ref[0]).astype(BF16)

    o_ref[...] = _dot(h_scr[...], w_ref[...])


def _row_tile(t, cap, mod, rows_per_batch):
    return min(t, cap, rows_per_batch) if mod.shape[1] == 1 else min(t, cap)


def _mod_spec(mod, tm, rows_per_batch):
    nb, r, d = mod.shape
    if r == 1:
        return pl.BlockSpec((1, 1, d), lambda i, *_: ((i * tm) // rows_per_batch, 0, 0))
    return pl.BlockSpec((1, r, d), lambda i, *_: (0, 0, 0))


def in_proj(x, g, sc, sh, w, rows_per_batch):
    t, d = x.shape
    n = w.shape[1]
    tm = _row_tile(t, 1024, sc, rows_per_batch)
    tn = 1024
    return pl.pallas_call(
        _in_proj_kernel,
        grid=(t // tm, n // tn),
        in_specs=[pl.BlockSpec((tm, d), lambda i, j: (i, 0)),
                  pl.BlockSpec((1, d), lambda i, j: (0, 0)),
                  _mod_spec(sc, tm, rows_per_batch),
                  _mod_spec(sh, tm, rows_per_batch),
                  pl.BlockSpec((d, tn), lambda i, j: (0, j))],
        out_specs=pl.BlockSpec((tm, tn), lambda i, j: (i, j)),
        out_shape=jax.ShapeDtypeStruct((t, n), F32),
        scratch_shapes=[pltpu.VMEM((tm, d), BF16)],
        compiler_params=_cparams(("parallel", "arbitrary"), VMEM_LIMIT),
        name="in_proj",
    )(x, g.reshape(1, d), sc, sh, w)


def _causal_conv_chunk(u, prev_ref, w_ref):
    c = u.shape[0]
    rows = _iota2(u.shape, 0)
    prev = prev_ref[...]
    out = u * w_ref[CONV_W - 1:CONV_W, :]
    for k in range(1, CONV_W):
        shifted = jnp.where(rows >= k, pltpu.roll(u, k, 0), pltpu.roll(prev, k, 0))
        out = out + shifted * w_ref[CONV_W - 1 - k:CONV_W - k, :]
    prev_ref[...] = u
    del c
    return out


def _inv_unit_lower_many(ms):
    c = ms[0].shape[0]
    n = range(len(ms))
    eye = (_iota2((c, c), 0) == _iota2((c, c), 1)).astype(F32)
    x = [-m for m in ms]
    p = [eye + x[i] for i in n]
    steps = int(math.ceil(math.log2(c))) - 1
    xs = [_split2(x[i]) for i in n]
    for _ in range(steps):
        x = [_dot(xs[i][0], xs[i][0]) + (_dot(xs[i][0], xs[i][1]) + _dot(xs[i][1], xs[i][0])) for i in n]
        xs = [_split2(x[i]) for i in n]
        ps = [_split2(p[i]) for i in n]
        p = [p[i] + (_dot(ps[i][0], xs[i][0]) + (_dot(ps[i][0], xs[i][1]) + _dot(ps[i][1], xs[i][0])))
             for i in n]
    return p


def _run_sequences(body, seq_refs, shared_refs, s_out_ref, scratch):
    ci = pl.program_id(1)
    nc = pl.num_programs(1)

    @pl.when(ci == 0)
    def _():
        for s in scratch:
            s[...] = jnp.zeros_like(s)

    nseq = s_out_ref.shape[0]
    body([tuple(r.at[0, sq] for r in seq_refs) for sq in range(nseq)], shared_refs,
         [tuple(s.at[sq] for s in scratch) for sq in range(nseq)])

    @pl.when(ci == nc - 1)
    def _():
        s_out_ref[...] = scratch[0][...]


def _gdn_prompt_kernel(qkv_ref, z_ref, sm_ref, cw_ref, gcoef_ref, dtb_ref, ng_ref,
                       o_ref, s_out_ref, s_scr, prev_scr):
    _run_sequences(_gdn_chunk, (qkv_ref, z_ref, sm_ref, o_ref), (cw_ref, gcoef_ref, dtb_ref, ng_ref),
                   s_out_ref, (s_scr, prev_scr))


def _gdn_chunk(seqs, shared, scr):
    cw_ref, gcoef_ref, dtb_ref, ng_ref = shared
    c = CHUNK
    ri = _iota2((c, c), 0)
    cj = _iota2((c, c), 1)
    incl = cj <= ri
    strict = cj < ri
    tril = incl.astype(BF16)
    q, k, v, gc, gr, bc, g_last, zs, outs, states = [], [], [], [], [], [], [], [], [], []
    for (qkv_ref, z_ref, sm_ref, o_ref), (s_scr, prev_scr) in zip(seqs, scr):
        x = _silu(_causal_conv_chunk(qkv_ref[...], prev_scr, cw_ref))
        sm = sm_ref[:, 0:128]
        g = gcoef_ref[...] * _softplus(sm + dtb_ref[...])
        beta = jax.nn.sigmoid(sm)
        gcum = _tri_dot_left(tril, g)
        gcum_t = gcum.T
        z = z_ref[...]
        for h in range(HA):
            q.append(x[:, h * DKA:(h + 1) * DKA])
            k.append(x[:, HA * DKA + h * DKA:HA * DKA + (h + 1) * DKA])
            v.append(x[:, 2 * HA * DKA + h * DVA:2 * HA * DKA + (h + 1) * DVA])
            gc.append(gcum[:, SM_A + h:SM_A + h + 1])
            gr.append(gcum_t[SM_A + h:SM_A + h + 1, :])
            bc.append(beta[:, SM_B + h:SM_B + h + 1])
            g_last.append(gcum[c - 1:c, SM_A + h:SM_A + h + 1])
            zs.append(z[:, h * DVA:(h + 1) * DVA])
            outs.append((o_ref, h))
            states.append((s_scr, h))
    n = range(len(q))
    q = [q[i] * lax.rsqrt(jnp.sum(q[i] * q[i], axis=-1, keepdims=True) + EPS) * (DKA ** -0.5) for i in n]
    k = [k[i] * lax.rsqrt(jnp.sum(k[i] * k[i], axis=-1, keepdims=True) + EPS) for i in n]
    decay = [jnp.exp(jnp.where(incl, gc[i] - gr[i], -jnp.inf)) for i in n]
    eg = [jnp.exp(gc[i]) for i in n]
    kb = [k[i].astype(BF16) for i in n]
    kk = [_dot_nt(kb[i], kb[i]) for i in n]
    m = [jnp.where(strict, decay[i], 0.0) * kk[i] * bc[i] for i in n]
    tinv = _inv_unit_lower_many(m)
    tb = [tinv[i].astype(BF16) for i in n]
    w = [_dot(tb[i], ((bc[i] * eg[i]) * k[i]).astype(BF16)) for i in n]
    u = [_dot(tb[i], (bc[i] * v[i]).astype(BF16)) for i in n]
    s = [ref[h] for ref, h in states]
    sb = [s[i].astype(BF16) for i in n]
    u = [u[i] - _dot(w[i].astype(BF16), sb[i]) for i in n]
    ub = [u[i].astype(BF16) for i in n]
    attn = [_dot_nt(q[i].astype(BF16), kb[i]) * decay[i] for i in n]
    o = [_dot((q[i] * eg[i]).astype(BF16), sb[i]) + _dot(attn[i].astype(BF16), ub[i]) for i in n]
    kd = [(k[i] * jnp.exp(g_last[i] - gc[i])).T.astype(BF16) for i in n]
    s_new = [jnp.exp(g_last[i]) * s[i] + _dot(kd[i], ub[i]) for i in n]
    for i in n:
        ref, h = states[i]
        ref[h] = s_new[i]
    for i in n:
        on = o[i] * lax.rsqrt(jnp.mean(o[i] * o[i], axis=-1, keepdims=True) + EPS) * ng_ref[...]
        ref, h = outs[i]
        ref[:, h * DVA:(h + 1) * DVA] = (on * _silu(zs[i])).astype(BF16)


def _seqs_per_step(bsz):
    return 4 if bsz % 4 == 0 else (2 if bsz % 2 == 0 else 1)


def _seq_spec(nseq, width, col_block):
    return pl.BlockSpec((1, nseq, CHUNK, width), lambda b, i: (b, 0, i, col_block))


def _fixed_spec(shape):
    return pl.BlockSpec(shape, lambda b, i: (0,) * len(shape))


def gdn_prompt(proj, bsz, seq, conv_w, gcoef, dtb, norm_g):
    nc = seq // CHUNK
    c = CHUNK
    ns = _seqs_per_step(bsz)
    proj4 = proj.reshape(bsz // ns, ns, seq, N_PROJ)
    o, s = pl.pallas_call(
        _gdn_prompt_kernel,
        grid=(bsz // ns, nc),
        in_specs=[_seq_spec(ns, GDN_CONV_DIM, OFF_QKVA // GDN_CONV_DIM),
                  _seq_spec(ns, 512, OFF_ZA // 512),
                  _seq_spec(ns, 512, OFF_SMALL // 512),
                  _fixed_spec((CONV_W, GDN_CONV_DIM)), _fixed_spec((1, 128)), _fixed_spec((1, 128)),
                  _fixed_spec((1, DVA))],
        out_specs=[_seq_spec(ns, HA * DVA, 0),
                   pl.BlockSpec((ns, HA, DKA, DVA), lambda b, i: (b, 0, 0, 0))],
        out_shape=[jax.ShapeDtypeStruct((bsz // ns, ns, seq, HA * DVA), BF16),
                   jax.ShapeDtypeStruct((bsz, HA, DKA, DVA), F32)],
        scratch_shapes=[pltpu.VMEM((ns, HA, DKA, DVA), F32), pltpu.VMEM((ns, c, GDN_CONV_DIM), F32)],
        compiler_params=_cparams(("parallel", "arbitrary")),
        name="gdn_prompt",
    )(proj4, proj4, proj4, conv_w, gcoef, dtb, norm_g.reshape(1, DVA))
    return o.reshape(bsz * seq, HA * DVA), s


SB_TQ = 512
SB_TK = 256


def _sb_prompt_kernel(bias_ref, q_ref, k_ref, v_ref, o_ref):
    h = pl.program_id(1)
    qi = pl.program_id(2)
    tq = q_ref.shape[0]
    tk = min(SB_TK, tq)
    bias = bias_ref[h]
    q = q_ref[...].astype(BF16)
    ri = _iota2((tq, tk), 0)
    cj = _iota2((tq, tk), 1)
    tri = (_iota2((tk, tk), 0) > _iota2((tk, tk), 1)).astype(BF16)
    nkb = (qi + 1) * (tq // tk)

    def make_body(masked):
        def body(jj, carry):
            acc, run = carry
            j = nkb - 1 - jj
            off = pl.multiple_of(j * tk, tk)
            kb = k_ref[pl.ds(off, tk), :].astype(BF16)
            vb = v_ref[pl.ds(off, tk), :].astype(BF16)
            z = _dot_nt(q, kb) * (DB ** -0.5) + bias
            sp = _softplus_log(z)
            lf = -sp
            if masked:
                mask = (j * tk + cj) < (qi * tq + ri)
                lf = jnp.where(mask, lf, 0.0)
            hi, lo = _split2(lf)
            after = (_dot(hi, tri) + _dot(lo, tri)) + run
            a = jnp.exp((z - sp) + after)
            if masked:
                a = jnp.where(mask, a, 0.0)
            acc = acc + _dot(a.astype(BF16), vb)
            run = run + jnp.sum(lf, axis=-1, keepdims=True)
            return acc, run
        return body

    ndiag = tq // tk
    carry = lax.fori_loop(0, ndiag, make_body(True),
                          (jnp.zeros((tq, DB), F32), jnp.zeros((tq, 1), F32)))
    acc, _ = lax.fori_loop(ndiag, nkb, make_body(False), carry)
    o_ref[...] = acc.astype(BF16)


def sb_prompt(proj, bsz, seq, bias):
    tq = min(SB_TQ, seq)
    nq = seq // tq
    return pl.pallas_call(
        _sb_prompt_kernel,
        grid=(bsz, HB, nq),
        in_specs=[pl.BlockSpec(memory_space=pltpu.SMEM),
                  pl.BlockSpec((tq, DB), lambda b, h, i: (b * nq + i, OFF_QB // DB + h)),
                  pl.BlockSpec((seq, DB), lambda b, h, i: (b, OFF_KB // DB + h)),
                  pl.BlockSpec((seq, DB), lambda b, h, i: (b, OFF_VB // DB + h))],
        out_specs=pl.BlockSpec((tq, DB), lambda b, h, i: (b * nq + i, h)),
        out_shape=jax.ShapeDtypeStruct((bsz * seq, HB * DB), BF16),
        compiler_params=_cparams(("parallel", "parallel", "arbitrary")),
        name="sb_prompt",
    )(bias, proj, proj, proj)


GLA_SUB = 16


def _gla_prompt_kernel(q_ref, k_ref, v_ref, r_ref, sm_ref, w2_ref, b2_ref, ng_ref,
                       o_ref, s_out_ref, s_scr):
    _run_sequences(_gla_chunk, (q_ref, k_ref, v_ref, r_ref, sm_ref, o_ref), (w2_ref, b2_ref, ng_ref),
                   s_out_ref, (s_scr,))


def _gla_chunk(seqs, shared, scr):
    w2_ref, b2_ref, ng_ref = shared
    c = CHUNK
    ri = _iota2((c, c), 0)
    cj = _iota2((c, c), 1)
    incl = cj <= ri
    tril = incl.astype(BF16)
    jrow = _iota2((c, DKC), 0)
    q, k, v, r, bc, b_last_col, outs, states = [], [], [], [], [], [], [], []
    for (q_ref, k_ref, v_ref, r_ref, sm_ref, o_ref), (s_scr,) in zip(seqs, scr):
        sm = sm_ref[:, 0:128]
        pre = _dot3(sm, w2_ref[...]) + b2_ref[...]
        loga = -_softplus(-pre) * (1.0 / GLA_TAU)
        bcum = _tri_dot_left(tril, loga)
        bcum_t = bcum.T
        qa, ka, va, ra = q_ref[...], k_ref[...], v_ref[...], r_ref[...]
        for h in range(HC):
            q.append(qa[:, h * DKC:(h + 1) * DKC] * (DKC ** -0.5))
            k.append(ka[:, h * DKC:(h + 1) * DKC])
            v.append(va[:, h * DVC:(h + 1) * DVC])
            r.append(ra[:, h * DVC:(h + 1) * DVC])
            bc.append(bcum[:, h * DKC:(h + 1) * DKC])
            b_last_col.append(bcum_t[h * DKC:(h + 1) * DKC, c - 1:c])
            outs.append((o_ref, h))
            states.append((s_scr, h))
    n = range(len(q))
    rows = [[] for _ in n]
    for sb in range(c // GLA_SUB):
        i0 = sb * GLA_SUB
        ref = [bc[i][i0:i0 + 1, :] for i in n]
        qe = [(q[i][i0:i0 + GLA_SUB] * jnp.exp(bc[i][i0:i0 + GLA_SUB] - ref[i])).astype(BF16) for i in n]
        ke = [(k[i] * jnp.exp(jnp.where(jrow < i0 + GLA_SUB, ref[i] - bc[i], 0.0))).astype(BF16) for i in n]
        for i in n:
            rows[i].append(_dot_nt(qe[i], ke[i]))
    attn = [jnp.where(incl, jnp.concatenate(rows[i], axis=0), 0.0).astype(BF16) for i in n]
    s = [ref_[h] for ref_, h in states]
    vb = [v[i].astype(BF16) for i in n]
    o = [_dot((q[i] * jnp.exp(bc[i])).astype(BF16), s[i].astype(BF16)) + _dot(attn[i], vb[i]) for i in n]
    kd = [(k[i] * jnp.exp(bc[i][c - 1:c, :] - bc[i])).T.astype(BF16) for i in n]
    s_new = [jnp.exp(b_last_col[i]) * s[i] + _dot(kd[i], vb[i]) for i in n]
    for i in n:
        ref_, h = states[i]
        ref_[h] = s_new[i]
    for i in n:
        on = o[i] * lax.rsqrt(jnp.mean(o[i] * o[i], axis=-1, keepdims=True) + EPS) * ng_ref[...]
        ref_, h = outs[i]
        ref_[:, h * DVC:(h + 1) * DVC] = (on * _silu(r[i])).astype(BF16)


def gla_prompt(proj, bsz, seq, w2pad, b2, norm_g):
    nc = seq // CHUNK
    ns = _seqs_per_step(bsz)
    proj4 = proj.reshape(bsz // ns, ns, seq, N_PROJ)
    o, s = pl.pallas_call(
        _gla_prompt_kernel,
        grid=(bsz // ns, nc),
        in_specs=[_seq_spec(ns, HC * DKC, OFF_QC // (HC * DKC)),
                  _seq_spec(ns, HC * DKC, OFF_KC // (HC * DKC)),
                  _seq_spec(ns, HC * DVC, OFF_VC // (HC * DVC)),
                  _seq_spec(ns, HC * DVC, OFF_RC // (HC * DVC)),
                  _seq_spec(ns, 512, OFF_SMALL // 512),
                  _fixed_spec((128, HC * DKC)), _fixed_spec((1, HC * DKC)), _fixed_spec((1, DVC))],
        out_specs=[_seq_spec(ns, HC * DVC, 0),
                   pl.BlockSpec((ns, HC, DKC, DVC), lambda b, i: (b, 0, 0, 0))],
        out_shape=[jax.ShapeDtypeStruct((bsz // ns, ns, seq, HC * DVC), BF16),
                   jax.ShapeDtypeStruct((bsz, HC, DKC, DVC), F32)],
        scratch_shapes=[pltpu.VMEM((ns, HC, DKC, DVC), F32)],
        compiler_params=_cparams(("parallel", "arbitrary")),
        name="gla_prompt",
    )(proj4, proj4, proj4, proj4, proj4, w2pad, b2.reshape(1, HC * DKC), norm_g.reshape(1, DVC))
    return o.reshape(bsz * seq, HC * DVC), s


def _ssd_prompt_kernel(dvec_ref, xbc_ref, z_ref, sm_ref, cw_ref, cb_ref, arow_ref, dtb_ref, ng_ref,
                       o_ref, s_out_ref, s_scr, prev_scr):
    _run_sequences(_ssd_chunk, (xbc_ref, z_ref, sm_ref, o_ref),
                   (dvec_ref, cw_ref, cb_ref, arow_ref, dtb_ref, ng_ref), s_out_ref, (s_scr, prev_scr))


def _ssd_chunk(seqs, shared, scr):
    dvec_ref, cw_ref, cb_ref, arow_ref, dtb_ref, ng_ref = shared
    c = CHUNK
    ri = _iota2((c, c), 0)
    cj = _iota2((c, c), 1)
    incl = cj <= ri
    tril = incl.astype(BF16)
    hpg = HD // NG
    xh, zh, bgb, cg, cbg, li, lj, dti, dtj, l_last, dcoef, states = ([] for _ in range(12))
    for (xbc_ref, z_ref, sm_ref, o_ref), (s_scr, prev_scr) in zip(seqs, scr):
        xbc = _silu(_causal_conv_chunk(xbc_ref[...], prev_scr, cw_ref) + cb_ref[...])
        sm = sm_ref[:, 0:128]
        dt = _softplus(sm + dtb_ref[...])
        lcum = _tri_dot_left(tril, dt * arow_ref[...])
        lcum_t = lcum.T
        dt_t = dt.T
        z = z_ref[...]
        grp = []
        for g in range(NG):
            b_g = xbc[:, HD * PD + g * NSTATE:HD * PD + (g + 1) * NSTATE]
            c_g = xbc[:, HD * PD + NG * NSTATE + g * NSTATE:HD * PD + NG * NSTATE + (g + 1) * NSTATE]
            b_gb = b_g.astype(BF16)
            grp.append((b_gb, c_g, _dot_nt(c_g.astype(BF16), b_gb)))
        for h in range(HD):
            b_gb, c_g, cb_g = grp[h // hpg]
            lane = SM_DT + h
            xh.append(xbc[:, h * PD:(h + 1) * PD])
            zh.append(z[:, h * PD:(h + 1) * PD])
            bgb.append(b_gb)
            cg.append(c_g)
            cbg.append(cb_g)
            li.append(lcum[:, lane:lane + 1])
            lj.append(lcum_t[lane:lane + 1, :])
            dti.append(dt[:, lane:lane + 1])
            dtj.append(dt_t[lane:lane + 1, :])
            l_last.append(lcum[c - 1:c, lane:lane + 1])
            dcoef.append(dvec_ref[h])
            states.append((s_scr, h))
    n = range(len(xh))
    scores = [(cbg[i] * jnp.exp(jnp.where(incl, li[i] - lj[i], -jnp.inf)) * dtj[i]).astype(BF16) for i in n]
    s = [ref[h] for ref, h in states]
    xb = [xh[i].astype(BF16) for i in n]
    y = [_dot(scores[i], xb[i]) + _dot_nt((cg[i] * jnp.exp(li[i])).astype(BF16), s[i].astype(BF16))
         for i in n]
    xs = [(xh[i] * (dti[i] * jnp.exp(l_last[i] - li[i]))).T.astype(BF16) for i in n]
    s_new = [jnp.exp(l_last[i]) * s[i] + _dot(xs[i], bgb[i]) for i in n]
    for i in n:
        ref, h = states[i]
        ref[h] = s_new[i]
    ys = [(y[i] + dcoef[i] * xh[i]) * _silu(zh[i]) for i in n]
    gw = HD * PD // NG
    for sq, (_, _, _, o_ref) in enumerate(seqs):
        for g in range(NG):
            first = sq * HD + g * hpg
            yg = jnp.concatenate(ys[first:first + hpg], axis=-1)
            yn = yg * lax.rsqrt(jnp.mean(yg * yg, axis=-1, keepdims=True) + EPS)
            o_ref[:, g * gw:(g + 1) * gw] = (yn * ng_ref[:, g * gw:(g + 1) * gw]).astype(BF16)


def ssd_prompt(proj, bsz, seq, conv_w, conv_b, arow, dtb, dvec, norm_g):
    nc = seq // CHUNK
    c = CHUNK
    ns = _seqs_per_step(bsz)
    proj4 = proj.reshape(bsz // ns, ns, seq, N_PROJ)
    o, s = pl.pallas_call(
        _ssd_prompt_kernel,
        grid=(bsz // ns, nc),
        in_specs=[pl.BlockSpec(memory_space=pltpu.SMEM),
                  _seq_spec(ns, SSD_CONV_DIM, OFF_XBC // SSD_CONV_DIM),
                  _seq_spec(ns, 512, OFF_ZD // 512),
                  _seq_spec(ns, 512, OFF_SMALL // 512),
                  _fixed_spec((CONV_W, SSD_CONV_DIM)), _fixed_spec((1, SSD_CONV_DIM)),
                  _fixed_spec((1, 128)), _fixed_spec((1, 128)), _fixed_spec((1, HD * PD))],
        out_specs=[_seq_spec(ns, HD * PD, 0),
                   pl.BlockSpec((ns, HD, PD, NSTATE), lambda b, i: (b, 0, 0, 0))],
        out_shape=[jax.ShapeDtypeStruct((bsz // ns, ns, seq, HD * PD), BF16),
                   jax.ShapeDtypeStruct((bsz, HD, PD, NSTATE), F32)],
        scratch_shapes=[pltpu.VMEM((ns, HD, PD, NSTATE), F32), pltpu.VMEM((ns, c, SSD_CONV_DIM), F32)],
        compiler_params=_cparams(("parallel", "arbitrary")),
        name="ssd_prompt",
    )(dvec, proj4, proj4, proj4, conv_w, conv_b.reshape(1, SSD_CONV_DIM), arow, dtb,
      norm_g.reshape(1, HD * PD))
    return o.reshape(bsz * seq, HD * PD), s


def _row8(r):
    return jnp.concatenate([r, jnp.zeros((7, r.shape[1]), F32)], axis=0)


def _sample_mixers_kernel(gneg_ref, gdtb_ref, aneg_ref, sdtb_ref, dvec_ref,
                          p_ref, gs_ref, gbuf_ref, ls_ref, ss_ref, sbuf_ref,
                          gcw_ref, gng_ref, w2_ref, b2_ref, lng_ref, scw_ref, scb_ref, sng_ref,
                          oa_ref, oc_ref, od_ref, gs_out, gbuf_out, ls_out, ss_out, sbuf_out):
    l = 0
    sm = p_ref[0, :, OFF_SMALL:OFF_SMALL + 128]

    u = p_ref[0, :, OFF_QKVA:OFF_QKVA + GDN_CONV_DIM]
    buf = gbuf_ref[0, 0]
    conv = (buf[0:1] * gcw_ref[0:1, :] + buf[1:2] * gcw_ref[1:2, :]
            + buf[2:3] * gcw_ref[2:3, :] + u * gcw_ref[3:4, :])
    gbuf_out[0, 0:2, :] = buf[1:3]
    gbuf_out[0, 2:3, :] = u
    x = _silu(conv)
    za = p_ref[0, :, OFF_ZA:OFF_ZA + HA * DVA]
    ha = range(HA)
    q = [x[:, h * DKA:(h + 1) * DKA] for h in ha]
    k = [x[:, HA * DKA + h * DKA:HA * DKA + (h + 1) * DKA] for h in ha]
    v = [x[:, 2 * HA * DKA + h * DVA:2 * HA * DKA + (h + 1) * DVA] for h in ha]
    q = [q[h] * lax.rsqrt(jnp.sum(q[h] * q[h], axis=-1, keepdims=True) + EPS) * (DKA ** -0.5) for h in ha]
    k = [k[h] * lax.rsqrt(jnp.sum(k[h] * k[h], axis=-1, keepdims=True) + EPS) for h in ha]
    g = [gneg_ref[h] * _softplus(sm[:, SM_A + h:SM_A + h + 1] + gdtb_ref[h]) for h in ha]
    b = [jax.nn.sigmoid(sm[:, SM_B + h:SM_B + h + 1]) for h in ha]
    eg = [jnp.exp(g[h]) for h in ha]
    s = [gs_ref[0, 0, h] for h in ha]
    lhs = [jnp.concatenate([k[h], q[h] * eg[h], jnp.zeros((6, DKA), F32)], axis=0) for h in ha]
    kq = [_dot(lhs[h], s[h], HIGHEST) for h in ha]
    uu = [b[h] * v[h] - (b[h] * eg[h]) * kq[h][0:1] for h in ha]
    o = [kq[h][1:2] + jnp.sum(q[h] * k[h], axis=-1, keepdims=True) * uu[h] for h in ha]
    kcol = [_col_from_row(k[h]) for h in ha]
    for h in ha:
        gs_out[0, h] = eg[h] * s[h] + kcol[h] * uu[h]
    for h in ha:
        on = o[h] * lax.rsqrt(jnp.mean(o[h] * o[h], axis=-1, keepdims=True) + EPS) * gng_ref[...]
        oa_ref[0, :, h * DVA:(h + 1) * DVA] = (on * _silu(za[:, h * DVA:(h + 1) * DVA])).astype(BF16)

    pre = _dot(jnp.broadcast_to(sm, (8, 128)), w2_ref[...], HIGHEST)[0:1] + b2_ref[...]
    loga = -_softplus(-pre) * (1.0 / GLA_TAU)
    qc = p_ref[0, :, OFF_QC:OFF_QC + HC * DKC]
    kc = p_ref[0, :, OFF_KC:OFF_KC + HC * DKC]
    vc = p_ref[0, :, OFF_VC:OFF_VC + HC * DVC]
    rc = p_ref[0, :, OFF_RC:OFF_RC + HC * DVC]
    hc = range(HC)
    q = [qc[:, h * DKC:(h + 1) * DKC] * (DKC ** -0.5) for h in hc]
    k = [kc[:, h * DKC:(h + 1) * DKC] for h in hc]
    v = [vc[:, h * DVC:(h + 1) * DVC] for h in hc]
    ea = [jnp.exp(loga[:, h * DKC:(h + 1) * DKC]) for h in hc]
    s = [ls_ref[0, 0, h] for h in hc]
    qs = [_dot(_row8(q[h] * ea[h]), s[h], HIGHEST)[0:1] for h in hc]
    o = [qs[h] + jnp.sum(q[h] * k[h], axis=-1, keepdims=True) * v[h] for h in hc]
    eacol = [_col_from_row(ea[h]) for h in hc]
    kcol = [_col_from_row(k[h]) for h in hc]
    for h in hc:
        ls_out[0, h] = eacol[h] * s[h] + kcol[h] * v[h]
    for h in hc:
        on = o[h] * lax.rsqrt(jnp.mean(o[h] * o[h], axis=-1, keepdims=True) + EPS) * lng_ref[...]
        oc_ref[0, :, h * DVC:(h + 1) * DVC] = (on * _silu(rc[:, h * DVC:(h + 1) * DVC])).astype(BF16)

    us = p_ref[0, :, OFF_XBC:OFF_XBC + SSD_CONV_DIM]
    sbuf = sbuf_ref[0, 0]
    sconv = (sbuf[0:1] * scw_ref[0:1, :] + sbuf[1:2] * scw_ref[1:2, :]
             + sbuf[2:3] * scw_ref[2:3, :] + us * scw_ref[3:4, :])
    sbuf_out[0, 0:2, :] = sbuf[1:3]
    sbuf_out[0, 2:3, :] = us
    xbc = _silu(sconv + scb_ref[...])
    zd = p_ref[0, :, OFF_ZD:OFF_ZD + HD * PD]
    hd = range(HD)
    grp = [h // (HD // NG) for h in hd]
    bg = [xbc[:, HD * PD + grp[h] * NSTATE:HD * PD + (grp[h] + 1) * NSTATE] for h in hd]
    cg = [xbc[:, HD * PD + NG * NSTATE + grp[h] * NSTATE:HD * PD + NG * NSTATE + (grp[h] + 1) * NSTATE]
          for h in hd]
    xh = [xbc[:, h * PD:(h + 1) * PD] for h in hd]
    dt = [_softplus(sm[:, SM_DT + h:SM_DT + h + 1] + sdtb_ref[h]) for h in hd]
    el = [jnp.exp(dt[h] * aneg_ref[h]) for h in hd]
    s = [ss_ref[0, 0, h] for h in hd]
    score = [jnp.sum(cg[h] * bg[h], axis=-1, keepdims=True) * dt[h] for h in hd]
    cs = [_dot_nt(_row8(cg[h] * el[h]), s[h], HIGHEST)[0:1] for h in hd]
    xcol = [_col_from_row(xh[h] * dt[h]) for h in hd]
    for h in hd:
        ss_out[0, h] = el[h] * s[h] + xcol[h] * bg[h]
    ys = [(score[h] * xh[h] + cs[h] + dvec_ref[h] * xh[h]) * _silu(zd[:, h * PD:(h + 1) * PD]) for h in hd]
    gw = HD * PD // NG
    for g in range(NG):
        yg = jnp.concatenate(ys[g * (HD // NG):(g + 1) * (HD // NG)], axis=-1)
        yn = yg * lax.rsqrt(jnp.mean(yg * yg, axis=-1, keepdims=True) + EPS)
        od_ref[0, :, g * gw:(g + 1) * gw] = (yn * sng_ref[:, g * gw:(g + 1) * gw]).astype(BF16)
    del l


def sample_mixers(proj3, layer, state_gdn, state_gdn_conv, state_gla, state_ssd, state_ssd_conv, prm):
    nb = proj3.shape[0]
    smem = pl.BlockSpec(memory_space=pltpu.SMEM)
    full = lambda shape: pl.BlockSpec(shape, lambda b: (0,) * len(shape))
    l = layer
    outs = pl.pallas_call(
        _sample_mixers_kernel,
        grid=(nb,),
        in_specs=[smem, smem, smem, smem, smem,
                  pl.BlockSpec((1, 1, N_PROJ), lambda b: (b, 0, 0)),
                  pl.BlockSpec((1, 1, HA, DKA, DVA), lambda b: (l, b, 0, 0, 0)),
                  pl.BlockSpec((1, 1, CONV_W - 1, GDN_CONV_DIM), lambda b: (l, b, 0, 0)),
                  pl.BlockSpec((1, 1, HC, DKC, DVC), lambda b: (l, b, 0, 0, 0)),
                  pl.BlockSpec((1, 1, HD, PD, NSTATE), lambda b: (l, b, 0, 0, 0)),
                  pl.BlockSpec((1, 1, CONV_W - 1, SSD_CONV_DIM), lambda b: (l, b, 0, 0)),
                  full((CONV_W, GDN_CONV_DIM)), full((1, DVA)),
                  full((128, HC * DKC)), full((1, HC * DKC)), full((1, DVC)),
                  full((CONV_W, SSD_CONV_DIM)), full((1, SSD_CONV_DIM)), full((1, HD * PD))],
        out_specs=[pl.BlockSpec((1, 1, HA * DVA), lambda b: (b, 0, 0)),
                   pl.BlockSpec((1, 1, HC * DVC), lambda b: (b, 0, 0)),
                   pl.BlockSpec((1, 1, HD * PD), lambda b: (b, 0, 0)),
                   pl.BlockSpec((1, HA, DKA, DVA), lambda b: (b, 0, 0, 0)),
                   pl.BlockSpec((1, CONV_W - 1, GDN_CONV_DIM), lambda b: (b, 0, 0)),
                   pl.BlockSpec((1, HC, DKC, DVC), lambda b: (b, 0, 0, 0)),
                   pl.BlockSpec((1, HD, PD, NSTATE), lambda b: (b, 0, 0, 0)),
                   pl.BlockSpec((1, CONV_W - 1, SSD_CONV_DIM), lambda b: (b, 0, 0))],
        out_shape=[jax.ShapeDtypeStruct((nb, 1, HA * DVA), BF16),
                   jax.ShapeDtypeStruct((nb, 1, HC * DVC), BF16),
                   jax.ShapeDtypeStruct((nb, 1, HD * PD), BF16),
                   jax.ShapeDtypeStruct((nb, HA, DKA, DVA), F32),
                   jax.ShapeDtypeStruct((nb, CONV_W - 1, GDN_CONV_DIM), F32),
                   jax.ShapeDtypeStruct((nb, HC, DKC, DVC), F32),
                   jax.ShapeDtypeStruct((nb, HD, PD, NSTATE), F32),
                   jax.ShapeDtypeStruct((nb, CONV_W - 1, SSD_CONV_DIM), F32)],
        compiler_params=_cparams(("parallel",)),
        name="sample_mixers",
    )(prm["gdn_negA"], prm["gdn_dt_bias"], prm["ssd_negA"], prm["ssd_dt_bias"], prm["ssd_D"],
      proj3, state_gdn, state_gdn_conv, state_gla, state_ssd, state_ssd_conv,
      prm["gdn_conv_w"], prm["gdn_norm_g"].reshape(1, DVA),
      prm["w2pad"], prm["gla_b2"].reshape(1, HC * DKC), prm["gla_norm_g"].reshape(1, DVC),
      prm["ssd_conv_w"], prm["ssd_conv_b"].reshape(1, SSD_CONV_DIM),
      prm["ssd_norm_g"].reshape(1, HD * PD))
    return outs


SA_PAGES = 32


def _sample_attn_kernel(pt_ref, q_ref, brow_ref, *refs):
    k_refs = refs[0:SA_PAGES]
    v_refs = refs[SA_PAGES:2 * SA_PAGES]
    o_ref, acc_scr, run_scr = refs[2 * SA_PAGES:]
    j = pl.program_id(1)
    nj = pl.num_programs(1)

    @pl.when(j == 0)
    def _():
        acc_scr[...] = jnp.zeros_like(acc_scr)
        run_scr[...] = jnp.zeros_like(run_scr)

    w = PAGE_SIZE * HB
    q = q_ref[0]
    q8 = jnp.concatenate([q[:, h * DB:(h + 1) * DB] for h in range(HB)]
                         + [jnp.zeros((8 - HB, DB), F32)], axis=0).astype(BF16)
    rows = _iota2((8, w), 0)
    lanes = _iota2((8, w), 1)
    sel = (lanes % HB) == rows
    zs = []
    for p in range(SA_PAGES):
        kp = k_refs[p][0, 0].astype(BF16)
        zz = _dot_nt(q8, kp)
        zs.append(jnp.sum(jnp.where(sel, zz, 0.0), axis=0, keepdims=True))
    z = jnp.concatenate(zs, axis=0) * (DB ** -0.5) + brow_ref[...]
    sp = _softplus_log(z)
    lf = -sp
    plane = _iota2((SA_PAGES, w), 1)
    suf = jnp.where(plane < w - HB, pltpu.roll(lf, w - HB, 1), 0.0)
    tot = lf
    step = HB
    while step < w:
        suf = suf + jnp.where(plane < w - step, pltpu.roll(suf, w - step, 1), 0.0)
        tot = tot + pltpu.roll(tot, step, 1)
        step *= 2
    run = run_scr[0:1, :]
    runs = []
    for p in range(SA_PAGES):
        runs.append(run)
        run = run + tot[p:p + 1, :]
    run_scr[...] = jnp.broadcast_to(run, run_scr.shape)
    a = jnp.exp((z - sp) + (suf + jnp.concatenate(runs, axis=0)))
    acc = acc_scr[...]
    for p in range(SA_PAGES):
        vp = v_refs[p][0, 0].astype(BF16)
        ap = jnp.where(sel, jnp.broadcast_to(a[p:p + 1, :], (8, w)), 0.0).astype(BF16)
        acc = acc + _dot(ap, vp)
    acc_scr[...] = acc

    @pl.when(j == nj - 1)
    def _():
        o_ref[0] = acc[0:HB].astype(BF16)


def sample_attn(proj3, layer, cache_k4, cache_v4, page_table, bias):
    nb, npg = page_table.shape
    l = layer
    nj = npg // SA_PAGES

    def page_spec(p):
        return pl.BlockSpec((1, 1, PAGE_SIZE * HB, DB),
                            lambda b, j, pt: (l, pt[b, npg - 1 - (j * SA_PAGES + p)], 0, 0))

    grid_spec = pltpu.PrefetchScalarGridSpec(
        num_scalar_prefetch=1,
        grid=(nb, nj),
        in_specs=([pl.BlockSpec((1, 1, HB * DB), lambda b, j, pt: (b, 0, OFF_QB // (HB * DB))),
                   pl.BlockSpec((1, PAGE_SIZE * HB), lambda b, j, pt: (0, 0))]
                  + [page_spec(p) for p in range(SA_PAGES)]
                  + [page_spec(p) for p in range(SA_PAGES)]),
        out_specs=pl.BlockSpec((1, HB, DB), lambda b, j, pt: (b, 0, 0)),
        scratch_shapes=[pltpu.VMEM((8, DB), F32), pltpu.VMEM((8, PAGE_SIZE * HB), F32)],
    )
    brow = jnp.tile(bias.astype(F32), PAGE_SIZE).reshape(1, PAGE_SIZE * HB)
    return pl.pallas_call(
        _sample_attn_kernel,
        grid_spec=grid_spec,
        out_shape=jax.ShapeDtypeStruct((nb, HB, DB), BF16),
        compiler_params=_cparams(("parallel", "arbitrary")),
        name="sample_attn",
    )(page_table, proj3, brow, *([cache_k4] * SA_PAGES), *([cache_v4] * SA_PAGES))


def _out_proj_kernel(a_ref, b_ref, c_ref, d_ref, w_ref, x_ref, g1_ref, n2_ref, sc_ref, sh_ref,
                     xo_ref, h_ref):
    gw = GROUP_WIDTH
    acc = _dot(a_ref[...], w_ref[0:gw, :])
    acc = acc + _dot(b_ref[...], w_ref[gw:2 * gw, :])
    acc = acc + _dot(c_ref[...], w_ref[2 * gw:3 * gw, :])
    acc = acc + _dot(d_ref[...], w_ref[3 * gw:4 * gw, :])
    x = x_ref[...] + g1_ref[0] * acc
    xo_ref[...] = x
    y = x * lax.rsqrt(jnp.mean(x * x, axis=-1, keepdims=True) + EPS) * n2_ref[...]
    h_ref[...] = (y * (1.0 + sc_ref[0]) + sh_ref[0]).astype(BF16)


def out_proj(oa, ob, oc, od, w_out, x, g1, n2, sc2, sh2, rows_per_batch):
    t, d = x.shape
    tm = _row_tile(t, 512, g1, rows_per_batch)
    part = pl.BlockSpec((tm, GROUP_WIDTH), lambda i: (i, 0))
    return pl.pallas_call(
        _out_proj_kernel,
        grid=(t // tm,),
        in_specs=[part, part, part, part,
                  pl.BlockSpec((d, d), lambda i: (0, 0)),
                  pl.BlockSpec((tm, d), lambda i: (i, 0)),
                  _mod_spec(g1, tm, rows_per_batch),
                  pl.BlockSpec((1, d), lambda i: (0, 0)),
                  _mod_spec(sc2, tm, rows_per_batch),
                  _mod_spec(sh2, tm, rows_per_batch)],
        out_specs=[pl.BlockSpec((tm, d), lambda i: (i, 0)),
                   pl.BlockSpec((tm, d), lambda i: (i, 0))],
        out_shape=[jax.ShapeDtypeStruct((t, d), F32), jax.ShapeDtypeStruct((t, d), BF16)],
        compiler_params=_cparams(("parallel",), VMEM_LIMIT),
        name="out_proj",
    )(oa, ob, oc, od, w_out, x, g1, n2.reshape(1, d), sc2, sh2)


ROUTE_SUB = 128


def _topk_rows_many(vals, k, ids=None):
    if ids is None:
        ids = _iota2(vals[0].shape, 0).astype(F32)
    n = range(len(vals))
    out_v = [[] for _ in n]
    out_i = [[] for _ in n]
    for _ in range(k):
        m = [jnp.max(vals[i], axis=0, keepdims=True) for i in n]
        idx = [jnp.min(jnp.where(vals[i] == m[i], ids, 1e9), axis=0, keepdims=True) for i in n]
        vals = [jnp.where(ids == idx[i], -jnp.inf, vals[i]) for i in n]
        for i in n:
            out_v[i].append(m[i])
            out_i[i].append(idx[i])
    return [(jnp.concatenate(out_v[i], axis=0), jnp.concatenate(out_i[i], axis=0)) for i in n]


_CAND_ROWS = [(a, PEER_TOPK // (a + 1)) for a in range(PEER_TOPK)]
_N_CAND = sum(nb for _, nb in _CAND_ROWS)
_N_CAND_PAD = -(-_N_CAND // 8) * 8


def _cand_ids(n):
    r = _iota2((_N_CAND_PAD, n), 0)
    ids = jnp.full((_N_CAND_PAD, n), 1e9, F32)
    start = 0
    for a, nb in _CAND_ROWS:
        ids = jnp.where((r >= start) & (r < start + nb), (a * PEER_TOPK + r - start).astype(F32), ids)
        start += nb
    return ids


def _select_rows(table, sel):
    out = jnp.zeros(sel.shape, F32)
    for a in range(table.shape[0]):
        out = jnp.where(sel == float(a), table[a:a + 1, :], out)
    return out


def _peer_route_kernel(h_ref, wq_ref, sk_ref, e1_ref, e2_ref, gt_ref, q_scr, e1_scr, e2_scr, gt_scr):
    tm = h_ref.shape[0]
    q = _dot(h_ref[...], wq_ref[...])
    for cgrp in range(2 * PEER_HEADS):
        q_scr[cgrp] = q[:, cgrp * PK_DIM:(cgrp + 1) * PK_DIM]
    sk0 = sk_ref[0]
    sk1 = sk_ref[1]
    kk = PEER_TOPK

    n = min(tm, ROUTE_SUB)
    cand_ids = _cand_ids(n)
    subs = range(tm // n)

    def head_body(hd, _):
        s = []
        for sub in subs:
            r0 = sub * n
            s.append(_dot_nt(sk0, q_scr[2 * hd, r0:r0 + n, :]))
            s.append(_dot_nt(sk1, q_scr[2 * hd + 1, r0:r0 + n, :]))
        top = _topk_rows_many(s, kk)
        cand = []
        for sub in subs:
            sv0, sv1 = top[2 * sub][0], top[2 * sub + 1][0]
            cand.append(jnp.concatenate(
                [sv0[a:a + 1, :] + sv1[0:nb, :] for a, nb in _CAND_ROWS]
                + [jnp.full((_N_CAND_PAD - _N_CAND, n), -jnp.inf, F32)], axis=0))
        ctop = _topk_rows_many(cand, kk, cand_ids)
        ro = pl.multiple_of(hd * kk, kk)
        for sub in subs:
            r0 = sub * n
            cv, cidx = ctop[sub]
            ia = jnp.floor(cidx * (1.0 / kk))
            ib = cidx - ia * kk
            e1 = _select_rows(top[2 * sub][1], ia)
            e2 = _select_rows(top[2 * sub + 1][1], ib)
            ex = jnp.exp(cv - jnp.max(cv, axis=0, keepdims=True))
            gates = ex / jnp.sum(ex, axis=0, keepdims=True)
            e1_scr[pl.ds(ro, kk), r0:r0 + n] = e1
            e2_scr[pl.ds(ro, kk), r0:r0 + n] = e2
            gt_scr[pl.ds(ro, kk), r0:r0 + n] = gates
        return 0

    lax.fori_loop(0, PEER_HEADS, head_body, 0)

    e1_ref[...] = e1_scr[...].T
    e2_ref[...] = e2_scr[...].T
    gt_ref[...] = gt_scr[...].T


def peer_route(h2, wq, sub_keys):
    t, d = h2.shape
    tm = min(t, 512)
    nq = wq.shape[1]
    nj = PEER_HEADS * PEER_TOPK
    out = jax.ShapeDtypeStruct((t, nj), F32)
    ospec = pl.BlockSpec((tm, nj), lambda i: (i, 0))
    return pl.pallas_call(
        _peer_route_kernel,
        grid=(t // tm,),
        in_specs=[pl.BlockSpec((tm, d), lambda i: (i, 0)),
                  pl.BlockSpec((d, nq), lambda i: (0, 0)),
                  pl.BlockSpec((2, N_KEYS, PK_DIM), lambda i: (0, 0, 0))],
        out_specs=[ospec, ospec, ospec],
        out_shape=[out, out, out],
        scratch_shapes=[pltpu.VMEM((2 * PEER_HEADS, tm, PK_DIM), F32),
                        pltpu.VMEM((nj, tm), F32), pltpu.VMEM((nj, tm), F32),
                        pltpu.VMEM((nj, tm), F32)],
        compiler_params=_cparams(("parallel",), VMEM_LIMIT),
        name="peer_route",
    )(h2, wq, sub_keys)


GATE_GRP = 16


def _transpose8(vs):
    sub = _iota2(vs[0].shape, 0)
    vs = list(vs)
    for d in (4, 2, 1):
        keep = (sub & d) == 0
        out = list(vs)
        for i in range(8):
            if i & d:
                continue
            a, b = vs[i], vs[i + d]
            out[i] = jnp.where(keep, a, pltpu.roll(b, d, 0))
            out[i + d] = jnp.where(keep, pltpu.roll(a, 8 - d, 0), b)
        vs = out
    return vs


def _build_gates(e1_ref, e2_ref, gt_ref, g_scr, stage_scr):
    tm = e1_ref.shape[0]
    nj = PEER_HEADS * PEER_TOPK
    riota = _iota2((N_KEYS, nj), 0).astype(F32)

    half = GATE_GRP // 2

    def group(gi, _):
        t0 = pl.multiple_of(gi * GATE_GRP, GATE_GRP)
        for part in range(2):
            toks = [part * half + i for i in range(half)]
            e1 = [e1_ref[pl.ds(t0 + tt, 1), :] for tt in toks]
            e2 = [e2_ref[pl.ds(t0 + tt, 1), :] for tt in toks]
            g = [gt_ref[pl.ds(t0 + tt, 1), :] for tt in toks]
            pt = [jnp.where(riota == e1[i], g[i], 0.0).astype(BF16) for i in range(half)]
            qt = [jnp.where(riota == e2[i], 1.0, 0.0).astype(BF16) for i in range(half)]
            gm = [_dot_nt(pt[i], qt[i]) for i in range(half)]
            for i, tt in enumerate(toks):
                stage_scr[tt * N_KEYS:(tt + 1) * N_KEYS, :] = gm[i]

        def flush(eb, _):
            r0 = pl.multiple_of(eb * 8, 8)
            tiles = [stage_scr[pl.ds(tt * N_KEYS + r0, 8), :] for tt in range(GATE_GRP)]
            lo = _transpose8(tiles[0:8])
            hi = _transpose8(tiles[8:16])
            for r in range(8):
                rows = jnp.concatenate([lo[r], hi[r]], axis=0)
                g_scr[r0 + r, pl.ds(t0, GATE_GRP), :] = rows.astype(BF16)
            return 0

        lax.fori_loop(0, N_KEYS // 8, flush, 0)
        return 0

    lax.fori_loop(0, tm // GATE_GRP, group, 0)


PEER_TE = 1024
PEER_TE_SUB = 512


def _peer_dense_kernel(h_ref, u_ref, v_ref, e1_ref, e2_ref, gt_ref, x_ref, g2_ref, fg_ref, o_ref,
                       g_scr, stage_scr, *, final_norm):
    e = pl.program_id(1)
    ne = pl.num_programs(1)

    @pl.when(e == 0)
    def _():
        o_ref[...] = jnp.zeros_like(o_ref)
        _build_gates(e1_ref, e2_ref, gt_ref, g_scr, stage_scr)

    h = h_ref[...]
    part = None
    per_sub = PEER_TE_SUB // N_KEYS
    for c in range(PEER_TE // PEER_TE_SUB):
        rows = slice(c * PEER_TE_SUB, (c + 1) * PEER_TE_SUB)
        a = _dot_nt(h, u_ref[0, rows, :])
        act = a * (lax.erf(a * (0.5 ** 0.5)) + 1.0) * 0.5
        g0 = e * (PEER_TE // N_KEYS) + c * per_sub
        g = jnp.concatenate([g_scr[g0 + k] for k in range(per_sub)], axis=1)
        hh = (g.astype(F32) * act).astype(BF16)
        p = _dot(hh, v_ref[0, rows, :])
        part = p if part is None else part + p
    o_ref[...] += part

    @pl.when(e == ne - 1)
    def _():
        x = x_ref[...] + g2_ref[0] * o_ref[...]
        if final_norm:
            x = x * lax.rsqrt(jnp.mean(x * x, axis=-1, keepdims=True) + EPS) * fg_ref[...]
        o_ref[...] = x


def peer_dense(h2, u_all, v_all, layer, e1, e2, gt, x, g2, rows_per_batch, final_g, final_norm):
    t, d = x.shape
    tm = _row_tile(t, 512, g2, rows_per_batch)
    te = PEER_TE
    nj = PEER_HEADS * PEER_TOPK
    l = layer
    once = pl.Buffered(1)
    rspec = pl.BlockSpec((tm, nj), lambda i, e: (i, 0), pipeline_mode=once)
    return pl.pallas_call(
        functools.partial(_peer_dense_kernel, final_norm=final_norm),
        grid=(t // tm, N_EXPERTS // te),
        in_specs=[pl.BlockSpec((tm, d), lambda i, e: (i, 0), pipeline_mode=once),
                  pl.BlockSpec((1, te, d), lambda i, e: (l, e, 0)),
                  pl.BlockSpec((1, te, d), lambda i, e: (l, e, 0)),
                  rspec, rspec, rspec,
                  pl.BlockSpec((tm, d), lambda i, e: (i, 0), pipeline_mode=once),
                  _mod_spec(g2, tm, rows_per_batch),
                  pl.BlockSpec((1, d), lambda i, e: (0, 0))],
        out_specs=pl.BlockSpec((tm, d), lambda i, e: (i, 0)),
        out_shape=jax.ShapeDtypeStruct((t, d), F32),
        scratch_shapes=[pltpu.VMEM((N_KEYS, tm, N_KEYS), BF16),
                        pltpu.VMEM((GATE_GRP * N_KEYS, N_KEYS), F32)],
        compiler_params=_cparams(("parallel", "arbitrary"), VMEM_LIMIT),
        name="peer_dense",
    )(h2, u_all, v_all, e1, e2, gt, x, g2, final_g.reshape(1, d))


def _cast_kernel(x_ref, o_ref):
    o_ref[...] = x_ref[...].astype(BF16)


def cast_bf16(x):
    depth, n, d = x.shape
    tn = 1024
    return pl.pallas_call(
        _cast_kernel,
        grid=(depth, n // tn),
        in_specs=[pl.BlockSpec((1, tn, d), lambda l, i: (l, i, 0))],
        out_specs=pl.BlockSpec((1, tn, d), lambda l, i: (l, i, 0)),
        out_shape=jax.ShapeDtypeStruct(x.shape, BF16),
        compiler_params=_cparams(("parallel", "parallel"), VMEM_LIMIT),
        name="cast_bf16",
    )(x)


def _permute_w_in(w):
    d = w.shape[0]
    small = jnp.concatenate([w[:, 2048:2056], w[:, 5128:5144], w[:, 6680:6688]], axis=1)
    pad = jnp.zeros((d, N_PROJ - OFF_SMALL - small.shape[1]), w.dtype)
    return jnp.concatenate([w[:, 0:2048], w[:, 2056:5128], w[:, 5144:6680], small, pad],
                           axis=1).astype(BF16)


def _small_row(vals, off):
    return jnp.zeros((1, 128), F32).at[0, off:off + vals.shape[0]].set(vals.astype(F32))


def _layer_params(l, ada_w, ada_b, norm1_g, norm2_g, w_in, w_out, gdn_conv_w, gdn_A_log, gdn_dt_bias,
                  gdn_norm_g, sb_bias, gla_w2, gla_b2, gla_norm_g, ssd_conv_w, ssd_conv_b, ssd_A_log,
                  ssd_dt_bias, ssd_D, ssd_norm_g, peer_w_query, peer_sub_keys, peer_u, peer_v):
    del ada_w, ada_b
    w2pad = jnp.zeros((128, HC * DKC), F32).at[SM_G:SM_G + GLA_RANK, :].set(gla_w2[l])
    return {
        "norm1_g": norm1_g[l], "norm2_g": norm2_g[l],
        "w_in": _permute_w_in(w_in[l]), "w_out": w_out[l].astype(BF16),
        "gdn_conv_w": gdn_conv_w[l], "gdn_negA": -jnp.exp(gdn_A_log[l]), "gdn_dt_bias": gdn_dt_bias[l],
        "gdn_gcoef": _small_row(-jnp.exp(gdn_A_log[l]), SM_A),
        "gdn_dtb": _small_row(gdn_dt_bias[l], SM_A),
        "gdn_norm_g": gdn_norm_g[l], "sb_bias": sb_bias[l],
        "w2pad": w2pad, "gla_b2": gla_b2[l], "gla_norm_g": gla_norm_g[l],
        "ssd_conv_w": ssd_conv_w[l], "ssd_conv_b": ssd_conv_b[l],
        "ssd_negA": -jnp.exp(ssd_A_log[l]), "ssd_dt_bias": ssd_dt_bias[l],
        "ssd_arow": _small_row(-jnp.exp(ssd_A_log[l]), SM_DT),
        "ssd_dtb": _small_row(ssd_dt_bias[l], SM_DT),
        "ssd_D": ssd_D[l], "ssd_norm_g": ssd_norm_g[l],
        "wq": peer_w_query[l].astype(BF16), "sub_keys": peer_sub_keys[l],
        "peer_u": peer_u, "peer_v": peer_v, "layer": l,
    }


def _split_mod(mod):
    return [mod[:, i * D_MODEL:(i + 1) * D_MODEL] for i in range(6)]


def _peer_block(h2, x, g2, prm, rows_per_batch):
    e1, e2, gt = peer_route(h2, prm["wq"], prm["sub_keys"])
    return peer_dense(h2, prm["peer_u"], prm["peer_v"], prm["layer"], e1, e2, gt, x, g2, rows_per_batch,
                      prm["final_g"], prm["layer"] == DEPTH - 1)


def _peer_block_padded(h2, x, g2, prm):
    t = x.shape[0]
    tpad = -(-t // ROUTE_SUB) * ROUTE_SUB
    pad = lambda a: jnp.concatenate([a, jnp.zeros((tpad - t,) + a.shape[1:], a.dtype)], axis=0)
    g2p = pad(g2.reshape(t, -1)).reshape(1, tpad, -1)
    return _peer_block(pad(h2), pad(x), g2p, prm, 1)[:t]


def kernel(x_prompt, x_sample, cache_k, cache_v, state_gdn, state_gdn_conv, state_gla, state_ssd, state_ssd_conv, page_table, c_prompt, c_sample, ada_w, ada_b, norm1_g, norm2_g, w_in, w_out, gdn_conv_w, gdn_A_log, gdn_dt_bias, gdn_norm_g, sb_bias, gla_w2, gla_b2, gla_norm_g, ssd_conv_w, ssd_conv_b, ssd_A_log, ssd_dt_bias, ssd_D, ssd_norm_g, peer_w_query, peer_sub_keys, peer_u, peer_v, final_norm_g):
    bsz, seq, d = x_prompt.shape
    nb = x_sample.shape[0]
    tp = bsz * seq
    n_pool = cache_k.shape[1]
    cache_k4 = cache_k.reshape(DEPTH, n_pool, PAGE_SIZE * HB, DB)
    cache_v4 = cache_v.reshape(DEPTH, n_pool, PAGE_SIZE * HB, DB)

    n_c = bsz + nb
    r_pad = -(-n_c // 8) * 8
    c_all = jnp.concatenate([c_prompt, c_sample, jnp.zeros((r_pad - n_c, d), F32)], axis=0)
    mod = ada_mod(c_all, ada_w, ada_b)

    peer_u = cast_bf16(peer_u)
    peer_v = cast_bf16(peer_v)
    xp = x_prompt.reshape(tp, d)
    xs = x_sample.reshape(nb, d)
    outs_p, outs_s = [], []
    for l in range(DEPTH):
        prm = _layer_params(l, ada_w, ada_b, norm1_g, norm2_g, w_in, w_out, gdn_conv_w, gdn_A_log,
                            gdn_dt_bias, gdn_norm_g, sb_bias, gla_w2, gla_b2, gla_norm_g, ssd_conv_w,
                            ssd_conv_b, ssd_A_log, ssd_dt_bias, ssd_D, ssd_norm_g, peer_w_query,
                            peer_sub_keys, peer_u, peer_v)
        prm["final_g"] = final_norm_g
        mp = [m.reshape(bsz, 1, d) for m in _split_mod(mod[l, 0:bsz])]
        ms = [m.reshape(1, nb, d) for m in _split_mod(mod[l, bsz:bsz + nb])]

        proj = in_proj(xp, prm["norm1_g"], mp[1], mp[0], prm["w_in"], seq)
        oa, gdn_s = gdn_prompt(proj, bsz, seq, prm["gdn_conv_w"], prm["gdn_gcoef"], prm["gdn_dtb"],
                               prm["gdn_norm_g"])
        ob = sb_prompt(proj, bsz, seq, prm["sb_bias"])
        oc, gla_s = gla_prompt(proj, bsz, seq, prm["w2pad"], prm["gla_b2"], prm["gla_norm_g"])
        od, ssd_s = ssd_prompt(proj, bsz, seq, prm["ssd_conv_w"], prm["ssd_conv_b"], prm["ssd_arow"],
                               prm["ssd_dtb"], prm["ssd_D"], prm["ssd_norm_g"])
        p3 = proj.reshape(bsz, seq, N_PROJ)
        outs_p.append((p3[:, :, OFF_KB:OFF_KB + HB * DB].reshape(bsz, seq, HB, DB),
                       p3[:, :, OFF_VB:OFF_VB + HB * DB].reshape(bsz, seq, HB, DB),
                       gdn_s, p3[:, seq - (CONV_W - 1):, OFF_QKVA:OFF_QKVA + GDN_CONV_DIM],
                       gla_s, ssd_s, p3[:, seq - (CONV_W - 1):, OFF_XBC:OFF_XBC + SSD_CONV_DIM]))
        xp, h2 = out_proj(oa, ob, oc, od, prm["w_out"], xp, mp[2], prm["norm2_g"], mp[4], mp[3], seq)
        xp = _peer_block(h2, xp, mp[5], prm, seq)

        proj_s = in_proj(xs, prm["norm1_g"], ms[1], ms[0], prm["w_in"], 1)
        ps3 = proj_s.reshape(nb, 1, N_PROJ)
        (oa_s, oc_s, od_s, gdn_n, gbuf_n, gla_n, ssd_n, sbuf_n) = sample_mixers(
            ps3, l, state_gdn, state_gdn_conv, state_gla, state_ssd, state_ssd_conv, prm)
        ob_s = sample_attn(ps3, l, cache_k4, cache_v4, page_table, prm["sb_bias"])
        outs_s.append((proj_s[:, OFF_KB:OFF_KB + HB * DB].reshape(nb, 1, HB, DB),
                       proj_s[:, OFF_VB:OFF_VB + HB * DB].reshape(nb, 1, HB, DB),
                       gdn_n, gbuf_n, gla_n, ssd_n, sbuf_n))
        xs, h2s = out_proj(oa_s.reshape(nb, -1), ob_s.reshape(nb, -1), oc_s.reshape(nb, -1),
                           od_s.reshape(nb, -1), prm["w_out"], xs, ms[2], prm["norm2_g"], ms[4], ms[3], 1)
        xs = _peer_block_padded(h2s, xs, ms[5], prm)

    y_prompt = xp.reshape(bsz, seq, d)
    y_sample = xs.reshape(nb, 1, d)
    stk = lambda lst, i: jnp.stack([s[i] for s in lst], axis=0)
    return (y_prompt, y_sample, stk(outs_p, 0), stk(outs_p, 1), stk(outs_s, 0), stk(outs_s, 1),
            stk(outs_p, 2), stk(outs_s, 2), stk(outs_p, 3), stk(outs_s, 3), stk(outs_p, 4), stk(outs_s, 4),
            stk(outs_p, 5), stk(outs_s, 5), stk(outs_p, 6), stk(outs_s, 6))
```
